```python
import math
import jax, jax.numpy as jnp
from jax import lax
import numpy as np

D_MODEL = 1024
BATCH = 1
SEQ = 16384
DEPTH = 1
DEC_BATCH = 32
DEC_SEQ = 2048
PAST_LEN = 128

ATT_H = 8
ATT_KV = 2
ATT_G = ATT_H // ATT_KV
ATT_HD = 64
ATT_W = ATT_H * ATT_HD
ATT_KV_W = ATT_KV * ATT_HD
WINDOW = 128
BLK = 128
ATT_SCALE = 1.0 / math.sqrt(ATT_HD)
RW_H = 8
RW_N = 64
RW_W = RW_H * RW_N
DECAY_LORA = 64
AAA_LORA = 64
GATE_LORA = 160
RW_MIX_W = 3 * RW_W + 2 * DECAY_LORA + 2 * AAA_LORA + GATE_LORA
GATE_W = 2 * D_MODEL
IN_W = ATT_W + 2 * ATT_KV_W + RW_MIX_W + GATE_W
D_FF = 2816
NORM_EPS = 1e-6
GN_EPS = 64e-5

kernel_name = "hybrid_bidir_window_gqa_rwkv7_convglu"


def rms_norm(x, g):
    xf = x.astype(jnp.float32)
    y = xf * lax.rsqrt(jnp.mean(xf * xf, axis=-1, keepdims=True) + NORM_EPS)
    return (y * g.astype(jnp.float32)).astype(x.dtype)


def banded_gqa_alibi_sink(q, k, v, sink):
    B, T = q.shape[0], q.shape[1]
    nb = T // BLK
    kp = jnp.pad(k, ((0, 0), (BLK, BLK), (0, 0), (0, 0)))
    vp = jnp.pad(v, ((0, 0), (BLK, BLK), (0, 0), (0, 0)))
    slopes = jnp.exp2(-8.0 / ATT_H * jnp.arange(1, ATT_H + 1, dtype=jnp.float32)).reshape(ATT_KV, ATT_G)
    offs_q = jnp.arange(BLK)
    offs_k = jnp.arange(3 * BLK) - BLK
    dist = jnp.abs(offs_q[:, None] - offs_k[None, :])
    penalty = slopes[:, :, None, None] * dist.astype(jnp.float32)
    sink_b = sink.astype(jnp.float32).reshape(ATT_KV, ATT_G)[None, :, :, None, None]

    def block(i):
        start = i * BLK
        qb = lax.dynamic_slice_in_dim(q, start, BLK, axis=1)
        kb = lax.dynamic_slice_in_dim(kp, start, 3 * BLK, axis=1)
        vb = lax.dynamic_slice_in_dim(vp, start, 3 * BLK, axis=1)
        kpos = start - BLK + offs_k
        valid = (dist <= WINDOW) & ((kpos >= 0) & (kpos < T))[None, :]
        s = jnp.einsum('bqkgd,bskd->bkgqs', qb, kb).astype(jnp.float32) * ATT_SCALE - penalty
        s = jnp.where(valid, s, -jnp.inf)
        mx = jnp.maximum(jnp.max(s, axis=-1, keepdims=True), sink_b)
        p = jnp.exp(s - mx)
        den = jnp.sum(p, axis=-1, keepdims=True) + jnp.exp(sink_b - mx)
        return jnp.einsum('bkgqs,bskd->bqkgd', (p / den).astype(vb.dtype), vb)

    o = lax.map(block, jnp.arange(nb))
    return jnp.moveaxis(o, 0, 1).reshape(B, T, ATT_W)


def _heads(t):
    return t.reshape(t.shape[:-1] + (RW_H, RW_N))


def _scan_shared(t):
    tt = jnp.moveaxis(t, 1, 0)
    return jnp.stack([tt, tt[::-1]], axis=1)


def _scan_dir(t):
    tt = jnp.transpose(t, (1, 2, 0, 3, 4))
    return jnp.stack([tt[:, 0], tt[::-1, 1]], axis=1)


def _rwkv_step(S, inp):
    r, w, k, v, aa, bb = inp
    sa = jnp.einsum('dbhij,dbhj->dbhi', S, aa)
    S = S * w[..., None, :] + sa[..., :, None] * bb[..., None, :] + v[..., :, None] * k[..., None, :]
    y = jnp.einsum('dbhij,dbhj->dbhi', S, r)
    return S, y


def rwkv7_bidir(z, mu_prev, mu_next, w0, w2, a0, a2, g2, k_k, k_a, r_k, ln_w, ln_b):
    B, T, _ = z.shape
    z = z.astype(jnp.float32)
    zp = jnp.pad(z, ((0, 0), (1, 1), (0, 0)))
    z = z + mu_prev * (zp[:, :-2] - z) + mu_next * (zp[:, 2:] - z)
    o1 = RW_W; o2 = 2 * RW_W; o3 = 3 * RW_W
    o4 = o3 + 2 * DECAY_LORA; o5 = o4 + 2 * AAA_LORA
    r, k, v, wd, ad, gd = jnp.split(z, [o1, o2, o3, o4, o5], axis=-1)
    wd = wd.reshape(B, T, 2, DECAY_LORA)
    ad = ad.reshape(B, T, 2, AAA_LORA)
    w_log = -jax.nn.softplus(-(w0 + jnp.einsum('btdl,dlc->btdc', jnp.tanh(wd), w2))) - 0.5
    decay = jnp.exp(-jnp.exp(w_log))
    a = jax.nn.sigmoid(a0 + jnp.einsum('btdl,dlc->btdc', ad, a2))
    g = jnp.matmul(jax.nn.sigmoid(gd), g2)
    kk = _heads(k * k_k)
    kk = kk / jnp.maximum(jnp.sqrt(jnp.sum(kk * kk, axis=-1, keepdims=True)), 1e-12)
    k_dir = k[:, :, None, :] * (1.0 + (a - 1.0) * k_a)
    r_h, v_h = _heads(r), _heads(v)
    k_dir_h, decay_h, a_h = _heads(k_dir), _heads(decay), _heads(a)
    xs = (_scan_shared(r_h), _scan_dir(decay_h), _scan_dir(k_dir_h), _scan_shared(v_h),
          _scan_shared(-kk), _scan_dir(kk[:, :, None] * a_h))
    S0 = jnp.zeros((2, B, RW_H, RW_N, RW_N), jnp.float32)
    _, ys = lax.scan(_rwkv_step, S0, xs)
    y = jnp.moveaxis(ys[:, 0] + ys[::-1, 1], 0, 1)
    mu = jnp.mean(y, axis=-1, keepdims=True)
    var = jnp.mean(jnp.square(y - mu), axis=-1, keepdims=True)
    y = (y - mu) * lax.rsqrt(var + GN_EPS) * _heads(ln_w) + _heads(ln_b)
    bonus = jnp.sum(jnp.sum(r_h[:, :, None] * k_dir_h * r_k, axis=-1, keepdims=True), axis=2) * v_h
    return (y + bonus).reshape(B, T, RW_W) * g


def conv_glu_ffn(u, w_up, conv_w, conv_b, w_down):
    h = u @ w_up
    hp = jnp.pad(h, ((0, 0), (1, 1), (0, 0)))
    h = hp[:, :-2] * conv_w[0] + hp[:, 1:-1] * conv_w[1] + hp[:, 2:] * conv_w[2] + conv_b
    gate, up = jnp.split(h, 2, axis=-1)
    return (jax.nn.gelu(gate, approximate=True) * up) @ w_down


def encoder_layer(x, g_mix_pre, g_mix_post, g_ffn_pre, g_ffn_post, w_in, attn_sink,
                  rw_mu_prev, rw_mu_next, rw_w0, rw_w2, rw_a0, rw_a2, rw_g2, rw_k_k, rw_k_a,
                  rw_r_k, rw_ln_w, rw_ln_b, w_branch_attn, w_branch_rwkv, w_out,
                  w_ffn_up, ffn_conv_w, ffn_conv_b, w_ffn_down):
    B, T, _ = x.shape
    u = rms_norm(x, g_mix_pre)
    proj = u @ w_in
    c1 = ATT_W; c2 = c1 + ATT_KV_W; c3 = c2 + ATT_KV_W; c4 = c3 + RW_MIX_W
    q, k, v, z, gates = jnp.split(proj, [c1, c2, c3, c4], axis=-1)
    q = q.reshape(B, T, ATT_KV, ATT_G, ATT_HD)
    k = k.reshape(B, T, ATT_KV, ATT_HD)
    v = v.reshape(B, T, ATT_KV, ATT_HD)
    o_attn = banded_gqa_alibi_sink(q, k, v, attn_sink)
    o_rwkv = rwkv7_bidir(z, rw_mu_prev, rw_mu_next, rw_w0, rw_w2, rw_a0, rw_a2, rw_g2,
                         rw_k_k, rw_k_a, rw_r_k, rw_ln_w, rw_ln_b).astype(x.dtype)
    g_attn, g_rwkv = jnp.split(jax.nn.sigmoid(gates), 2, axis=-1)
    merged = g_attn * (o_attn @ w_branch_attn) + g_rwkv * (o_rwkv @ w_branch_rwkv)
    h = x + rms_norm(merged @ w_out, g_mix_post)
    f = conv_glu_ffn(rms_norm(h, g_ffn_pre), w_ffn_up, ffn_conv_w, ffn_conv_b, w_ffn_down)
    return h + rms_norm(f, g_ffn_post)


def setup_inputs(seed: int = 0) -> dict:
    key = jax.random.key(seed)
    ks = jax.random.split(key, 32)
    L = DEPTH
    f32 = jnp.float32

    def nrm(k, shape, scale):
        return jax.random.normal(k, shape, f32) * scale

    def unif(k, shape, lo, hi):
        return jax.random.uniform(k, shape, f32, minval=lo, maxval=hi)

    return {
        "x_prompt": nrm(ks[0], (BATCH, SEQ, D_MODEL), 1.0),
        "x_sample": nrm(ks[1], (DEC_BATCH, DEC_SEQ, D_MODEL), 1.0),
        "norm_mix_pre": 1.0 + nrm(ks[2], (L, D_MODEL), 0.02),
        "norm_mix_post": 1.0 + nrm(ks[3], (L, D_MODEL), 0.02),
        "norm_ffn_pre": 1.0 + nrm(ks[4], (L, D_MODEL), 0.02),
        "norm_ffn_post": 1.0 + nrm(ks[5], (L, D_MODEL), 0.02),
        "w_in": nrm(ks[6], (L, D_MODEL, IN_W), D_MODEL ** -0.5),
        "attn_sink": nrm(ks[7], (L, ATT_H), 0.5),
        "rw_mu_prev": unif(ks[8], (L, RW_MIX_W), 0.0, 0.5),
        "rw_mu_next": unif(ks[9], (L, RW_MIX_W), 0.0, 0.5),
        "rw_w0": unif(ks[10], (L, 2, RW_W), -5.0, -0.5),
        "rw_w2": nrm(ks[11], (L, 2, DECAY_LORA, RW_W), 0.1),
        "rw_a0": nrm(ks[12], (L, 2, RW_W), 0.1),
        "rw_a2": nrm(ks[13], (L, 2, AAA_LORA, RW_W), 0.5 * AAA_LORA ** -0.5),
        "rw_g2": nrm(ks[14], (L, GATE_LORA, RW_W), GATE_LORA ** -0.5),
        "rw_k_k": 0.85 + nrm(ks[15], (L, RW_W), 0.02),
        "rw_k_a": 1.0 + nrm(ks[16], (L, RW_W), 0.02),
        "rw_r_k": nrm(ks[17], (L, RW_H, RW_N), 0.1),
        "rw_ln_w": 1.0 + nrm(ks[18], (L, RW_W), 0.02),
        "rw_ln_b": nrm(ks[19], (L, RW_W), 0.02),
        "w_branch_attn": nrm(ks[20], (L, ATT_W, D_MODEL), ATT_W ** -0.5),
        "w_branch_rwkv": nrm(ks[21], (L, RW_W, D_MODEL), RW_W ** -0.5),
        "w_out": nrm(ks[22], (L, D_MODEL, D_MODEL), D_MODEL ** -0.5),
        "w_ffn_up": nrm(ks[23], (L, D_MODEL, 2 * D_FF), D_MODEL ** -0.5),
        "ffn_conv_w": nrm(ks[24], (L, 3, 2 * D_FF), 3.0 ** -0.5),
        "ffn_conv_b": nrm(ks[25], (L, 2 * D_FF), 0.02),
        "w_ffn_down": nrm(ks[26], (L, D_FF, D_MODEL), D_FF ** -0.5),
    }


def reference(x_prompt, x_sample, norm_mix_pre, norm_mix_post, norm_ffn_pre, norm_ffn_post,
              w_in, attn_sink, rw_mu_prev, rw_mu_next, rw_w0, rw_w2, rw_a0, rw_a2, rw_g2,
              rw_k_k, rw_k_a, rw_r_k, rw_ln_w, rw_ln_b, w_branch_attn, w_branch_rwkv, w_out,
              w_ffn_up, ffn_conv_w, ffn_conv_b, w_ffn_down):
    def run(x):
        h = x
        for l in range(DEPTH):
            h = encoder_layer(h, norm_mix_pre[l], norm_mix_post[l], norm_ffn_pre[l], norm_ffn_post[l],
                              w_in[l], attn_sink[l], rw_mu_prev[l], rw_mu_next[l], rw_w0[l], rw_w2[l],
                              rw_a0[l], rw_a2[l], rw_g2[l], rw_k_k[l], rw_k_a[l], rw_r_k[l],
                              rw_ln_w[l], rw_ln_b[l], w_branch_attn[l], w_branch_rwkv[l], w_out[l],
                              w_ffn_up[l], ffn_conv_w[l], ffn_conv_b[l], w_ffn_down[l])
        return h

    y_prompt = run(x_prompt)
    y_sample = run(x_sample)
    return (y_prompt, y_sample)
```

```python
import functools
import math

import numpy as np
import jax
import jax.numpy as jnp
from jax import lax
from jax.experimental import pallas as pl
from jax.experimental.pallas import tpu as pltpu

F32 = jnp.float32
BF16 = jnp.bfloat16

D_MODEL = 1024
ATT_H = 8
ATT_KV = 2
ATT_G = ATT_H // ATT_KV
ATT_HD = 64
ATT_W = ATT_H * ATT_HD
ATT_KV_W = ATT_KV * ATT_HD
WINDOW = 128
BLK = 128
ATT_SCALE = 1.0 / math.sqrt(ATT_HD)
RW_H = 8
RW_N = 64
RW_W = RW_H * RW_N
DECAY_LORA = 64
AAA_LORA = 64
GATE_LORA = 160
RW_LORA_W = 2 * DECAY_LORA + 2 * AAA_LORA + GATE_LORA
RW_MIX_W = 3 * RW_W + RW_LORA_W
GATE_W = 2 * D_MODEL
D_FF = 2816
NORM_EPS = 1e-6
GN_EPS = 64e-5

VMEM_LIMIT_BYTES = 56 * 2**20

CHUNK = 64
GROUP_H = 4
GROUP_W = GROUP_H * RW_N
N_GROUPS = RW_H // GROUP_H
RW_TILE = 256
HALO = 16

PROJ_TILE = 512
ATT_TILE = 256
MERGE_TILE = 512
FFN_TILE = 256
FFN_HALO = 8
FFN_COLS = 256


def _dot(a, b):
    return jnp.dot(a, b, preferred_element_type=F32)


def _dot_nt(a, b):
    return lax.dot_general(a, b, (((1,), (1,)), ((), ())), preferred_element_type=F32)


def _dot_tn(a, b):
    return lax.dot_general(a, b, (((0,), (0,)), ((), ())), preferred_element_type=F32)


def _sigmoid(x):
    return 1.0 / (1.0 + jnp.exp(-x))


def _rms(x, g):
    return x * lax.rsqrt(jnp.mean(x * x, axis=-1, keepdims=True) + NORM_EPS) * g


def _const_spec(shape):
    nd = len(shape)
    return pl.BlockSpec(shape, lambda *_: (0,) * nd, pipeline_mode=pl.Buffered(1))


def _params(n_axes):
    return pltpu.CompilerParams(dimension_semantics=("arbitrary",) * n_axes,
                                vmem_limit_bytes=VMEM_LIMIT_BYTES)


def _in_proj_kernel(x_ref, g_ref, wq_ref, wkv_ref, wrkv_ref, wzl_ref,
                    q_ref, kv_ref, rkv_ref, zl_ref):
    u = _rms(x_ref[...], g_ref[...]).astype(BF16)
    q_ref[...] = (_dot(u, wq_ref[...]) * ATT_SCALE).astype(BF16)
    kv_ref[...] = _dot(u, wkv_ref[...]).astype(BF16)
    rkv_ref[...] = _dot(u, wrkv_ref[...]).astype(BF16)
    zl_ref[...] = _dot(u, wzl_ref[...]).astype(BF16)


def _in_proj(x, g, wq, wkv, wrkv, wzl):
    n = x.shape[0]
    tm = PROJ_TILE
    row = lambda w: pl.BlockSpec((tm, w), lambda i: (i, 0))
    return pl.pallas_call(
        _in_proj_kernel,
        grid=(n // tm,),
        in_specs=[row(D_MODEL), _const_spec((1, D_MODEL)), _const_spec(wq.shape),
                  _const_spec(wkv.shape), _const_spec(wrkv.shape), _const_spec(wzl.shape)],
        out_specs=[row(ATT_W), row(2 * ATT_KV_W), row(3 * RW_W), row(RW_LORA_W)],
        out_shape=[jax.ShapeDtypeStruct((n, ATT_W), BF16),
                   jax.ShapeDtypeStruct((n, 2 * ATT_KV_W), BF16),
                   jax.ShapeDtypeStruct((n, 3 * RW_W), BF16),
                   jax.ShapeDtypeStruct((n, RW_LORA_W), BF16)],
        compiler_params=_params(1),
        name="in_proj",
    )(x, g, wq, wkv, wrkv, wzl)


def _attn_bias():
    slopes = np.exp2(-8.0 / ATT_H * np.arange(1, ATT_H + 1, dtype=np.float64))
    dist = np.abs(np.arange(BLK)[:, None] - (np.arange(3 * BLK)[None, :] - BLK))
    bias = -(slopes[:, None, None] * dist[None])
    bias = np.where(dist[None] <= WINDOW, bias, -np.inf)
    return jnp.asarray(bias.reshape(ATT_KV, ATT_G * BLK, 3 * BLK), F32)


def _attn_kernel(sink_ref, q_ref, kvp_ref, kvc_ref, kvn_ref, bias_ref, o_ref):
    j = pl.program_id(1)
    nj = pl.num_programs(1)
    has_next = jnp.where(j < nj - 1, 1.0, 0.0).astype(F32)
    kvwin = jnp.concatenate([kvp_ref[...].astype(F32), kvc_ref[...].astype(F32),
                             kvn_ref[...].astype(F32) * has_next], axis=0)
    lane128 = lax.broadcasted_iota(jnp.int32, kvwin[:, :ATT_KV_W].shape, 1)
    lane256 = lax.broadcasted_iota(jnp.int32, (BLK, ATT_G * ATT_HD), 1)
    col = lax.broadcasted_iota(jnp.int32, (1, 3 * BLK), 1)
    n_qb = ATT_TILE // BLK

    def lane_tiled(x, kv):
        swapped = pltpu.roll(x, ATT_HD, 1)
        pair = jnp.where((lane128 < ATT_HD) == (kv == 0), x, swapped)
        return jnp.concatenate([pair, pair], axis=1).astype(BF16)

    kt = [lane_tiled(kvwin[:, :ATT_KV_W], kv) for kv in range(ATT_KV)]
    vt = [lane_tiled(kvwin[:, ATT_KV_W:], kv) for kv in range(ATT_KV)]

    for qb in range(n_qb):
        col_ok = col >= jnp.maximum(2 - (j * n_qb + qb), 0) * BLK
        qrows = slice(qb * BLK, (qb + 1) * BLK)
        krows = slice(qb * BLK, qb * BLK + 3 * BLK)
        for kv in range(ATT_KV):
            lanes = slice(kv * ATT_G * ATT_HD, (kv + 1) * ATT_G * ATT_HD)
            qkv = q_ref[qrows, lanes].astype(F32)
            lhs = jnp.concatenate(
                [jnp.where(lane256 // ATT_HD == g, qkv, 0.0) for g in range(ATT_G)],
                axis=0).astype(BF16)
            s = _dot_nt(lhs, kt[kv][krows]) + bias_ref[kv]
            s = jnp.where(col_ok, s, -jnp.inf)
            sink = jnp.concatenate(
                [jnp.full((BLK, 1), sink_ref[kv * ATT_G + g], F32) for g in range(ATT_G)], axis=0)
            mx = jnp.maximum(jnp.max(s, axis=-1, keepdims=True), sink)
            p = jnp.exp(s - mx)
            den = jnp.sum(p, axis=-1, keepdims=True) + jnp.exp(sink - mx)
            pv = _dot(p.astype(BF16), vt[kv][krows]) * (1.0 / den)
            o = jnp.zeros((BLK, ATT_G * ATT_HD), F32)
            for g in range(ATT_G):
                o = o + jnp.where(lane256 // ATT_HD == g, pv[g * BLK:(g + 1) * BLK], 0.0)
            o_ref[qrows, lanes] = o.astype(BF16)


def _attention(q, kv, sink, bias, b, t):
    n = q.shape[0]
    tq = ATT_TILE
    nj = t // tq
    per = tq // BLK
    nblk = n // BLK

    def prev_map(i, j):
        return (jnp.maximum((i * nj + j) * per - 1, 0), 0)

    def next_map(i, j):
        return (jnp.minimum((i * nj + j + 1) * per, nblk - 1), 0)

    return pl.pallas_call(
        _attn_kernel,
        grid=(b, nj),
        in_specs=[pl.BlockSpec(memory_space=pltpu.SMEM),
                  pl.BlockSpec((tq, ATT_W), lambda i, j: (i * nj + j, 0)),
                  pl.BlockSpec((BLK, 2 * ATT_KV_W), prev_map),
                  pl.BlockSpec((tq, 2 * ATT_KV_W), lambda i, j: (i * nj + j, 0)),
                  pl.BlockSpec((BLK, 2 * ATT_KV_W), next_map),
                  _const_spec(bias.shape)],
        out_specs=pl.BlockSpec((tq, ATT_W), lambda i, j: (i * nj + j, 0)),
        out_shape=jax.ShapeDtypeStruct((n, ATT_W), BF16),
        compiler_params=_params(2),
        name="attention",
    )(sink, q, kv, kv, kv, bias)


def _block_diag(x, mask):
    tiled = jnp.concatenate([x.astype(F32)] * GROUP_H, axis=0)
    return jnp.where(mask, tiled, 0.0).astype(BF16)


def _shift_mix(c_ref, p_ref, n_ref, mu_ref, has_prev, has_next):
    c = c_ref[...].astype(F32)
    rows = c.shape[0]
    prow = p_ref[HALO - 1:HALO, :].astype(F32) * has_prev
    nrow = n_ref[0:1, :].astype(F32) * has_next
    rid = lax.broadcasted_iota(jnp.int32, c.shape, 0)
    up = jnp.where(rid == 0, prow, pltpu.roll(c, 1, 0))
    dn = jnp.where(rid == rows - 1, nrow, pltpu.roll(c, rows - 1, 0))
    return c + mu_ref[0:1, :] * (up - c) + mu_ref[1:2, :] * (dn - c)


def _rwkv_kernel(reverse, final, *refs):
    if final:
        (rkv_c, rkv_p, rkv_n, zl_c, zl_p, zl_n, mu_rkv, mu_zl, w0, w2, a0, a2, k_k, k_a, ones,
         y0, a0o, a2o, g2, r_k, ln_w, ln_b,
         out_ref, s_ref, at_s, rt_s, bt_s, kt_s, v_s, bp_s, kp_s, pc_s, y_s) = refs
    else:
        (rkv_c, rkv_p, rkv_n, zl_c, zl_p, zl_n, mu_rkv, mu_zl, w0, w2, a0, a2, k_k, k_a, ones,
         out_ref, s_ref, at_s, rt_s, bt_s, kt_s, v_s, bp_s, kp_s, pc_s, y_s) = refs

    j = pl.program_id(1)
    nj = pl.num_programs(1)
    tj = nj - 1 - j if reverse else j
    has_prev = jnp.where(tj > 0, 1.0, 0.0).astype(F32)
    has_next = jnp.where(tj < nj - 1, 1.0, 0.0).astype(F32)

    @pl.when(j == 0)
    def _():
        s_ref[...] = jnp.zeros_like(s_ref)

    zs = _shift_mix(rkv_c, rkv_p, rkv_n, mu_rkv, has_prev, has_next)
    zl = _shift_mix(zl_c, zl_p, zl_n, mu_zl, has_prev, has_next)
    r = zs[:, :RW_W]
    k = zs[:, RW_W:2 * RW_W]
    v = zs[:, 2 * RW_W:]
    wd = zl[:, :2 * DECAY_LORA]
    ad = zl[:, 2 * DECAY_LORA:2 * DECAY_LORA + 2 * AAA_LORA].astype(BF16)

    w = w0[...] + _dot(jnp.tanh(wd).astype(BF16), w2[...])
    lw = -math.exp(-0.5) * _sigmoid(w)
    a = _sigmoid(a0[...] + _dot(ad, a2[...]))
    kkk = k * k_k[...]
    ss = _dot((kkk * kkk).astype(BF16), ones[...])
    kk = kkk / jnp.maximum(jnp.sqrt(ss), 1e-12)
    kdir = k * (1.0 + (a - 1.0) * k_a[...])
    bb = kk * a

    rid = lax.broadcasted_iota(jnp.int32, (RW_TILE, RW_TILE), 0)
    cid = lax.broadcasted_iota(jnp.int32, (RW_TILE, RW_TILE), 1)
    tri = (cid >= rid) if reverse else (cid <= rid)
    lmat = jnp.where(jnp.logical_and(rid // CHUNK == cid // CHUNK, tri), 1.0, 0.0).astype(BF16)
    lw_hi = lw.astype(BF16)
    lw_lo = (lw - lw_hi.astype(F32)).astype(BF16)
    cum = _dot(lmat, lw_hi) + _dot(lmat, lw_lo)

    at_s[...] = (-kk * jnp.exp(cum - lw)).astype(BF16)
    rt_s[...] = (r * jnp.exp(cum)).astype(BF16)
    e_neg = jnp.exp(-cum)
    bt_s[...] = (bb * e_neg).astype(BF16)
    kt_s[...] = (kdir * e_neg).astype(BF16)
    v_s[...] = v.astype(BF16)
    for ci in range(RW_TILE // CHUNK):
        rows = slice(ci * CHUNK, (ci + 1) * CHUNK)
        end = ci * CHUNK if reverse else (ci + 1) * CHUNK - 1
        cum_end = cum[end:end + 1, :]
        e_end = jnp.exp(cum_end - cum[rows])
        bp_s[rows, :] = (bb[rows] * e_end).astype(BF16)
        kp_s[rows, :] = (kdir[rows] * e_end).astype(BF16)
        pc_s[ci:ci + 1, :] = jnp.exp(cum_end)

    trow = lax.broadcasted_iota(jnp.int32, (CHUNK, GROUP_W), 0)
    scol = lax.broadcasted_iota(jnp.int32, (CHUNK, GROUP_W), 1) % CHUNK
    strict = (scol > trow) if reverse else (scol < trow)
    incl = (scol >= trow) if reverse else (scol <= trow)
    eye = jnp.where(scol == trow, 1.0, 0.0).astype(F32)
    bdr = lax.broadcasted_iota(jnp.int32, (GROUP_H * CHUNK, GROUP_W), 0) // CHUNK
    bdc = lax.broadcasted_iota(jnp.int32, (GROUP_H * CHUNK, GROUP_W), 1) // RW_N
    bdmask = bdr == bdc
    bd = functools.partial(_block_diag, mask=bdmask)

    n_ch = RW_TILE // CHUNK
    for step in range(n_ch):
        ci = n_ch - 1 - step if reverse else step
        rows = slice(ci * CHUNK, (ci + 1) * CHUNK)
        for g in range(N_GROUPS):
            lanes = slice(g * GROUP_W, (g + 1) * GROUP_W)
            a_t = at_s[rows, lanes]
            r_t = rt_s[rows, lanes]
            v_c = v_s[rows, lanes]
            ar = jnp.concatenate([a_t, r_t], axis=0)
            g1 = _dot_nt(ar, bd(bt_s[rows, lanes]))
            g2_ = _dot_nt(ar, bd(kt_s[rows, lanes]))
            ab = jnp.where(strict, g1[:CHUNK], 0.0)
            rb = jnp.where(incl, g1[CHUNK:], 0.0)
            ak = jnp.where(strict, g2_[:CHUNK], 0.0)
            rk = jnp.where(incl, g2_[CHUNK:], 0.0)

            t_inv = eye + ab
            pw = _dot(ab.astype(BF16), bd(ab))
            n_lvl = int(math.log2(CHUNK))
            for lvl in range(1, n_lvl):
                bd_pw = bd(pw)
                if lvl < n_lvl - 1:
                    res = _dot(jnp.concatenate([pw, t_inv], axis=0).astype(BF16), bd_pw)
                    pw = res[:CHUNK]
                    t_inv = t_inv + res[CHUNK:]
                else:
                    t_inv = t_inv + _dot(t_inv.astype(BF16), bd_pw)

            bd_v = bd(v_c)
            s_bd = s_ref[g]
            a_s = _dot_nt(ar, s_bd.astype(BF16))
            u_in = a_s[:CHUNK] + _dot(ak.astype(BF16), bd_v)
            u = _dot(t_inv.astype(BF16), bd(u_in))
            y = a_s[CHUNK:] + _dot(jnp.concatenate([rb, rk], axis=1).astype(BF16),
                                   jnp.concatenate([bd(u), bd_v], axis=0))
            y_s[rows, lanes] = y
            uv = jnp.concatenate([u.astype(BF16), v_c], axis=0)
            bkp = jnp.concatenate([bp_s[rows, lanes], kp_s[rows, lanes]], axis=0)
            upd = _dot_tn(uv, bkp)
            s_ref[g] = s_bd * pc_s[ci:ci + 1, lanes] + jnp.where(bdmask, upd, 0.0)

    y = y_s[...]
    if not final:
        out_ref[...] = y
    else:
        ytot = y + y0[...]
        inv_n = 1.0 / RW_N
        mu = _dot(ytot.astype(BF16), ones[...]) * inv_n
        d = ytot - mu
        var = _dot((d * d).astype(BF16), ones[...]) * inv_n
        yn = d * lax.rsqrt(var + GN_EPS) * ln_w[...] + ln_b[...]
        a_other = _sigmoid(a0o[...] + _dot(ad, a2o[...]))
        ksum = k * (2.0 + (a + a_other - 2.0) * k_a[...])
        bonus = _dot((r * ksum * r_k[...]).astype(BF16), ones[...]) * v
        gd = zl[:, 2 * DECAY_LORA + 2 * AAA_LORA:]
        gate = _dot(_sigmoid(gd).astype(BF16), g2[...])
        out_ref[...] = ((yn + bonus) * gate).astype(BF16)


def _rwkv_pass(reverse, rkv, zl, consts, extra, b, t):
    n = rkv.shape[0]
    tb = RW_TILE
    nj = t // tb
    per = tb // HALO
    nhalo = n // HALO
    final = reverse

    def tile(i, j):
        return i * nj + (nj - 1 - j if reverse else j)

    def cur_map(i, j):
        return (tile(i, j), 0)

    def prev_map(i, j):
        return (jnp.maximum(tile(i, j) * per - 1, 0), 0)

    def next_map(i, j):
        return (jnp.minimum((tile(i, j) + 1) * per, nhalo - 1), 0)

    def stream(w):
        return [pl.BlockSpec((tb, w), cur_map), pl.BlockSpec((HALO, w), prev_map),
                pl.BlockSpec((HALO, w), next_map)]

    in_specs = stream(3 * RW_W) + stream(RW_LORA_W) + [_const_spec(c.shape) for c in consts]
    args = [rkv, rkv, rkv, zl, zl, zl] + list(consts)
    if final:
        y0 = extra[0]
        in_specs += [pl.BlockSpec((tb, RW_W), cur_map)] + [_const_spec(c.shape) for c in extra[1:]]
        args += list(extra)
    out_dtype = BF16 if final else F32
    act = lambda: pltpu.VMEM((tb, RW_W), BF16)
    return pl.pallas_call(
        functools.partial(_rwkv_kernel, reverse, final),
        grid=(b, nj),
        in_specs=in_specs,
        out_specs=pl.BlockSpec((tb, RW_W), cur_map),
        out_shape=jax.ShapeDtypeStruct((n, RW_W), out_dtype),
        scratch_shapes=[pltpu.VMEM((N_GROUPS, GROUP_W, GROUP_W), F32),
                        act(), act(), act(), act(), act(), act(), act(),
                        pltpu.VMEM((tb // CHUNK, RW_W), F32),
                        pltpu.VMEM((tb, RW_W), F32)],
        compiler_params=_params(2),
        name="rwkv_bwd" if reverse else "rwkv_fwd",
    )(*args)


def _merge_kernel(x_ref, oa_ref, ob_ref, gpre_ref, wg_ref, wa_ref, wb_ref, wo_ref, gpost_ref, h_ref):
    x = x_ref[...]
    u = _rms(x, gpre_ref[...]).astype(BF16)
    gates = _sigmoid(_dot(u, wg_ref[...]))
    merged = (gates[:, :D_MODEL] * _dot(oa_ref[...], wa_ref[...])
              + gates[:, D_MODEL:] * _dot(ob_ref[...], wb_ref[...]))
    m = _dot(merged.astype(BF16), wo_ref[...])
    h_ref[...] = x + _rms(m, gpost_ref[...])


def _merge(x, oa, ob, gpre, wg, wa, wb, wo, gpost):
    n = x.shape[0]
    tm = MERGE_TILE
    row = lambda w: pl.BlockSpec((tm, w), lambda i: (i, 0))
    consts = [gpre, wg, wa, wb, wo, gpost]
    return pl.pallas_call(
        _merge_kernel,
        grid=(n // tm,),
        in_specs=[row(D_MODEL), row(ATT_W), row(RW_W)] + [_const_spec(c.shape) for c in consts],
        out_specs=row(D_MODEL),
        out_shape=jax.ShapeDtypeStruct((n, D_MODEL), F32),
        compiler_params=_params(1),
        name="merge",
    )(x, oa, ob, *consts)


def _gelu_tanh(x):
    return 0.5 * x * (1.0 + jnp.tanh(math.sqrt(2.0 / math.pi) * (x + 0.044715 * (x * x * x))))


def _ffn_kernel(hc_ref, hp_ref, hn_ref, gpre_ref, wup_ref, cw_ref, cb_ref, wdn_ref, gpost_ref, o_ref):
    j = pl.program_id(1)
    nj = pl.num_programs(1)
    tm = FFN_TILE
    ext = tm + 2 * FFN_HALO
    hc = hc_ref[...]
    hx = jnp.concatenate([hp_ref[...], hc, hn_ref[...]], axis=0)
    rid = lax.broadcasted_iota(jnp.int32, (ext, 1), 0)
    keep = jnp.logical_and(jnp.logical_or(rid >= FFN_HALO, j > 0),
                           jnp.logical_or(rid < FFN_HALO + tm, j < nj - 1))
    u = jnp.where(keep, _rms(hx, gpre_ref[...]), 0.0).astype(BF16)
    acc = jnp.zeros((tm, D_MODEL), F32)
    for c0 in range(0, D_FF, FFN_COLS):
        halves = []
        for off in (c0, D_FF + c0):
            cols = slice(off, off + FFN_COLS)
            hh = _dot(u, wup_ref[:, cols])
            prev = pltpu.roll(hh, 1, 0)[FFN_HALO:FFN_HALO + tm]
            nxt = pltpu.roll(hh, ext - 1, 0)[FFN_HALO:FFN_HALO + tm]
            cur = hh[FFN_HALO:FFN_HALO + tm]
            halves.append(prev * cw_ref[0:1, cols] + cur * cw_ref[1:2, cols]
                          + nxt * cw_ref[2:3, cols] + cb_ref[:, cols])
        act = (_gelu_tanh(halves[0]) * halves[1]).astype(BF16)
        acc = acc + _dot(act, wdn_ref[c0:c0 + FFN_COLS, :])
    o_ref[...] = hc + _rms(acc, gpost_ref[...])


def _ffn(h, gpre, wup, cw, cb, wdn, gpost, b, t):
    n = h.shape[0]
    tm = FFN_TILE
    nj = t // tm
    per = tm // FFN_HALO
    nhalo = n // FFN_HALO

    def prev_map(i, j):
        return (jnp.maximum((i * nj + j) * per - 1, 0), 0)

    def next_map(i, j):
        return (jnp.minimum((i * nj + j + 1) * per, nhalo - 1), 0)

    consts = [gpre, wup, cw, cb, wdn, gpost]
    return pl.pallas_call(
        _ffn_kernel,
        grid=(b, nj),
        in_specs=[pl.BlockSpec((tm, D_MODEL), lambda i, j: (i * nj + j, 0)),
                  pl.BlockSpec((FFN_HALO, D_MODEL), prev_map),
                  pl.BlockSpec((FFN_HALO, D_MODEL), next_map)]
                 + [_const_spec(c.shape) for c in consts],
        out_specs=pl.BlockSpec((tm, D_MODEL), lambda i, j: (i * nj + j, 0)),
        out_shape=jax.ShapeDtypeStruct((n, D_MODEL), F32),
        compiler_params=_params(2),
        name="ffn",
    )(h, h, h, *consts)


def _prepare(norm_mix_pre, norm_mix_post, norm_ffn_pre, norm_ffn_post, w_in, attn_sink,
             rw_mu_prev, rw_mu_next, rw_w0, rw_w2, rw_a0, rw_a2, rw_g2, rw_k_k, rw_k_a,
             rw_r_k, rw_ln_w, rw_ln_b, w_branch_attn, w_branch_rwkv, w_out,
             w_ffn_up, ffn_conv_w, ffn_conv_b, w_ffn_down):
    c_q = ATT_W
    c_kv = c_q + 2 * ATT_KV_W
    c_rkv = c_kv + 3 * RW_W
    c_zl = c_rkv + RW_LORA_W
    row = lambda p: p.reshape(1, -1).astype(F32)

    def lora_pad(w2, d, n_lora):
        z = jnp.zeros((2 * n_lora, RW_W), F32)
        return z.at[d * n_lora:(d + 1) * n_lora].set(w2[d]).astype(BF16)

    mu = jnp.stack([rw_mu_prev, rw_mu_next]).astype(F32)
    head = np.arange(RW_W) // RW_N
    ones = jnp.asarray(head[:, None] == head[None, :], BF16)
    p = dict(
        g_mix_pre=row(norm_mix_pre), g_mix_post=row(norm_mix_post),
        g_ffn_pre=row(norm_ffn_pre), g_ffn_post=row(norm_ffn_post),
        wq=w_in[:, :c_q].astype(BF16), wkv=w_in[:, c_q:c_kv].astype(BF16),
        wrkv=w_in[:, c_kv:c_rkv].astype(BF16), wzl=w_in[:, c_rkv:c_zl].astype(BF16),
        wg=w_in[:, c_zl:].astype(BF16),
        sink=attn_sink.astype(F32), bias=_attn_bias(),
        mu_rkv=mu[:, :3 * RW_W], mu_zl=mu[:, 3 * RW_W:],
        w0=[row(rw_w0[d]) for d in range(2)],
        w2=[lora_pad(rw_w2, d, DECAY_LORA) for d in range(2)],
        a0=[row(rw_a0[d]) for d in range(2)],
        a2=[lora_pad(rw_a2, d, AAA_LORA) for d in range(2)],
        g2=rw_g2.astype(BF16), k_k=row(rw_k_k), k_a=row(rw_k_a), r_k=row(rw_r_k),
        ln_w=row(rw_ln_w), ln_b=row(rw_ln_b), ones=ones,
        wa=w_branch_attn.astype(BF16), wb=w_branch_rwkv.astype(BF16), wo=w_out.astype(BF16),
        wup=w_ffn_up.astype(BF16), cw=ffn_conv_w.astype(F32), cb=row(ffn_conv_b),
        wdn=w_ffn_down.astype(BF16),
    )
    return p


def _layer(x, p):
    b, t, _ = x.shape
    assert t % RW_TILE == 0 and t % ATT_TILE == 0 and t % FFN_TILE == 0
    assert (b * t) % PROJ_TILE == 0 and (b * t) % MERGE_TILE == 0
    x2 = x.reshape(b * t, D_MODEL)
    q, kv, rkv, zl = _in_proj(x2, p["g_mix_pre"], p["wq"], p["wkv"], p["wrkv"], p["wzl"])
    o_attn = _attention(q, kv, p["sink"], p["bias"], b, t)

    def consts(d):
        return [p["mu_rkv"], p["mu_zl"], p["w0"][d], p["w2"][d], p["a0"][d], p["a2"][d],
                p["k_k"], p["k_a"], p["ones"]]

    y_fwd = _rwkv_pass(False, rkv, zl, consts(0), None, b, t)
    o_rwkv = _rwkv_pass(True, rkv, zl, consts(1),
                        [y_fwd, p["a0"][0], p["a2"][0], p["g2"], p["r_k"], p["ln_w"], p["ln_b"]], b, t)
    h = _merge(x2, o_attn, o_rwkv, p["g_mix_pre"], p["wg"], p["wa"], p["wb"], p["wo"], p["g_mix_post"])
    out = _ffn(h, p["g_ffn_pre"], p["wup"], p["cw"], p["cb"], p["wdn"], p["g_ffn_post"], b, t)
    return out.reshape(b, t, D_MODEL)


def kernel(x_prompt, x_sample, norm_mix_pre, norm_mix_post, norm_ffn_pre, norm_ffn_post, w_in, attn_sink, rw_mu_prev, rw_mu_next, rw_w0, rw_w2, rw_a0, rw_a2, rw_g2, rw_k_k, rw_k_a, rw_r_k, rw_ln_w, rw_ln_b, w_branch_attn, w_branch_rwkv, w_out, w_ffn_up, ffn_conv_w, ffn_conv_b, w_ffn_down):
    weights = (norm_mix_pre, norm_mix_post, norm_ffn_pre, norm_ffn_post, w_in, attn_sink,
               rw_mu_prev, rw_mu_next, rw_w0, rw_w2, rw_a0, rw_a2, rw_g2, rw_k_k, rw_k_a,
               rw_r_k, rw_ln_w, rw_ln_b, w_branch_attn, w_branch_rwkv, w_out,
               w_ffn_up, ffn_conv_w, ffn_conv_b, w_ffn_down)
    depth = w_in.shape[0]
    layers = [_prepare(*(w[l] for w in weights)) for l in range(depth)]

    def run(x):
        for p in layers:
            x = _layer(x, p)
        return x

    return (run(x_prompt), run(x_sample))
```

```python
import functools
import math

import numpy as np
import jax
import jax.numpy as jnp
from jax import lax
from jax.experimental import pallas as pl
from jax.experimental.pallas import tpu as pltpu

F32 = jnp.float32
BF16 = jnp.bfloat16

D_MODEL = 1024
ATT_H = 8
ATT_KV = 2
ATT_G = ATT_H // ATT_KV
ATT_HD = 64
ATT_W = ATT_H * ATT_HD
ATT_KV_W = ATT_KV * ATT_HD
WINDOW = 128
BLK = 128
ATT_SCALE = 1.0 / math.sqrt(ATT_HD)
RW_H = 8
RW_N = 64
RW_W = RW_H * RW_N
DECAY_LORA = 64
AAA_LORA = 64
GATE_LORA = 160
RW_LORA_W = 2 * DECAY_LORA + 2 * AAA_LORA + GATE_LORA
RW_MIX_W = 3 * RW_W + RW_LORA_W
GATE_W = 2 * D_MODEL
D_FF = 2816
NORM_EPS = 1e-6
GN_EPS = 64e-5

VMEM_LIMIT_BYTES = 56 * 2**20

CHUNK = 64
GROUP_H = 4
GROUP_W = GROUP_H * RW_N
N_GROUPS = RW_H // GROUP_H
RW_TILE = 256
HALO = 16

PROJ_TILE = 512
ATT_TILE = 256
MERGE_TILE = 512
FFN_TILE = 256
FFN_HALO = 8
FFN_COLS = 256


def _dot(a, b):
    return jnp.dot(a, b, preferred_element_type=F32)


def _dot_nt(a, b):
    return lax.dot_general(a, b, (((1,), (1,)), ((), ())), preferred_element_type=F32)


def _dot_tn(a, b):
    return lax.dot_general(a, b, (((0,), (0,)), ((), ())), preferred_element_type=F32)


def _sigmoid(x):
    return 1.0 / (1.0 + jnp.exp(-x))


def _rms(x, g):
    return x * lax.rsqrt(jnp.mean(x * x, axis=-1, keepdims=True) + NORM_EPS) * g


def _const_spec(shape):
    nd = len(shape)
    return pl.BlockSpec(shape, lambda *_: (0,) * nd, pipeline_mode=pl.Buffered(1))


def _params(n_axes):
    return pltpu.CompilerParams(dimension_semantics=("arbitrary",) * n_axes,
                                vmem_limit_bytes=VMEM_LIMIT_BYTES)


def _in_proj_kernel(x_ref, g_ref, wq_ref, wkv_ref, wrkv_ref, wzl_ref,
                    q_ref, kv_ref, rkv_ref, zl_ref):
    u = _rms(x_ref[...], g_ref[...]).astype(BF16)
    q_ref[...] = (_dot(u, wq_ref[...]) * ATT_SCALE).astype(BF16)
    kv_ref[...] = _dot(u, wkv_ref[...]).astype(BF16)
    rkv_ref[...] = _dot(u, wrkv_ref[...]).astype(BF16)
    zl_ref[...] = _dot(u, wzl_ref[...]).astype(BF16)


def _in_proj(x, g, wq, wkv, wrkv, wzl):
    n = x.shape[0]
    tm = PROJ_TILE
    row = lambda w: pl.BlockSpec((tm, w), lambda i: (i, 0))
    return pl.pallas_call(
        _in_proj_kernel,
        grid=(n // tm,),
        in_specs=[row(D_MODEL), _const_spec((1, D_MODEL)), _const_spec(wq.shape),
                  _const_spec(wkv.shape), _const_spec(wrkv.shape), _const_spec(wzl.shape)],
        out_specs=[row(ATT_W), row(2 * ATT_KV_W), row(3 * RW_W), row(RW_LORA_W)],
        out_shape=[jax.ShapeDtypeStruct((n, ATT_W), BF16),
                   jax.ShapeDtypeStruct((n, 2 * ATT_KV_W), BF16),
                   jax.ShapeDtypeStruct((n, 3 * RW_W), BF16),
                   jax.ShapeDtypeStruct((n, RW_LORA_W), BF16)],
        compiler_params=_params(1),
        name="in_proj",
    )(x, g, wq, wkv, wrkv, wzl)


def _attn_bias():
    slopes = np.exp2(-8.0 / ATT_H * np.arange(1, ATT_H + 1, dtype=np.float64))
    dist = np.abs(np.arange(BLK)[:, None] - (np.arange(3 * BLK)[None, :] - BLK))
    bias = -(slopes[:, None, None] * dist[None])
    bias = np.where(dist[None] <= WINDOW, bias, -np.inf)
    return jnp.asarray(bias.reshape(ATT_KV, ATT_G * BLK, 3 * BLK), F32)


def _attn_kernel(sink_ref, q_ref, kvp_ref, kvc_ref, kvn_ref, bias_ref, o_ref):
    j = pl.program_id(1)
    nj = pl.num_programs(1)
    has_next = jnp.where(j < nj - 1, 1.0, 0.0).astype(F32)
    kvwin = jnp.concatenate([kvp_ref[...].astype(F32), kvc_ref[...].astype(F32),
                             kvn_ref[...].astype(F32) * has_next], axis=0)
    lane128 = lax.broadcasted_iota(jnp.int32, kvwin[:, :ATT_KV_W].shape, 1)
    lane256 = lax.broadcasted_iota(jnp.int32, (BLK, ATT_G * ATT_HD), 1)
    col = lax.broadcasted_iota(jnp.int32, (1, 3 * BLK), 1)
    n_qb = ATT_TILE // BLK

    def lane_tiled(x, kv):
        swapped = pltpu.roll(x, ATT_HD, 1)
        pair = jnp.where((lane128 < ATT_HD) == (kv == 0), x, swapped)
        return jnp.concatenate([pair, pair], axis=1).astype(BF16)

    kt = [lane_tiled(kvwin[:, :ATT_KV_W], kv) for kv in range(ATT_KV)]
    vt = [lane_tiled(kvwin[:, ATT_KV_W:], kv) for kv in range(ATT_KV)]

    pairs = [(qb, kv) for qb in range(n_qb) for kv in range(ATT_KV)]
    every = range(len(pairs))
    head_of_lane = lane256 // ATT_HD

    def scores(x):
        qb, kv = pairs[x]
        qkv = q_ref[qb * BLK:(qb + 1) * BLK, kv * ATT_G * ATT_HD:(kv + 1) * ATT_G * ATT_HD].astype(F32)
        lhs = jnp.concatenate([jnp.where(head_of_lane == g, qkv, 0.0) for g in range(ATT_G)],
                              axis=0).astype(BF16)
        return _dot_nt(lhs, kt[kv][qb * BLK:qb * BLK + 3 * BLK])

    def softmax(x, s):
        qb, kv = pairs[x]
        col_ok = col >= jnp.maximum(2 - (j * n_qb + qb), 0) * BLK
        s = jnp.where(col_ok, s + bias_ref[kv], -jnp.inf)
        sink = jnp.concatenate(
            [jnp.full((BLK, 1), sink_ref[kv * ATT_G + g], F32) for g in range(ATT_G)], axis=0)
        mx = jnp.maximum(jnp.max(s, axis=-1, keepdims=True), sink)
        p = jnp.exp(s - mx)
        den = jnp.sum(p, axis=-1, keepdims=True) + jnp.exp(sink - mx)
        return p.astype(BF16), 1.0 / den

    def values(x, p, inv_den):
        qb, kv = pairs[x]
        pv = _dot(p, vt[kv][qb * BLK:qb * BLK + 3 * BLK]) * inv_den
        o = jnp.zeros((BLK, ATT_G * ATT_HD), F32)
        for g in range(ATT_G):
            o = o + jnp.where(head_of_lane == g, pv[g * BLK:(g + 1) * BLK], 0.0)
        o_ref[qb * BLK:(qb + 1) * BLK, kv * ATT_G * ATT_HD:(kv + 1) * ATT_G * ATT_HD] = o.astype(BF16)

    s_all = [scores(x) for x in every]
    p_all = [softmax(x, s_all[x]) for x in every]
    for x in every:
        values(x, *p_all[x])


def _attention(q, kv, sink, bias, b, t):
    n = q.shape[0]
    tq = ATT_TILE
    nj = t // tq
    per = tq // BLK
    nblk = n // BLK

    def prev_map(i, j):
        return (jnp.maximum((i * nj + j) * per - 1, 0), 0)

    def next_map(i, j):
        return (jnp.minimum((i * nj + j + 1) * per, nblk - 1), 0)

    return pl.pallas_call(
        _attn_kernel,
        grid=(b, nj),
        in_specs=[pl.BlockSpec(memory_space=pltpu.SMEM),
                  pl.BlockSpec((tq, ATT_W), lambda i, j: (i * nj + j, 0)),
                  pl.BlockSpec((BLK, 2 * ATT_KV_W), prev_map),
                  pl.BlockSpec((tq, 2 * ATT_KV_W), lambda i, j: (i * nj + j, 0)),
                  pl.BlockSpec((BLK, 2 * ATT_KV_W), next_map),
                  _const_spec(bias.shape)],
        out_specs=pl.BlockSpec((tq, ATT_W), lambda i, j: (i * nj + j, 0)),
        out_shape=jax.ShapeDtypeStruct((n, ATT_W), BF16),
        compiler_params=_params(2),
        name="attention",
    )(sink, q, kv, kv, kv, bias)


def _block_diag(x, mask):
    tiled = jnp.concatenate([x.astype(F32)] * GROUP_H, axis=0)
    return jnp.where(mask, tiled, 0.0).astype(BF16)


def _shift_mix(c_ref, p_ref, n_ref, mu_ref, has_prev, has_next):
    c = c_ref[...].astype(F32)
    rows = c.shape[0]
    prow = p_ref[HALO - 1:HALO, :].astype(F32) * has_prev
    nrow = n_ref[0:1, :].astype(F32) * has_next
    rid = lax.broadcasted_iota(jnp.int32, c.shape, 0)
    up = jnp.where(rid == 0, prow, pltpu.roll(c, 1, 0))
    dn = jnp.where(rid == rows - 1, nrow, pltpu.roll(c, rows - 1, 0))
    return c + mu_ref[0:1, :] * (up - c) + mu_ref[1:2, :] * (dn - c)


def _rwkv_kernel(reverse, final, *refs):
    if final:
        (rkv_c, rkv_p, rkv_n, zl_c, zl_p, zl_n, mu_rkv, mu_zl, w0, w2, a0, a2, k_k, k_a, ones,
         y0, a0o, a2o, g2, r_k, ln_w, ln_b,
         out_ref, s_ref, at_s, rt_s, bt_s, kt_s, v_s, bp_s, kp_s, pc_s, y_s) = refs
    else:
        (rkv_c, rkv_p, rkv_n, zl_c, zl_p, zl_n, mu_rkv, mu_zl, w0, w2, a0, a2, k_k, k_a, ones,
         out_ref, s_ref, at_s, rt_s, bt_s, kt_s, v_s, bp_s, kp_s, pc_s, y_s) = refs

    j = pl.program_id(1)
    nj = pl.num_programs(1)
    tj = nj - 1 - j if reverse else j
    has_prev = jnp.where(tj > 0, 1.0, 0.0).astype(F32)
    has_next = jnp.where(tj < nj - 1, 1.0, 0.0).astype(F32)

    @pl.when(j == 0)
    def _():
        s_ref[...] = jnp.zeros_like(s_ref)

    zs = _shift_mix(rkv_c, rkv_p, rkv_n, mu_rkv, has_prev, has_next)
    zl = _shift_mix(zl_c, zl_p, zl_n, mu_zl, has_prev, has_next)
    r = zs[:, :RW_W]
    k = zs[:, RW_W:2 * RW_W]
    v = zs[:, 2 * RW_W:]
    wd = zl[:, :2 * DECAY_LORA]
    ad = zl[:, 2 * DECAY_LORA:2 * DECAY_LORA + 2 * AAA_LORA].astype(BF16)

    w = w0[...] + _dot(jnp.tanh(wd).astype(BF16), w2[...])
    lw = -math.exp(-0.5) * _sigmoid(w)
    a = _sigmoid(a0[...] + _dot(ad, a2[...]))
    kkk = k * k_k[...]
    ss = _dot((kkk * kkk).astype(BF16), ones[...])
    kk = kkk / jnp.maximum(jnp.sqrt(ss), 1e-12)
    kdir = k * (1.0 + (a - 1.0) * k_a[...])
    bb = kk * a

    rid = lax.broadcasted_iota(jnp.int32, (RW_TILE, RW_TILE), 0)
    cid = lax.broadcasted_iota(jnp.int32, (RW_TILE, RW_TILE), 1)
    tri = (cid >= rid) if reverse else (cid <= rid)
    lmat = jnp.where(jnp.logical_and(rid // CHUNK == cid // CHUNK, tri), 1.0, 0.0).astype(BF16)
    lw_hi = lw.astype(BF16)
    lw_lo = (lw - lw_hi.astype(F32)).astype(BF16)
    cum = _dot(lmat, lw_hi) + _dot(lmat, lw_lo)

    at_s[...] = (-kk * jnp.exp(cum - lw)).astype(BF16)
    rt_s[...] = (r * jnp.exp(cum)).astype(BF16)
    e_neg = jnp.exp(-cum)
    bt_s[...] = (bb * e_neg).astype(BF16)
    kt_s[...] = (kdir * e_neg).astype(BF16)
    v_s[...] = v.astype(BF16)
    for ci in range(RW_TILE // CHUNK):
        rows = slice(ci * CHUNK, (ci + 1) * CHUNK)
        end = ci * CHUNK if reverse else (ci + 1) * CHUNK - 1
        cum_end = cum[end:end + 1, :]
        e_end = jnp.exp(cum_end - cum[rows])
        bp_s[rows, :] = (bb[rows] * e_end).astype(BF16)
        kp_s[rows, :] = (kdir[rows] * e_end).astype(BF16)
        pc_s[ci:ci + 1, :] = jnp.exp(cum_end)

    trow = lax.broadcasted_iota(jnp.int32, (CHUNK, GROUP_W), 0)
    scol = lax.broadcasted_iota(jnp.int32, (CHUNK, GROUP_W), 1) % CHUNK
    strict = (scol > trow) if reverse else (scol < trow)
    incl = (scol >= trow) if reverse else (scol <= trow)
    eye = jnp.where(scol == trow, 1.0, 0.0).astype(F32)
    bdr = lax.broadcasted_iota(jnp.int32, (GROUP_H * CHUNK, GROUP_W), 0) // CHUNK
    bdc = lax.broadcasted_iota(jnp.int32, (GROUP_H * CHUNK, GROUP_W), 1) // RW_N
    bdmask = bdr == bdc
    bd = functools.partial(_block_diag, mask=bdmask)

    n_ch = RW_TILE // CHUNK
    scan = [n_ch - 1 - s if reverse else s for s in range(n_ch)]
    pairs = [(ci, g) for ci in scan for g in range(N_GROUPS)]
    every = range(len(pairs))

    def ld(ref, x):
        ci, g = pairs[x]
        return ref[ci * CHUNK:(ci + 1) * CHUNK, g * GROUP_W:(g + 1) * GROUP_W]

    a_t = [ld(at_s, x) for x in every]
    r_t = [ld(rt_s, x) for x in every]
    v_c = [ld(v_s, x) for x in every]
    ar = [jnp.concatenate([a_t[x], r_t[x]], axis=0) for x in every]
    g1 = [_dot_nt(ar[x], bd(ld(bt_s, x))) for x in every]
    g2_ = [_dot_nt(ar[x], bd(ld(kt_s, x))) for x in every]
    ab = [jnp.where(strict, g1[x][:CHUNK], 0.0) for x in every]
    rb = [jnp.where(incl, g1[x][CHUNK:], 0.0) for x in every]
    ak = [jnp.where(strict, g2_[x][:CHUNK], 0.0) for x in every]
    rk = [jnp.where(incl, g2_[x][CHUNK:], 0.0) for x in every]

    t_inv = [eye + ab[x] for x in every]
    pw = [_dot(ab[x].astype(BF16), bd(ab[x])) for x in every]
    n_lvl = int(math.log2(CHUNK))
    for lvl in range(1, n_lvl):
        if lvl < n_lvl - 1:
            res = [_dot(jnp.concatenate([pw[x], t_inv[x]], axis=0).astype(BF16), bd(pw[x]))
                   for x in every]
            pw = [res[x][:CHUNK] for x in every]
            t_inv = [t_inv[x] + res[x][CHUNK:] for x in every]
        else:
            t_inv = [t_inv[x] + _dot(t_inv[x].astype(BF16), bd(pw[x])) for x in every]

    wv = [_dot(ak[x].astype(BF16), bd(v_c[x])) for x in every]
    atw = [_dot(t_inv[x].astype(BF16), jnp.concatenate([bd(a_t[x]), bd(wv[x])], axis=1))
           for x in every]
    a_p = [atw[x][:, :GROUP_W] for x in every]
    w_p = [atw[x][:, GROUP_W:] for x in every]
    b_p = [ld(bp_s, x) for x in every]
    m_bd = [jnp.where(bdmask, _dot_tn(a_p[x].astype(BF16), b_p[x]), 0.0).astype(BF16)
            for x in every]
    n_bd = [jnp.where(bdmask, _dot_tn(jnp.concatenate([w_p[x].astype(BF16), v_c[x]], axis=0),
                                      jnp.concatenate([b_p[x], ld(kp_s, x)], axis=0)), 0.0)
            for x in every]
    r_p = [r_t[x].astype(F32) + _dot(rb[x].astype(BF16), bd(a_p[x])) for x in every]
    y_i = [_dot(jnp.concatenate([rb[x], rk[x]], axis=1).astype(BF16),
                jnp.concatenate([bd(w_p[x]), bd(v_c[x])], axis=0)) for x in every]

    state = [s_ref[g] for g in range(N_GROUPS)]
    for x in every:
        ci, g = pairs[x]
        s_bf = state[g].astype(BF16)
        y_s[ci * CHUNK:(ci + 1) * CHUNK, g * GROUP_W:(g + 1) * GROUP_W] = (
            _dot_nt(r_p[x].astype(BF16), s_bf) + y_i[x])
        state[g] = (state[g] * pc_s[ci:ci + 1, g * GROUP_W:(g + 1) * GROUP_W]
                    + _dot(s_bf, m_bd[x]) + n_bd[x])
    for g in range(N_GROUPS):
        s_ref[g] = state[g]

    y = y_s[...]
    if not final:
        out_ref[...] = y
    else:
        ytot = y + y0[...]
        inv_n = 1.0 / RW_N
        mu = _dot(ytot.astype(BF16), ones[...]) * inv_n
        d = ytot - mu
        var = _dot((d * d).astype(BF16), ones[...]) * inv_n
        yn = d * lax.rsqrt(var + GN_EPS) * ln_w[...] + ln_b[...]
        a_other = _sigmoid(a0o[...] + _dot(ad, a2o[...]))
        ksum = k * (2.0 + (a + a_other - 2.0) * k_a[...])
        bonus = _dot((r * ksum * r_k[...]).astype(BF16), ones[...]) * v
        gd = zl[:, 2 * DECAY_LORA + 2 * AAA_LORA:]
        gate = _dot(_sigmoid(gd).astype(BF16), g2[...])
        out_ref[...] = ((yn + bonus) * gate).astype(BF16)


def _rwkv_pass(reverse, rkv, zl, consts, extra, b, t):
    n = rkv.shape[0]
    tb = RW_TILE
    nj = t // tb
    per = tb // HALO
    nhalo = n // HALO
    final = reverse

    def tile(i, j):
        return i * nj + (nj - 1 - j if reverse else j)

    def cur_map(i, j):
        return (tile(i, j), 0)

    def prev_map(i, j):
        return (jnp.maximum(tile(i, j) * per - 1, 0), 0)

    def next_map(i, j):
        return (jnp.minimum((tile(i, j) + 1) * per, nhalo - 1), 0)

    def stream(w):
        return [pl.BlockSpec((tb, w), cur_map), pl.BlockSpec((HALO, w), prev_map),
                pl.BlockSpec((HALO, w), next_map)]

    in_specs = stream(3 * RW_W) + stream(RW_LORA_W) + [_const_spec(c.shape) for c in consts]
    args = [rkv, rkv, rkv, zl, zl, zl] + list(consts)
    if final:
        y0 = extra[0]
        in_specs += [pl.BlockSpec((tb, RW_W), cur_map)] + [_const_spec(c.shape) for c in extra[1:]]
        args += list(extra)
    out_dtype = BF16 if final else F32
    act = lambda: pltpu.VMEM((tb, RW_W), BF16)
    return pl.pallas_call(
        functools.partial(_rwkv_kernel, reverse, final),
        grid=(b, nj),
        in_specs=in_specs,
        out_specs=pl.BlockSpec((tb, RW_W), cur_map),
        out_shape=jax.ShapeDtypeStruct((n, RW_W), out_dtype),
        scratch_shapes=[pltpu.VMEM((N_GROUPS, GROUP_W, GROUP_W), F32),
                        act(), act(), act(), act(), act(), act(), act(),
                        pltpu.VMEM((tb // CHUNK, RW_W), F32),
                        pltpu.VMEM((tb, RW_W), F32)],
        compiler_params=_params(2),
        name="rwkv_bwd" if reverse else "rwkv_fwd",
    )(*args)


def _merge_kernel(x_ref, oa_ref, ob_ref, gpre_ref, wg_ref, wa_ref, wb_ref, wo_ref, gpost_ref, h_ref):
    x = x_ref[...]
    u = _rms(x, gpre_ref[...]).astype(BF16)
    gates = _sigmoid(_dot(u, wg_ref[...]))
    merged = (gates[:, :D_MODEL] * _dot(oa_ref[...], wa_ref[...])
              + gates[:, D_MODEL:] * _dot(ob_ref[...], wb_ref[...]))
    m = _dot(merged.astype(BF16), wo_ref[...])
    h_ref[...] = x + _rms(m, gpost_ref[...])


def _merge(x, oa, ob, gpre, wg, wa, wb, wo, gpost):
    n = x.shape[0]
    tm = MERGE_TILE
    row = lambda w: pl.BlockSpec((tm, w), lambda i: (i, 0))
    consts = [gpre, wg, wa, wb, wo, gpost]
    return pl.pallas_call(
        _merge_kernel,
        grid=(n // tm,),
        in_specs=[row(D_MODEL), row(ATT_W), row(RW_W)] + [_const_spec(c.shape) for c in consts],
        out_specs=row(D_MODEL),
        out_shape=jax.ShapeDtypeStruct((n, D_MODEL), F32),
        compiler_params=_params(1),
        name="merge",
    )(x, oa, ob, *consts)


def _gelu_tanh(x):
    return 0.5 * x * (1.0 + jnp.tanh(math.sqrt(2.0 / math.pi) * (x + 0.044715 * (x * x * x))))


def _ffn_kernel(hc_ref, hp_ref, hn_ref, gpre_ref, wup_ref, cw_ref, cb_ref, wdn_ref, gpost_ref, o_ref):
    j = pl.program_id(1)
    nj = pl.num_programs(1)
    tm = FFN_TILE
    ext = tm + 2 * FFN_HALO
    hc = hc_ref[...]
    hx = jnp.concatenate([hp_ref[...], hc, hn_ref[...]], axis=0)
    rid = lax.broadcasted_iota(jnp.int32, (ext, 1), 0)
    keep = jnp.logical_and(jnp.logical_or(rid >= FFN_HALO, j > 0),
                           jnp.logical_or(rid < FFN_HALO + tm, j < nj - 1))
    u = jnp.where(keep, _rms(hx, gpre_ref[...]), 0.0).astype(BF16)
    def up(c0):
        return [_dot(u, wup_ref[:, off:off + FFN_COLS]) for off in (c0, D_FF + c0)]

    def conv(hh, off):
        cols = slice(off, off + FFN_COLS)
        prev = pltpu.roll(hh, 1, 0)[FFN_HALO:FFN_HALO + tm]
        nxt = pltpu.roll(hh, ext - 1, 0)[FFN_HALO:FFN_HALO + tm]
        cur = hh[FFN_HALO:FFN_HALO + tm]
        return (prev * cw_ref[0:1, cols] + cur * cw_ref[1:2, cols]
                + nxt * cw_ref[2:3, cols] + cb_ref[:, cols])

    acc = jnp.zeros((tm, D_MODEL), F32)
    starts = list(range(0, D_FF, FFN_COLS))
    pending = up(starts[0])
    for i, c0 in enumerate(starts):
        hh = pending
        if i + 1 < len(starts):
            pending = up(starts[i + 1])
        act = (_gelu_tanh(conv(hh[0], c0)) * conv(hh[1], D_FF + c0)).astype(BF16)
        acc = acc + _dot(act, wdn_ref[c0:c0 + FFN_COLS, :])
    o_ref[...] = hc + _rms(acc, gpost_ref[...])


def _ffn(h, gpre, wup, cw, cb, wdn, gpost, b, t):
    n = h.shape[0]
    tm = FFN_TILE
    nj = t // tm
    per = tm // FFN_HALO
    nhalo = n // FFN_HALO

    def prev_map(i, j):
        return (jnp.maximum((i * nj + j) * per - 1, 0), 0)

    def next_map(i, j):
        return (jnp.minimum((i * nj + j + 1) * per, nhalo - 1), 0)

    consts = [gpre, wup, cw, cb, wdn, gpost]
    return pl.pallas_call(
        _ffn_kernel,
        grid=(b, nj),
        in_specs=[pl.BlockSpec((tm, D_MODEL), lambda i, j: (i * nj + j, 0)),
                  pl.BlockSpec((FFN_HALO, D_MODEL), prev_map),
                  pl.BlockSpec((FFN_HALO, D_MODEL), next_map)]
                 + [_const_spec(c.shape) for c in consts],
        out_specs=pl.BlockSpec((tm, D_MODEL), lambda i, j: (i * nj + j, 0)),
        out_shape=jax.ShapeDtypeStruct((n, D_MODEL), F32),
        compiler_params=_params(2),
        name="ffn",
    )(h, h, h, *consts)


def _prepare(norm_mix_pre, norm_mix_post, norm_ffn_pre, norm_ffn_post, w_in, attn_sink,
             rw_mu_prev, rw_mu_next, rw_w0, rw_w2, rw_a0, rw_a2, rw_g2, rw_k_k, rw_k_a,
             rw_r_k, rw_ln_w, rw_ln_b, w_branch_attn, w_branch_rwkv, w_out,
             w_ffn_up, ffn_conv_w, ffn_conv_b, w_ffn_down):
    c_q = ATT_W
    c_kv = c_q + 2 * ATT_KV_W
    c_rkv = c_kv + 3 * RW_W
    c_zl = c_rkv + RW_LORA_W
    row = lambda p: p.reshape(1, -1).astype(F32)

    def lora_pad(w2, d, n_lora):
        z = jnp.zeros((2 * n_lora, RW_W), F32)
        return z.at[d * n_lora:(d + 1) * n_lora].set(w2[d]).astype(BF16)

    mu = jnp.stack([rw_mu_prev, rw_mu_next]).astype(F32)
    head = np.arange(RW_W) // RW_N
    ones = jnp.asarray(head[:, None] == head[None, :], BF16)
    p = dict(
        g_mix_pre=row(norm_mix_pre), g_mix_post=row(norm_mix_post),
        g_ffn_pre=row(norm_ffn_pre), g_ffn_post=row(norm_ffn_post),
        wq=w_in[:, :c_q].astype(BF16), wkv=w_in[:, c_q:c_kv].astype(BF16),
        wrkv=w_in[:, c_kv:c_rkv].astype(BF16), wzl=w_in[:, c_rkv:c_zl].astype(BF16),
        wg=w_in[:, c_zl:].astype(BF16),
        sink=attn_sink.astype(F32), bias=_attn_bias(),
        mu_rkv=mu[:, :3 * RW_W], mu_zl=mu[:, 3 * RW_W:],
        w0=[row(rw_w0[d]) for d in range(2)],
        w2=[lora_pad(rw_w2, d, DECAY_LORA) for d in range(2)],
        a0=[row(rw_a0[d]) for d in range(2)],
        a2=[lora_pad(rw_a2, d, AAA_LORA) for d in range(2)],
        g2=rw_g2.astype(BF16), k_k=row(rw_k_k), k_a=row(rw_k_a), r_k=row(rw_r_k),
        ln_w=row(rw_ln_w), ln_b=row(rw_ln_b), ones=ones,
        wa=w_branch_attn.astype(BF16), wb=w_branch_rwkv.astype(BF16), wo=w_out.astype(BF16),
        wup=w_ffn_up.astype(BF16), cw=ffn_conv_w.astype(F32), cb=row(ffn_conv_b),
        wdn=w_ffn_down.astype(BF16),
    )
    return p


def _layer(x, p):
    b, t, _ = x.shape
    assert t % RW_TILE == 0 and t % ATT_TILE == 0 and t % FFN_TILE == 0
    assert (b * t) % PROJ_TILE == 0 and (b * t) % MERGE_TILE == 0
    x2 = x.reshape(b * t, D_MODEL)
    q, kv, rkv, zl = _in_proj(x2, p["g_mix_pre"], p["wq"], p["wkv"], p["wrkv"], p["wzl"])
    o_attn = _attention(q, kv, p["sink"], p["bias"], b, t)

    def consts(d):
        return [p["mu_rkv"], p["mu_zl"], p["w0"][d], p["w2"][d], p["a0"][d], p["a2"][d],
                p["k_k"], p["k_a"], p["ones"]]

    y_fwd = _rwkv_pass(False, rkv, zl, consts(0), None, b, t)
    o_rwkv = _rwkv_pass(True, rkv, zl, consts(1),
                        [y_fwd, p["a0"][0], p["a2"][0], p["g2"], p["r_k"], p["ln_w"], p["ln_b"]], b, t)
    h = _merge(x2, o_attn, o_rwkv, p["g_mix_pre"], p["wg"], p["wa"], p["wb"], p["wo"], p["g_mix_post"])
    out = _ffn(h, p["g_ffn_pre"], p["wup"], p["cw"], p["cb"], p["wdn"], p["g_ffn_post"], b, t)
    return out.reshape(b, t, D_MODEL)


def kernel(x_prompt, x_sample, norm_mix_pre, norm_mix_post, norm_ffn_pre, norm_ffn_post, w_in, attn_sink, rw_mu_prev, rw_mu_next, rw_w0, rw_w2, rw_a0, rw_a2, rw_g2, rw_k_k, rw_k_a, rw_r_k, rw_ln_w, rw_ln_b, w_branch_attn, w_branch_rwkv, w_out, w_ffn_up, ffn_conv_w, ffn_conv_b, w_ffn_down):
    weights = (norm_mix_pre, norm_mix_post, norm_ffn_pre, norm_ffn_post, w_in, attn_sink,
               rw_mu_prev, rw_mu_next, rw_w0, rw_w2, rw_a0, rw_a2, rw_g2, rw_k_k, rw_k_a,
               rw_r_k, rw_ln_w, rw_ln_b, w_branch_attn, w_branch_rwkv, w_out,
               w_ffn_up, ffn_conv_w, ffn_conv_b, w_ffn_down)
    depth = w_in.shape[0]
    layers = [_prepare(*(w[l] for w in weights)) for l in range(depth)]

    def run(x):
        for p in layers:
            x = _layer(x, p)
        return x

    return (run(x_prompt), run(x_sample))
```

```python
import functools
import math

import numpy as np
import jax
import jax.numpy as jnp
from jax import lax
from jax.experimental import pallas as pl
from jax.experimental.pallas import tpu as pltpu

F32 = jnp.float32
BF16 = jnp.bfloat16

D_MODEL = 1024
ATT_H = 8
ATT_KV = 2
ATT_G = ATT_H // ATT_KV
ATT_HD = 64
ATT_W = ATT_H * ATT_HD
ATT_KV_W = ATT_KV * ATT_HD
WINDOW = 128
BLK = 128
ATT_SCALE = 1.0 / math.sqrt(ATT_HD)
RW_H = 8
RW_N = 64
RW_W = RW_H * RW_N
DECAY_LORA = 64
AAA_LORA = 64
GATE_LORA = 160
RW_LORA_W = 2 * DECAY_LORA + 2 * AAA_LORA + GATE_LORA
RW_MIX_W = 3 * RW_W + RW_LORA_W
GATE_W = 2 * D_MODEL
D_FF = 2816
NORM_EPS = 1e-6
GN_EPS = 64e-5

VMEM_LIMIT_BYTES = 56 * 2**20

CHUNK = 64
GROUP_H = 4
GROUP_W = GROUP_H * RW_N
N_GROUPS = RW_H // GROUP_H
RW_TILE = 512
HALO = 16

PROJ_TILE = 512
ATT_TILE = 256
MERGE_TILE = 512
FFN_TILE = 256
FFN_HALO = 8
FFN_COLS = 256


def _dot(a, b):
    return jnp.dot(a, b, preferred_element_type=F32)


def _dot_nt(a, b):
    return lax.dot_general(a, b, (((1,), (1,)), ((), ())), preferred_element_type=F32)


def _dot_tn(a, b):
    return lax.dot_general(a, b, (((0,), (0,)), ((), ())), preferred_element_type=F32)


def _sigmoid(x):
    return 1.0 / (1.0 + jnp.exp(-x))


def _rms(x, g):
    return x * lax.rsqrt(jnp.mean(x * x, axis=-1, keepdims=True) + NORM_EPS) * g


def _const_spec(shape):
    nd = len(shape)
    return pl.BlockSpec(shape, lambda *_: (0,) * nd, pipeline_mode=pl.Buffered(1))


def _params(n_axes):
    return pltpu.CompilerParams(dimension_semantics=("arbitrary",) * n_axes,
                                vmem_limit_bytes=VMEM_LIMIT_BYTES)


def _in_proj_kernel(x_ref, g_ref, wq_ref, wkv_ref, wrkv_ref, wzl_ref,
                    q_ref, kv_ref, rkv_ref, zl_ref):
    u = _rms(x_ref[...], g_ref[...]).astype(BF16)
    q_ref[...] = (_dot(u, wq_ref[...]) * ATT_SCALE).astype(BF16)
    kv_ref[...] = _dot(u, wkv_ref[...]).astype(BF16)
    rkv_ref[...] = _dot(u, wrkv_ref[...]).astype(BF16)
    zl_ref[...] = _dot(u, wzl_ref[...]).astype(BF16)


def _in_proj(x, g, wq, wkv, wrkv, wzl):
    n = x.shape[0]
    tm = PROJ_TILE
    row = lambda w: pl.BlockSpec((tm, w), lambda i: (i, 0))
    return pl.pallas_call(
        _in_proj_kernel,
        grid=(n // tm,),
        in_specs=[row(D_MODEL), _const_spec((1, D_MODEL)), _const_spec(wq.shape),
                  _const_spec(wkv.shape), _const_spec(wrkv.shape), _const_spec(wzl.shape)],
        out_specs=[row(ATT_W), row(2 * ATT_KV_W), row(3 * RW_W), row(RW_LORA_W)],
        out_shape=[jax.ShapeDtypeStruct((n, ATT_W), BF16),
                   jax.ShapeDtypeStruct((n, 2 * ATT_KV_W), BF16),
                   jax.ShapeDtypeStruct((n, 3 * RW_W), BF16),
                   jax.ShapeDtypeStruct((n, RW_LORA_W), BF16)],
        compiler_params=_params(1),
        name="in_proj",
    )(x, g, wq, wkv, wrkv, wzl)


def _attn_bias():
    slopes = np.exp2(-8.0 / ATT_H * np.arange(1, ATT_H + 1, dtype=np.float64))
    dist = np.abs(np.arange(BLK)[:, None] - (np.arange(3 * BLK)[None, :] - BLK))
    bias = -(slopes[:, None, None] * dist[None])
    bias = np.where(dist[None] <= WINDOW, bias, -np.inf)
    bias = bias.reshape(ATT_KV, ATT_G * BLK, 3 * BLK).transpose(0, 2, 1)
    return jnp.asarray(bias, F32)


def _attn_kernel(sink_ref, q_ref, kvp_ref, kvc_ref, kvn_ref, bias_ref, o_ref):
    j = pl.program_id(1)
    nj = pl.num_programs(1)
    has_next = jnp.where(j < nj - 1, 1.0, 0.0).astype(F32)
    kvwin = jnp.concatenate([kvp_ref[...].astype(F32), kvc_ref[...].astype(F32),
                             kvn_ref[...].astype(F32) * has_next], axis=0)
    kwin = kvwin[:, :ATT_KV_W].astype(BF16)
    v_t = kvwin[:, ATT_KV_W:].T.astype(BF16)
    krow = lax.broadcasted_iota(jnp.int32, (3 * BLK, 1), 0)
    n_qb = ATT_TILE // BLK
    gw = ATT_G * ATT_HD
    no_q = jnp.zeros((ATT_HD, ATT_G * BLK), F32)

    pairs = [(qb, kv) for qb in range(n_qb) for kv in range(ATT_KV)]
    every = range(len(pairs))

    def scores(x):
        qb, kv = pairs[x]
        q_t = q_ref[qb * BLK:(qb + 1) * BLK, kv * gw:(kv + 1) * gw].astype(F32).T
        q_t = jnp.concatenate([q_t[g * ATT_HD:(g + 1) * ATT_HD] for g in range(ATT_G)], axis=1)
        q_t = jnp.concatenate([q_t, no_q] if kv == 0 else [no_q, q_t], axis=0).astype(BF16)
        return _dot(kwin[qb * BLK:qb * BLK + 3 * BLK], q_t)

    def softmax(x, s):
        qb, kv = pairs[x]
        key_ok = krow >= jnp.maximum(2 - (j * n_qb + qb), 0) * BLK
        s = jnp.where(key_ok, s + bias_ref[kv], -jnp.inf)
        sink = jnp.concatenate(
            [jnp.full((1, BLK), sink_ref[kv * ATT_G + g], F32) for g in range(ATT_G)], axis=1)
        mx = jnp.maximum(jnp.max(s, axis=0, keepdims=True), sink)
        p = jnp.exp(s - mx)
        den = jnp.sum(p, axis=0, keepdims=True) + jnp.exp(sink - mx)
        return p.astype(BF16), 1.0 / den

    def values(x, p, inv_den):
        qb, kv = pairs[x]
        o_t = _dot(v_t[kv * ATT_HD:(kv + 1) * ATT_HD, qb * BLK:qb * BLK + 3 * BLK], p) * inv_den
        o_t = jnp.concatenate([o_t[:, g * BLK:(g + 1) * BLK] for g in range(ATT_G)], axis=0)
        o_ref[qb * BLK:(qb + 1) * BLK, kv * gw:(kv + 1) * gw] = o_t.T.astype(BF16)

    s_all = [scores(x) for x in every]
    p_all = [softmax(x, s_all[x]) for x in every]
    for x in every:
        values(x, *p_all[x])


def _attention(q, kv, sink, bias, b, t):
    n = q.shape[0]
    tq = ATT_TILE
    nj = t // tq
    per = tq // BLK
    nblk = n // BLK

    def prev_map(i, j):
        return (jnp.maximum((i * nj + j) * per - 1, 0), 0)

    def next_map(i, j):
        return (jnp.minimum((i * nj + j + 1) * per, nblk - 1), 0)

    return pl.pallas_call(
        _attn_kernel,
        grid=(b, nj),
        in_specs=[pl.BlockSpec(memory_space=pltpu.SMEM),
                  pl.BlockSpec((tq, ATT_W), lambda i, j: (i * nj + j, 0)),
                  pl.BlockSpec((BLK, 2 * ATT_KV_W), prev_map),
                  pl.BlockSpec((tq, 2 * ATT_KV_W), lambda i, j: (i * nj + j, 0)),
                  pl.BlockSpec((BLK, 2 * ATT_KV_W), next_map),
                  _const_spec(bias.shape)],
        out_specs=pl.BlockSpec((tq, ATT_W), lambda i, j: (i * nj + j, 0)),
        out_shape=jax.ShapeDtypeStruct((n, ATT_W), BF16),
        compiler_params=_params(2),
        name="attention",
    )(sink, q, kv, kv, kv, bias)


def _block_diag(x, mask):
    tiled = jnp.concatenate([x.astype(F32)] * GROUP_H, axis=0)
    return jnp.where(mask, tiled, 0.0).astype(BF16)


def _shift_mix(c_ref, p_ref, n_ref, mu_ref, has_prev, has_next):
    c = c_ref[...].astype(F32)
    rows = c.shape[0]
    prow = p_ref[HALO - 1:HALO, :].astype(F32) * has_prev
    nrow = n_ref[0:1, :].astype(F32) * has_next
    rid = lax.broadcasted_iota(jnp.int32, c.shape, 0)
    up = jnp.where(rid == 0, prow, pltpu.roll(c, 1, 0))
    dn = jnp.where(rid == rows - 1, nrow, pltpu.roll(c, rows - 1, 0))
    return c + mu_ref[0:1, :] * (up - c) + mu_ref[1:2, :] * (dn - c)


def _rwkv_kernel(reverse, final, *refs):
    if final:
        (rkv_c, rkv_p, rkv_n, zl_c, zl_p, zl_n, mu_rkv, mu_zl, w0, w2, a0, a2, k_k, k_a, ones,
         y0, a0o, a2o, g2, r_k, ln_w, ln_b,
         out_ref, s_ref, at_s, rt_s, bt_s, kt_s, v_s, bp_s, kp_s, pc_s, y_s) = refs
    else:
        (rkv_c, rkv_p, rkv_n, zl_c, zl_p, zl_n, mu_rkv, mu_zl, w0, w2, a0, a2, k_k, k_a, ones,
         out_ref, s_ref, at_s, rt_s, bt_s, kt_s, v_s, bp_s, kp_s, pc_s, y_s) = refs

    j = pl.program_id(1)
    nj = pl.num_programs(1)
    tj = nj - 1 - j if reverse else j
    has_prev = jnp.where(tj > 0, 1.0, 0.0).astype(F32)
    has_next = jnp.where(tj < nj - 1, 1.0, 0.0).astype(F32)

    @pl.when(j == 0)
    def _():
        s_ref[...] = jnp.zeros_like(s_ref)

    zs = _shift_mix(rkv_c, rkv_p, rkv_n, mu_rkv, has_prev, has_next)
    zl = _shift_mix(zl_c, zl_p, zl_n, mu_zl, has_prev, has_next)
    r = zs[:, :RW_W]
    k = zs[:, RW_W:2 * RW_W]
    v = zs[:, 2 * RW_W:]
    wd = zl[:, :2 * DECAY_LORA]
    ad = zl[:, 2 * DECAY_LORA:2 * DECAY_LORA + 2 * AAA_LORA].astype(BF16)

    w = w0[...] + _dot(jnp.tanh(wd).astype(BF16), w2[...])
    lw = -math.exp(-0.5) * _sigmoid(w)
    a = _sigmoid(a0[...] + _dot(ad, a2[...]))
    kkk = k * k_k[...]
    ss = _dot((kkk * kkk).astype(BF16), ones[...])
    kk = kkk / jnp.maximum(jnp.sqrt(ss), 1e-12)
    kdir = k * (1.0 + (a - 1.0) * k_a[...])
    bb = kk * a

    rid = lax.broadcasted_iota(jnp.int32, (RW_TILE, RW_TILE), 0)
    cid = lax.broadcasted_iota(jnp.int32, (RW_TILE, RW_TILE), 1)
    tri = (cid >= rid) if reverse else (cid <= rid)
    lmat = jnp.where(jnp.logical_and(rid // CHUNK == cid // CHUNK, tri), 1.0, 0.0).astype(BF16)
    lw_hi = lw.astype(BF16)
    lw_lo = (lw - lw_hi.astype(F32)).astype(BF16)
    cum = _dot(lmat, lw_hi) + _dot(lmat, lw_lo)

    at_s[...] = (-kk * jnp.exp(cum - lw)).astype(BF16)
    rt_s[...] = (r * jnp.exp(cum)).astype(BF16)
    e_neg = jnp.exp(-cum)
    bt_s[...] = (bb * e_neg).astype(BF16)
    kt_s[...] = (kdir * e_neg).astype(BF16)
    v_s[...] = v.astype(BF16)
    for ci in range(RW_TILE // CHUNK):
        rows = slice(ci * CHUNK, (ci + 1) * CHUNK)
        end = ci * CHUNK if reverse else (ci + 1) * CHUNK - 1
        cum_end = cum[end:end + 1, :]
        e_end = jnp.exp(cum_end - cum[rows])
        bp_s[rows, :] = (bb[rows] * e_end).astype(BF16)
        kp_s[rows, :] = (kdir[rows] * e_end).astype(BF16)
        pc_s[ci:ci + 1, :] = jnp.exp(cum_end)

    trow = lax.broadcasted_iota(jnp.int32, (CHUNK, GROUP_W), 0)
    scol = lax.broadcasted_iota(jnp.int32, (CHUNK, GROUP_W), 1) % CHUNK
    strict = (scol > trow) if reverse else (scol < trow)
    incl = (scol >= trow) if reverse else (scol <= trow)
    eye = jnp.where(scol == trow, 1.0, 0.0).astype(F32)
    bdr = lax.broadcasted_iota(jnp.int32, (GROUP_H * CHUNK, GROUP_W), 0) // CHUNK
    bdc = lax.broadcasted_iota(jnp.int32, (GROUP_H * CHUNK, GROUP_W), 1) // RW_N
    bdmask = bdr == bdc
    bd = functools.partial(_block_diag, mask=bdmask)

    n_ch = RW_TILE // CHUNK
    scan = [n_ch - 1 - s if reverse else s for s in range(n_ch)]
    pairs = [(ci, g) for ci in scan for g in range(N_GROUPS)]
    every = range(len(pairs))

    def ld(ref, x):
        ci, g = pairs[x]
        return ref[ci * CHUNK:(ci + 1) * CHUNK, g * GROUP_W:(g + 1) * GROUP_W]

    a_t = [ld(at_s, x) for x in every]
    r_t = [ld(rt_s, x) for x in every]
    v_c = [ld(v_s, x) for x in every]
    ar = [jnp.concatenate([a_t[x], r_t[x]], axis=0) for x in every]
    g1 = [_dot_nt(ar[x], bd(ld(bt_s, x))) for x in every]
    g2_ = [_dot_nt(ar[x], bd(ld(kt_s, x))) for x in every]
    ab = [jnp.where(strict, g1[x][:CHUNK], 0.0) for x in every]
    rb = [jnp.where(incl, g1[x][CHUNK:], 0.0) for x in every]
    ak = [jnp.where(strict, g2_[x][:CHUNK], 0.0) for x in every]
    rk = [jnp.where(incl, g2_[x][CHUNK:], 0.0) for x in every]

    t_inv = [eye + ab[x] for x in every]
    pw = [_dot(ab[x].astype(BF16), bd(ab[x])) for x in every]
    n_lvl = int(math.log2(CHUNK))
    for lvl in range(1, n_lvl):
        if lvl < n_lvl - 1:
            res = [_dot(jnp.concatenate([pw[x], t_inv[x]], axis=0).astype(BF16), bd(pw[x]))
                   for x in every]
            pw = [res[x][:CHUNK] for x in every]
            t_inv = [t_inv[x] + res[x][CHUNK:] for x in every]
        else:
            t_inv = [t_inv[x] + _dot(t_inv[x].astype(BF16), bd(pw[x])) for x in every]

    wv = [_dot(ak[x].astype(BF16), bd(v_c[x])) for x in every]
    atw = [_dot(t_inv[x].astype(BF16), jnp.concatenate([bd(a_t[x]), bd(wv[x])], axis=1))
           for x in every]
    a_p = [atw[x][:, :GROUP_W] for x in every]
    w_p = [atw[x][:, GROUP_W:] for x in every]
    b_p = [ld(bp_s, x) for x in every]
    m_bd = [jnp.where(bdmask, _dot_tn(a_p[x].astype(BF16), b_p[x]), 0.0).astype(BF16)
            for x in every]
    n_bd = [jnp.where(bdmask, _dot_tn(jnp.concatenate([w_p[x].astype(BF16), v_c[x]], axis=0),
                                      jnp.concatenate([b_p[x], ld(kp_s, x)], axis=0)), 0.0)
            for x in every]
    r_p = [r_t[x].astype(F32) + _dot(rb[x].astype(BF16), bd(a_p[x])) for x in every]
    y_i = [_dot(jnp.concatenate([rb[x], rk[x]], axis=1).astype(BF16),
                jnp.concatenate([bd(w_p[x]), bd(v_c[x])], axis=0)) for x in every]

    state = [s_ref[g] for g in range(N_GROUPS)]
    for x in every:
        ci, g = pairs[x]
        s_bf = state[g].astype(BF16)
        y_s[ci * CHUNK:(ci + 1) * CHUNK, g * GROUP_W:(g + 1) * GROUP_W] = (
            _dot_nt(r_p[x].astype(BF16), s_bf) + y_i[x])
        state[g] = (state[g] * pc_s[ci:ci + 1, g * GROUP_W:(g + 1) * GROUP_W]
                    + _dot(s_bf, m_bd[x]) + n_bd[x])
    for g in range(N_GROUPS):
        s_ref[g] = state[g]

    y = y_s[...]
    if not final:
        out_ref[...] = y
    else:
        ytot = y + y0[...]
        inv_n = 1.0 / RW_N
        mu = _dot(ytot.astype(BF16), ones[...]) * inv_n
        d = ytot - mu
        var = _dot((d * d).astype(BF16), ones[...]) * inv_n
        yn = d * lax.rsqrt(var + GN_EPS) * ln_w[...] + ln_b[...]
        a_other = _sigmoid(a0o[...] + _dot(ad, a2o[...]))
        ksum = k * (2.0 + (a + a_other - 2.0) * k_a[...])
        bonus = _dot((r * ksum * r_k[...]).astype(BF16), ones[...]) * v
        gd = zl[:, 2 * DECAY_LORA + 2 * AAA_LORA:]
        gate = _dot(_sigmoid(gd).astype(BF16), g2[...])
        out_ref[...] = ((yn + bonus) * gate).astype(BF16)


def _rwkv_pass(reverse, rkv, zl, consts, extra, b, t):
    n = rkv.shape[0]
    tb = RW_TILE
    nj = t // tb
    per = tb // HALO
    nhalo = n // HALO
    final = reverse

    def tile(i, j):
        return i * nj + (nj - 1 - j if reverse else j)

    def cur_map(i, j):
        return (tile(i, j), 0)

    def prev_map(i, j):
        return (jnp.maximum(tile(i, j) * per - 1, 0), 0)

    def next_map(i, j):
        return (jnp.minimum((tile(i, j) + 1) * per, nhalo - 1), 0)

    def stream(w):
        return [pl.BlockSpec((tb, w), cur_map), pl.BlockSpec((HALO, w), prev_map),
                pl.BlockSpec((HALO, w), next_map)]

    in_specs = stream(3 * RW_W) + stream(RW_LORA_W) + [_const_spec(c.shape) for c in consts]
    args = [rkv, rkv, rkv, zl, zl, zl] + list(consts)
    if final:
        y0 = extra[0]
        in_specs += [pl.BlockSpec((tb, RW_W), cur_map)] + [_const_spec(c.shape) for c in extra[1:]]
        args += list(extra)
    out_dtype = BF16 if final else F32
    act = lambda: pltpu.VMEM((tb, RW_W), BF16)
    return pl.pallas_call(
        functools.partial(_rwkv_kernel, reverse, final),
        grid=(b, nj),
        in_specs=in_specs,
        out_specs=pl.BlockSpec((tb, RW_W), cur_map),
        out_shape=jax.ShapeDtypeStruct((n, RW_W), out_dtype),
        scratch_shapes=[pltpu.VMEM((N_GROUPS, GROUP_W, GROUP_W), F32),
                        act(), act(), act(), act(), act(), act(), act(),
                        pltpu.VMEM((tb // CHUNK, RW_W), F32),
                        pltpu.VMEM((tb, RW_W), F32)],
        compiler_params=_params(2),
        name="rwkv_bwd" if reverse else "rwkv_fwd",
    )(*args)


def _merge_kernel(x_ref, oa_ref, ob_ref, gpre_ref, wg_ref, wa_ref, wb_ref, wo_ref, gpost_ref, h_ref):
    x = x_ref[...]
    u = _rms(x, gpre_ref[...]).astype(BF16)
    gates = _sigmoid(_dot(u, wg_ref[...]))
    merged = (gates[:, :D_MODEL] * _dot(oa_ref[...], wa_ref[...])
              + gates[:, D_MODEL:] * _dot(ob_ref[...], wb_ref[...]))
    m = _dot(merged.astype(BF16), wo_ref[...])
    h_ref[...] = x + _rms(m, gpost_ref[...])


def _merge(x, oa, ob, gpre, wg, wa, wb, wo, gpost):
    n = x.shape[0]
    tm = MERGE_TILE
    row = lambda w: pl.BlockSpec((tm, w), lambda i: (i, 0))
    consts = [gpre, wg, wa, wb, wo, gpost]
    return pl.pallas_call(
        _merge_kernel,
        grid=(n // tm,),
        in_specs=[row(D_MODEL), row(ATT_W), row(RW_W)] + [_const_spec(c.shape) for c in consts],
        out_specs=row(D_MODEL),
        out_shape=jax.ShapeDtypeStruct((n, D_MODEL), F32),
        compiler_params=_params(1),
        name="merge",
    )(x, oa, ob, *consts)


def _gelu_tanh(x):
    return 0.5 * x * (1.0 + jnp.tanh(math.sqrt(2.0 / math.pi) * (x + 0.044715 * (x * x * x))))


def _ffn_kernel(hc_ref, hp_ref, hn_ref, gpre_ref, wup_ref, cw_ref, cb_ref, wdn_ref, gpost_ref, o_ref):
    j = pl.program_id(1)
    nj = pl.num_programs(1)
    tm = FFN_TILE
    ext = tm + 2 * FFN_HALO
    hc = hc_ref[...]
    hx = jnp.concatenate([hp_ref[...], hc, hn_ref[...]], axis=0)
    rid = lax.broadcasted_iota(jnp.int32, (ext, 1), 0)
    keep = jnp.logical_and(jnp.logical_or(rid >= FFN_HALO, j > 0),
                           jnp.logical_or(rid < FFN_HALO + tm, j < nj - 1))
    u = jnp.where(keep, _rms(hx, gpre_ref[...]), 0.0).astype(BF16)

    def up(c0):
        return [_dot(u, wup_ref[:, off:off + FFN_COLS]) for off in (c0, D_FF + c0)]

    def conv(hh, off):
        cols = slice(off, off + FFN_COLS)
        prev = pltpu.roll(hh, 1, 0)[FFN_HALO:FFN_HALO + tm]
        nxt = pltpu.roll(hh, ext - 1, 0)[FFN_HALO:FFN_HALO + tm]
        cur = hh[FFN_HALO:FFN_HALO + tm]
        return (prev * cw_ref[0:1, cols] + cur * cw_ref[1:2, cols]
                + nxt * cw_ref[2:3, cols] + cb_ref[:, cols])

    acc = jnp.zeros((tm, D_MODEL), F32)
    starts = list(range(0, D_FF, FFN_COLS))
    pending = up(starts[0])
    for i, c0 in enumerate(starts):
        hh = pending
        if i + 1 < len(starts):
            pending = up(starts[i + 1])
        act = (_gelu_tanh(conv(hh[0], c0)) * conv(hh[1], D_FF + c0)).astype(BF16)
        acc = acc + _dot(act, wdn_ref[c0:c0 + FFN_COLS, :])
    o_ref[...] = hc + _rms(acc, gpost_ref[...])


def _ffn(h, gpre, wup, cw, cb, wdn, gpost, b, t):
    n = h.shape[0]
    tm = FFN_TILE
    nj = t // tm
    per = tm // FFN_HALO
    nhalo = n // FFN_HALO

    def prev_map(i, j):
        return (jnp.maximum((i * nj + j) * per - 1, 0), 0)

    def next_map(i, j):
        return (jnp.minimum((i * nj + j + 1) * per, nhalo - 1), 0)

    consts = [gpre, wup, cw, cb, wdn, gpost]
    return pl.pallas_call(
        _ffn_kernel,
        grid=(b, nj),
        in_specs=[pl.BlockSpec((tm, D_MODEL), lambda i, j: (i * nj + j, 0)),
                  pl.BlockSpec((FFN_HALO, D_MODEL), prev_map),
                  pl.BlockSpec((FFN_HALO, D_MODEL), next_map)]
                 + [_const_spec(c.shape) for c in consts],
        out_specs=pl.BlockSpec((tm, D_MODEL), lambda i, j: (i * nj + j, 0)),
        out_shape=jax.ShapeDtypeStruct((n, D_MODEL), F32),
        compiler_params=_params(2),
        name="ffn",
    )(h, h, h, *consts)


def _prepare(norm_mix_pre, norm_mix_post, norm_ffn_pre, norm_ffn_post, w_in, attn_sink,
             rw_mu_prev, rw_mu_next, rw_w0, rw_w2, rw_a0, rw_a2, rw_g2, rw_k_k, rw_k_a,
             rw_r_k, rw_ln_w, rw_ln_b, w_branch_attn, w_branch_rwkv, w_out,
             w_ffn_up, ffn_conv_w, ffn_conv_b, w_ffn_down):
    c_q = ATT_W
    c_kv = c_q + 2 * ATT_KV_W
    c_rkv = c_kv + 3 * RW_W
    c_zl = c_rkv + RW_LORA_W
    row = lambda p: p.reshape(1, -1).astype(F32)

    def lora_pad(w2, d, n_lora):
        z = jnp.zeros((2 * n_lora, RW_W), F32)
        return z.at[d * n_lora:(d + 1) * n_lora].set(w2[d]).astype(BF16)

    mu = jnp.stack([rw_mu_prev, rw_mu_next]).astype(F32)
    head = np.arange(RW_W) // RW_N
    ones = jnp.asarray(head[:, None] == head[None, :], BF16)
    p = dict(
        g_mix_pre=row(norm_mix_pre), g_mix_post=row(norm_mix_post),
        g_ffn_pre=row(norm_ffn_pre), g_ffn_post=row(norm_ffn_post),
        wq=w_in[:, :c_q].astype(BF16), wkv=w_in[:, c_q:c_kv].astype(BF16),
        wrkv=w_in[:, c_kv:c_rkv].astype(BF16), wzl=w_in[:, c_rkv:c_zl].astype(BF16),
        wg=w_in[:, c_zl:].astype(BF16),
        sink=attn_sink.astype(F32), bias=_attn_bias(),
        mu_rkv=mu[:, :3 * RW_W], mu_zl=mu[:, 3 * RW_W:],
        w0=[row(rw_w0[d]) for d in range(2)],
        w2=[lora_pad(rw_w2, d, DECAY_LORA) for d in range(2)],
        a0=[row(rw_a0[d]) for d in range(2)],
        a2=[lora_pad(rw_a2, d, AAA_LORA) for d in range(2)],
        g2=rw_g2.astype(BF16), k_k=row(rw_k_k), k_a=row(rw_k_a), r_k=row(rw_r_k),
        ln_w=row(rw_ln_w), ln_b=row(rw_ln_b), ones=ones,
        wa=w_branch_attn.astype(BF16), wb=w_branch_rwkv.astype(BF16), wo=w_out.astype(BF16),
        wup=w_ffn_up.astype(BF16), cw=ffn_conv_w.astype(F32), cb=row(ffn_conv_b),
        wdn=w_ffn_down.astype(BF16),
    )
    return p


def _layer(x, p):
    b, t, _ = x.shape
    assert t % RW_TILE == 0 and t % ATT_TILE == 0 and t % FFN_TILE == 0
    assert (b * t) % PROJ_TILE == 0 and (b * t) % MERGE_TILE == 0
    x2 = x.reshape(b * t, D_MODEL)
    q, kv, rkv, zl = _in_proj(x2, p["g_mix_pre"], p["wq"], p["wkv"], p["wrkv"], p["wzl"])
    o_attn = _attention(q, kv, p["sink"], p["bias"], b, t)

    def consts(d):
        return [p["mu_rkv"], p["mu_zl"], p["w0"][d], p["w2"][d], p["a0"][d], p["a2"][d],
                p["k_k"], p["k_a"], p["ones"]]

    y_fwd = _rwkv_pass(False, rkv, zl, consts(0), None, b, t)
    o_rwkv = _rwkv_pass(True, rkv, zl, consts(1),
                        [y_fwd, p["a0"][0], p["a2"][0], p["g2"], p["r_k"], p["ln_w"], p["ln_b"]], b, t)
    h = _merge(x2, o_attn, o_rwkv, p["g_mix_pre"], p["wg"], p["wa"], p["wb"], p["wo"], p["g_mix_post"])
    out = _ffn(h, p["g_ffn_pre"], p["wup"], p["cw"], p["cb"], p["wdn"], p["g_ffn_post"], b, t)
    return out.reshape(b, t, D_MODEL)


def kernel(x_prompt, x_sample, norm_mix_pre, norm_mix_post, norm_ffn_pre, norm_ffn_post, w_in, attn_sink, rw_mu_prev, rw_mu_next, rw_w0, rw_w2, rw_a0, rw_a2, rw_g2, rw_k_k, rw_k_a, rw_r_k, rw_ln_w, rw_ln_b, w_branch_attn, w_branch_rwkv, w_out, w_ffn_up, ffn_conv_w, ffn_conv_b, w_ffn_down):
    weights = (norm_mix_pre, norm_mix_post, norm_ffn_pre, norm_ffn_post, w_in, attn_sink,
               rw_mu_prev, rw_mu_next, rw_w0, rw_w2, rw_a0, rw_a2, rw_g2, rw_k_k, rw_k_a,
               rw_r_k, rw_ln_w, rw_ln_b, w_branch_attn, w_branch_rwkv, w_out,
               w_ffn_up, ffn_conv_w, ffn_conv_b, w_ffn_down)
    depth = w_in.shape[0]
    layers = [_prepare(*(w[l] for w in weights)) for l in range(depth)]

    def run(x):
        for p in layers:
            x = _layer(x, p)
        return x

    return (run(x_prompt), run(x_sample))
```

```python
import functools
import math

import numpy as np
import jax
import jax.numpy as jnp
from jax import lax
from jax.experimental import pallas as pl
from jax.experimental.pallas import tpu as pltpu

F32 = jnp.float32
BF16 = jnp.bfloat16

D_MODEL = 1024
ATT_H = 8
ATT_KV = 2
ATT_G = ATT_H // ATT_KV
ATT_HD = 64
ATT_W = ATT_H * ATT_HD
ATT_KV_W = ATT_KV * ATT_HD
WINDOW = 128
BLK = 128
ATT_SCALE = 1.0 / math.sqrt(ATT_HD)
RW_H = 8
RW_N = 64
RW_W = RW_H * RW_N
DECAY_LORA = 64
AAA_LORA = 64
GATE_LORA = 160
RW_LORA_W = 2 * DECAY_LORA + 2 * AAA_LORA + GATE_LORA
RW_MIX_W = 3 * RW_W + RW_LORA_W
GATE_W = 2 * D_MODEL
D_FF = 2816
NORM_EPS = 1e-6
GN_EPS = 64e-5

VMEM_LIMIT_BYTES = 56 * 2**20

CHUNK = 64
GROUP_H = 4
GROUP_W = GROUP_H * RW_N
N_GROUPS = RW_H // GROUP_H
RW_TILE = 512

PROJ_TILE = 512
PROJ_HALO = 8
ATT_TILE = 256
MERGE_TILE = 512
FFN_TILE = 256
FFN_HALO = 8
FFN_COLS = 256


def _dot(a, b):
    return jnp.dot(a, b, preferred_element_type=F32)


def _dot_nt(a, b):
    return lax.dot_general(a, b, (((1,), (1,)), ((), ())), preferred_element_type=F32)


def _dot_tn(a, b):
    return lax.dot_general(a, b, (((0,), (0,)), ((), ())), preferred_element_type=F32)


def _sigmoid(x):
    return 1.0 / (1.0 + jnp.exp(-x))


def _rms(x, g):
    return x * lax.rsqrt(jnp.mean(x * x, axis=-1, keepdims=True) + NORM_EPS) * g


def _const_spec(shape):
    nd = len(shape)
    return pl.BlockSpec(shape, lambda *_: (0,) * nd, pipeline_mode=pl.Buffered(1))


def _params(n_axes):
    return pltpu.CompilerParams(dimension_semantics=("arbitrary",) * n_axes,
                                vmem_limit_bytes=VMEM_LIMIT_BYTES)


def _in_proj_kernel(xc_ref, xp_ref, xn_ref, g_ref, wq_ref, wkv_ref, wrkv_ref, wzl_ref, mu_rkv_ref, mu_zl_ref,
                    q_ref, kv_ref, rkv_ref, zl_ref):
    j = pl.program_id(1)
    nj = pl.num_programs(1)
    tm = PROJ_TILE
    ext = tm + 2 * PROJ_HALO
    x = jnp.concatenate([xp_ref[...], xc_ref[...], xn_ref[...]], axis=0)
    rid = lax.broadcasted_iota(jnp.int32, (ext, 1), 0)
    keep = jnp.logical_and(jnp.logical_or(rid >= PROJ_HALO, j > 0),
                           jnp.logical_or(rid < PROJ_HALO + tm, j < nj - 1))
    u = jnp.where(keep, _rms(x, g_ref[...]), 0.0).astype(BF16)
    um = u[PROJ_HALO:PROJ_HALO + tm]
    q_ref[...] = (_dot(um, wq_ref[...]) * ATT_SCALE).astype(BF16)
    kv_ref[...] = _dot(um, wkv_ref[...]).astype(BF16)

    def shifted(w_ref, mu_ref, o_ref, cols):
        z = _dot(u, w_ref[:, cols])
        c = z[PROJ_HALO:PROJ_HALO + tm]
        up = pltpu.roll(z, 1, 0)[PROJ_HALO:PROJ_HALO + tm]
        dn = pltpu.roll(z, ext - 1, 0)[PROJ_HALO:PROJ_HALO + tm]
        o_ref[:, cols] = (c + mu_ref[0:1, cols] * (up - c) + mu_ref[1:2, cols] * (dn - c)).astype(BF16)

    for c0 in range(0, 3 * RW_W, RW_W):
        shifted(wrkv_ref, mu_rkv_ref, rkv_ref, slice(c0, c0 + RW_W))
    shifted(wzl_ref, mu_zl_ref, zl_ref, slice(0, RW_LORA_W))


def _in_proj(x, g, wq, wkv, wrkv, wzl, mu_rkv, mu_zl, b, t):
    n = x.shape[0]
    tm = PROJ_TILE
    nj = t // tm
    per = tm // PROJ_HALO
    nhalo = n // PROJ_HALO

    def prev_map(i, j):
        return (jnp.maximum((i * nj + j) * per - 1, 0), 0)

    def next_map(i, j):
        return (jnp.minimum((i * nj + j + 1) * per, nhalo - 1), 0)

    row = lambda w: pl.BlockSpec((tm, w), lambda i, j: (i * nj + j, 0))
    consts = [g, wq, wkv, wrkv, wzl, mu_rkv, mu_zl]
    return pl.pallas_call(
        _in_proj_kernel,
        grid=(b, nj),
        in_specs=[row(D_MODEL), pl.BlockSpec((PROJ_HALO, D_MODEL), prev_map),
                  pl.BlockSpec((PROJ_HALO, D_MODEL), next_map)] + [_const_spec(c.shape) for c in consts],
        out_specs=[row(ATT_W), row(2 * ATT_KV_W), row(3 * RW_W), row(RW_LORA_W)],
        out_shape=[jax.ShapeDtypeStruct((n, ATT_W), BF16),
                   jax.ShapeDtypeStruct((n, 2 * ATT_KV_W), BF16),
                   jax.ShapeDtypeStruct((n, 3 * RW_W), BF16),
                   jax.ShapeDtypeStruct((n, RW_LORA_W), BF16)],
        compiler_params=_params(2),
        name="in_proj",
    )(x, x, x, *consts)


def _attn_bias():
    slopes = np.exp2(-8.0 / ATT_H * np.arange(1, ATT_H + 1, dtype=np.float64))
    dist = np.abs(np.arange(BLK)[:, None] - (np.arange(3 * BLK)[None, :] - BLK))
    bias = -(slopes[:, None, None] * dist[None])
    bias = np.where(dist[None] <= WINDOW, bias, -np.inf)
    bias = bias.reshape(ATT_KV, ATT_G * BLK, 3 * BLK).transpose(0, 2, 1)
    return jnp.asarray(bias, F32)


def _attn_kernel(sink_ref, q_ref, kvp_ref, kvc_ref, kvn_ref, bias_ref, o_ref):
    j = pl.program_id(1)
    nj = pl.num_programs(1)
    has_next = jnp.where(j < nj - 1, 1.0, 0.0).astype(F32)
    kvwin = jnp.concatenate([kvp_ref[...].astype(F32), kvc_ref[...].astype(F32),
                             kvn_ref[...].astype(F32) * has_next], axis=0)
    kwin = kvwin[:, :ATT_KV_W].astype(BF16)
    v_t = kvwin[:, ATT_KV_W:].T.astype(BF16)
    krow = lax.broadcasted_iota(jnp.int32, (3 * BLK, 1), 0)
    n_qb = ATT_TILE // BLK
    gw = ATT_G * ATT_HD
    no_q = jnp.zeros((ATT_HD, ATT_G * BLK), F32)

    pairs = [(qb, kv) for qb in range(n_qb) for kv in range(ATT_KV)]
    every = range(len(pairs))

    def scores(x):
        qb, kv = pairs[x]
        q_t = q_ref[qb * BLK:(qb + 1) * BLK, kv * gw:(kv + 1) * gw].astype(F32).T
        q_t = jnp.concatenate([q_t[g * ATT_HD:(g + 1) * ATT_HD] for g in range(ATT_G)], axis=1)
        q_t = jnp.concatenate([q_t, no_q] if kv == 0 else [no_q, q_t], axis=0).astype(BF16)
        return _dot(kwin[qb * BLK:qb * BLK + 3 * BLK], q_t)

    def softmax(x, s):
        qb, kv = pairs[x]
        key_ok = krow >= jnp.maximum(2 - (j * n_qb + qb), 0) * BLK
        s = jnp.where(key_ok, s + bias_ref[kv], -jnp.inf)
        sink = jnp.concatenate(
            [jnp.full((1, BLK), sink_ref[kv * ATT_G + g], F32) for g in range(ATT_G)], axis=1)
        mx = jnp.maximum(jnp.max(s, axis=0, keepdims=True), sink)
        p = jnp.exp(s - mx)
        den = jnp.sum(p, axis=0, keepdims=True) + jnp.exp(sink - mx)
        return p.astype(BF16), 1.0 / den

    def values(x, p, inv_den):
        qb, kv = pairs[x]
        o_t = _dot(v_t[kv * ATT_HD:(kv + 1) * ATT_HD, qb * BLK:qb * BLK + 3 * BLK], p) * inv_den
        o_t = jnp.concatenate([o_t[:, g * BLK:(g + 1) * BLK] for g in range(ATT_G)], axis=0)
        o_ref[qb * BLK:(qb + 1) * BLK, kv * gw:(kv + 1) * gw] = o_t.T.astype(BF16)

    s_all = [scores(x) for x in every]
    p_all = [softmax(x, s_all[x]) for x in every]
    for x in every:
        values(x, *p_all[x])


def _attention(q, kv, sink, bias, b, t):
    n = q.shape[0]
    tq = ATT_TILE
    nj = t // tq
    per = tq // BLK
    nblk = n // BLK

    def prev_map(i, j):
        return (jnp.maximum((i * nj + j) * per - 1, 0), 0)

    def next_map(i, j):
        return (jnp.minimum((i * nj + j + 1) * per, nblk - 1), 0)

    return pl.pallas_call(
        _attn_kernel,
        grid=(b, nj),
        in_specs=[pl.BlockSpec(memory_space=pltpu.SMEM),
                  pl.BlockSpec((tq, ATT_W), lambda i, j: (i * nj + j, 0)),
                  pl.BlockSpec((BLK, 2 * ATT_KV_W), prev_map),
                  pl.BlockSpec((tq, 2 * ATT_KV_W), lambda i, j: (i * nj + j, 0)),
                  pl.BlockSpec((BLK, 2 * ATT_KV_W), next_map),
                  _const_spec(bias.shape)],
        out_specs=pl.BlockSpec((tq, ATT_W), lambda i, j: (i * nj + j, 0)),
        out_shape=jax.ShapeDtypeStruct((n, ATT_W), BF16),
        compiler_params=_params(2),
        name="attention",
    )(sink, q, kv, kv, kv, bias)


def _block_diag(x, mask):
    tiled = jnp.concatenate([x.astype(F32)] * GROUP_H, axis=0)
    return jnp.where(mask, tiled, 0.0).astype(BF16)


def _head_sum(x):
    lanes = 2 * RW_N
    low = lax.broadcasted_iota(jnp.int32, (x.shape[0], lanes), 1) < RW_N
    out = []
    for c0 in range(0, RW_W, lanes):
        xc = x[:, c0:c0 + lanes]
        lo = jnp.sum(jnp.where(low, xc, 0.0), axis=1, keepdims=True)
        hi = jnp.sum(jnp.where(low, 0.0, xc), axis=1, keepdims=True)
        out.append(jnp.where(low, lo, hi))
    return jnp.concatenate(out, axis=1)


def _rwkv_kernel(reverse, final, *refs):
    if final:
        (rkv_ref, zl_ref, w0, w2, a0, a2, k_k, k_a,
         y0, a0o, a2o, g2, r_k, ln_w, ln_b,
         out_ref, s_ref, at_s, rt_s, bt_s, kt_s, v_s, bp_s, kp_s, pc_s, y_s) = refs
    else:
        (rkv_ref, zl_ref, w0, w2, a0, a2, k_k, k_a,
         out_ref, s_ref, at_s, rt_s, bt_s, kt_s, v_s, bp_s, kp_s, pc_s, y_s) = refs

    j = pl.program_id(1)

    @pl.when(j == 0)
    def _():
        s_ref[...] = jnp.zeros_like(s_ref)

    r = rkv_ref[:, :RW_W].astype(F32)
    k = rkv_ref[:, RW_W:2 * RW_W].astype(F32)
    v_bf = rkv_ref[:, 2 * RW_W:]
    v = v_bf.astype(F32)
    wd = zl_ref[:, :2 * DECAY_LORA].astype(F32)
    ad = zl_ref[:, 2 * DECAY_LORA:2 * DECAY_LORA + 2 * AAA_LORA]

    w = w0[...] + _dot(jnp.tanh(wd).astype(BF16), w2[...])
    lw = -math.exp(-0.5) * _sigmoid(w)
    a = _sigmoid(a0[...] + _dot(ad, a2[...]))
    kkk = k * k_k[...]
    ss = _head_sum(kkk * kkk)
    kk = kkk / jnp.maximum(jnp.sqrt(ss), 1e-12)
    kdir = k * (1.0 + (a - 1.0) * k_a[...])
    bb = kk * a

    rid = lax.broadcasted_iota(jnp.int32, (CHUNK, CHUNK), 0)
    cid = lax.broadcasted_iota(jnp.int32, (CHUNK, CHUNK), 1)
    lmat = jnp.where((cid >= rid) if reverse else (cid <= rid), 1.0, 0.0).astype(BF16)
    lw_hi = lw.astype(BF16)
    lw_lo = (lw - lw_hi.astype(F32)).astype(BF16)
    cum = jnp.concatenate(
        [_dot(lmat, lw_hi[c0:c0 + CHUNK]) + _dot(lmat, lw_lo[c0:c0 + CHUNK])
         for c0 in range(0, RW_TILE, CHUNK)], axis=0)

    at_s[...] = (-kk * jnp.exp(cum - lw)).astype(BF16)
    rt_s[...] = (r * jnp.exp(cum)).astype(BF16)
    e_neg = jnp.exp(-cum)
    bt_s[...] = (bb * e_neg).astype(BF16)
    kt_s[...] = (kdir * e_neg).astype(BF16)
    v_s[...] = v_bf
    for ci in range(RW_TILE // CHUNK):
        rows = slice(ci * CHUNK, (ci + 1) * CHUNK)
        end = ci * CHUNK if reverse else (ci + 1) * CHUNK - 1
        cum_end = cum[end:end + 1, :]
        e_end = jnp.exp(cum_end - cum[rows])
        bp_s[rows, :] = (bb[rows] * e_end).astype(BF16)
        kp_s[rows, :] = (kdir[rows] * e_end).astype(BF16)
        pc_s[ci:ci + 1, :] = jnp.exp(cum_end)

    trow = lax.broadcasted_iota(jnp.int32, (CHUNK, GROUP_W), 0)
    scol = lax.broadcasted_iota(jnp.int32, (CHUNK, GROUP_W), 1) % CHUNK
    strict = (scol > trow) if reverse else (scol < trow)
    incl = (scol >= trow) if reverse else (scol <= trow)
    eye = jnp.where(scol == trow, 1.0, 0.0).astype(F32)
    bdr = lax.broadcasted_iota(jnp.int32, (GROUP_H * CHUNK, GROUP_W), 0) // CHUNK
    bdc = lax.broadcasted_iota(jnp.int32, (GROUP_H * CHUNK, GROUP_W), 1) // RW_N
    bdmask = bdr == bdc
    bd = functools.partial(_block_diag, mask=bdmask)

    n_ch = RW_TILE // CHUNK
    scan = [n_ch - 1 - s if reverse else s for s in range(n_ch)]
    pairs = [(ci, g) for ci in scan for g in range(N_GROUPS)]
    every = range(len(pairs))

    def ld(ref, x):
        ci, g = pairs[x]
        return ref[ci * CHUNK:(ci + 1) * CHUNK, g * GROUP_W:(g + 1) * GROUP_W]

    a_t = [ld(at_s, x) for x in every]
    r_t = [ld(rt_s, x) for x in every]
    v_c = [ld(v_s, x) for x in every]
    ar = [jnp.concatenate([a_t[x], r_t[x]], axis=0) for x in every]
    g1 = [_dot_nt(ar[x], bd(ld(bt_s, x))) for x in every]
    g2_ = [_dot_nt(ar[x], bd(ld(kt_s, x))) for x in every]
    ab = [jnp.where(strict, g1[x][:CHUNK], 0.0) for x in every]
    rb = [jnp.where(incl, g1[x][CHUNK:], 0.0) for x in every]
    ak = [jnp.where(strict, g2_[x][:CHUNK], 0.0) for x in every]
    rk = [jnp.where(incl, g2_[x][CHUNK:], 0.0) for x in every]

    t_inv = [eye + ab[x] for x in every]
    pw = [_dot(ab[x].astype(BF16), bd(ab[x])) for x in every]
    n_lvl = int(math.log2(CHUNK))
    for lvl in range(1, n_lvl):
        if lvl < n_lvl - 1:
            res = [_dot(jnp.concatenate([pw[x], t_inv[x]], axis=0).astype(BF16), bd(pw[x]))
                   for x in every]
            pw = [res[x][:CHUNK] for x in every]
            t_inv = [t_inv[x] + res[x][CHUNK:] for x in every]
        else:
            t_inv = [t_inv[x] + _dot(t_inv[x].astype(BF16), bd(pw[x])) for x in every]

    wv = [_dot(ak[x].astype(BF16), bd(v_c[x])) for x in every]
    atw = [_dot(t_inv[x].astype(BF16), jnp.concatenate([bd(a_t[x]), bd(wv[x])], axis=1))
           for x in every]
    a_p = [atw[x][:, :GROUP_W] for x in every]
    w_p = [atw[x][:, GROUP_W:] for x in every]
    b_p = [ld(bp_s, x) for x in every]
    m_bd = [jnp.where(bdmask, _dot_tn(a_p[x].astype(BF16), b_p[x]), 0.0).astype(BF16)
            for x in every]
    n_bd = [jnp.where(bdmask, _dot_tn(jnp.concatenate([w_p[x].astype(BF16), v_c[x]], axis=0),
                                      jnp.concatenate([b_p[x], ld(kp_s, x)], axis=0)), 0.0)
            for x in every]
    r_p = [r_t[x].astype(F32) + _dot(rb[x].astype(BF16), bd(a_p[x])) for x in every]
    y_i = [_dot(jnp.concatenate([rb[x], rk[x]], axis=1).astype(BF16),
                jnp.concatenate([bd(w_p[x]), bd(v_c[x])], axis=0)) for x in every]

    state = [s_ref[g] for g in range(N_GROUPS)]
    for x in every:
        ci, g = pairs[x]
        s_bf = state[g].astype(BF16)
        y_s[ci * CHUNK:(ci + 1) * CHUNK, g * GROUP_W:(g + 1) * GROUP_W] = (
            _dot_nt(r_p[x].astype(BF16), s_bf) + y_i[x])
        state[g] = (state[g] * pc_s[ci:ci + 1, g * GROUP_W:(g + 1) * GROUP_W]
                    + _dot(s_bf, m_bd[x]) + n_bd[x])
    for g in range(N_GROUPS):
        s_ref[g] = state[g]

    y = y_s[...]
    if not final:
        out_ref[...] = y
    else:
        ytot = y + y0[...]
        inv_n = 1.0 / RW_N
        mu = _head_sum(ytot) * inv_n
        d = ytot - mu
        var = _head_sum(d * d) * inv_n
        yn = d * lax.rsqrt(var + GN_EPS) * ln_w[...] + ln_b[...]
        a_other = _sigmoid(a0o[...] + _dot(ad, a2o[...]))
        ksum = k * (2.0 + (a + a_other - 2.0) * k_a[...])
        bonus = _head_sum(r * ksum * r_k[...]) * v
        gd = zl_ref[:, 2 * DECAY_LORA + 2 * AAA_LORA:].astype(F32)
        gate = _dot(_sigmoid(gd).astype(BF16), g2[...])
        out_ref[...] = ((yn + bonus) * gate).astype(BF16)


def _rwkv_pass(reverse, rkv, zl, consts, extra, b, t):
    n = rkv.shape[0]
    tb = RW_TILE
    nj = t // tb
    final = reverse

    def cur_map(i, j):
        return (i * nj + (nj - 1 - j if reverse else j), 0)

    in_specs = ([pl.BlockSpec((tb, 3 * RW_W), cur_map), pl.BlockSpec((tb, RW_LORA_W), cur_map)]
                + [_const_spec(c.shape) for c in consts])
    args = [rkv, zl] + list(consts)
    if final:
        y0 = extra[0]
        in_specs += [pl.BlockSpec((tb, RW_W), cur_map)] + [_const_spec(c.shape) for c in extra[1:]]
        args += list(extra)
    out_dtype = BF16 if final else F32
    act = lambda: pltpu.VMEM((tb, RW_W), BF16)
    return pl.pallas_call(
        functools.partial(_rwkv_kernel, reverse, final),
        grid=(b, nj),
        in_specs=in_specs,
        out_specs=pl.BlockSpec((tb, RW_W), cur_map),
        out_shape=jax.ShapeDtypeStruct((n, RW_W), out_dtype),
        scratch_shapes=[pltpu.VMEM((N_GROUPS, GROUP_W, GROUP_W), F32),
                        act(), act(), act(), act(), act(), act(), act(),
                        pltpu.VMEM((tb // CHUNK, RW_W), F32),
                        pltpu.VMEM((tb, RW_W), F32)],
        compiler_params=_params(2),
        name="rwkv_bwd" if reverse else "rwkv_fwd",
    )(*args)


def _merge_kernel(x_ref, oa_ref, ob_ref, gpre_ref, wg_ref, wa_ref, wb_ref, wo_ref, gpost_ref, h_ref):
    x = x_ref[...]
    u = _rms(x, gpre_ref[...]).astype(BF16)
    gates = _sigmoid(_dot(u, wg_ref[...]))
    merged = (gates[:, :D_MODEL] * _dot(oa_ref[...], wa_ref[...])
              + gates[:, D_MODEL:] * _dot(ob_ref[...], wb_ref[...]))
    m = _dot(merged.astype(BF16), wo_ref[...])
    h_ref[...] = x + _rms(m, gpost_ref[...])


def _merge(x, oa, ob, gpre, wg, wa, wb, wo, gpost):
    n = x.shape[0]
    tm = MERGE_TILE
    row = lambda w: pl.BlockSpec((tm, w), lambda i: (i, 0))
    consts = [gpre, wg, wa, wb, wo, gpost]
    return pl.pallas_call(
        _merge_kernel,
        grid=(n // tm,),
        in_specs=[row(D_MODEL), row(ATT_W), row(RW_W)] + [_const_spec(c.shape) for c in consts],
        out_specs=row(D_MODEL),
        out_shape=jax.ShapeDtypeStruct((n, D_MODEL), F32),
        compiler_params=_params(1),
        name="merge",
    )(x, oa, ob, *consts)


def _gelu_tanh(x):
    return 0.5 * x * (1.0 + jnp.tanh(math.sqrt(2.0 / math.pi) * (x + 0.044715 * (x * x * x))))


def _ffn_kernel(hc_ref, hp_ref, hn_ref, gpre_ref, wup_ref, cw_ref, cb_ref, wdn_ref, gpost_ref, o_ref):
    j = pl.program_id(1)
    nj = pl.num_programs(1)
    tm = FFN_TILE
    ext = tm + 2 * FFN_HALO
    hc = hc_ref[...]
    hx = jnp.concatenate([hp_ref[...], hc, hn_ref[...]], axis=0)
    rid = lax.broadcasted_iota(jnp.int32, (ext, 1), 0)
    keep = jnp.logical_and(jnp.logical_or(rid >= FFN_HALO, j > 0),
                           jnp.logical_or(rid < FFN_HALO + tm, j < nj - 1))
    u = jnp.where(keep, _rms(hx, gpre_ref[...]), 0.0).astype(BF16)

    def up(c0):
        return [_dot(u, wup_ref[:, off:off + FFN_COLS]) for off in (c0, D_FF + c0)]

    def conv(hh, off):
        cols = slice(off, off + FFN_COLS)
        prev = pltpu.roll(hh, 1, 0)[FFN_HALO:FFN_HALO + tm]
        nxt = pltpu.roll(hh, ext - 1, 0)[FFN_HALO:FFN_HALO + tm]
        cur = hh[FFN_HALO:FFN_HALO + tm]
        return (prev * cw_ref[0:1, cols] + cur * cw_ref[1:2, cols]
                + nxt * cw_ref[2:3, cols] + cb_ref[:, cols])

    acc = jnp.zeros((tm, D_MODEL), F32)
    starts = list(range(0, D_FF, FFN_COLS))
    pending = up(starts[0])
    for i, c0 in enumerate(starts):
        hh = pending
        if i + 1 < len(starts):
            pending = up(starts[i + 1])
        act = (_gelu_tanh(conv(hh[0], c0)) * conv(hh[1], D_FF + c0)).astype(BF16)
        acc = acc + _dot(act, wdn_ref[c0:c0 + FFN_COLS, :])
    o_ref[...] = hc + _rms(acc, gpost_ref[...])


def _ffn(h, gpre, wup, cw, cb, wdn, gpost, b, t):
    n = h.shape[0]
    tm = FFN_TILE
    nj = t // tm
    per = tm // FFN_HALO
    nhalo = n // FFN_HALO

    def prev_map(i, j):
        return (jnp.maximum((i * nj + j) * per - 1, 0), 0)

    def next_map(i, j):
        return (jnp.minimum((i * nj + j + 1) * per, nhalo - 1), 0)

    consts = [gpre, wup, cw, cb, wdn, gpost]
    return pl.pallas_call(
        _ffn_kernel,
        grid=(b, nj),
        in_specs=[pl.BlockSpec((tm, D_MODEL), lambda i, j: (i * nj + j, 0)),
                  pl.BlockSpec((FFN_HALO, D_MODEL), prev_map),
                  pl.BlockSpec((FFN_HALO, D_MODEL), next_map)]
                 + [_const_spec(c.shape) for c in consts],
        out_specs=pl.BlockSpec((tm, D_MODEL), lambda i, j: (i * nj + j, 0)),
        out_shape=jax.ShapeDtypeStruct((n, D_MODEL), F32),
        compiler_params=_params(2),
        name="ffn",
    )(h, h, h, *consts)


def _prepare(norm_mix_pre, norm_mix_post, norm_ffn_pre, norm_ffn_post, w_in, attn_sink,
             rw_mu_prev, rw_mu_next, rw_w0, rw_w2, rw_a0, rw_a2, rw_g2, rw_k_k, rw_k_a,
             rw_r_k, rw_ln_w, rw_ln_b, w_branch_attn, w_branch_rwkv, w_out,
             w_ffn_up, ffn_conv_w, ffn_conv_b, w_ffn_down):
    c_q = ATT_W
    c_kv = c_q + 2 * ATT_KV_W
    c_rkv = c_kv + 3 * RW_W
    c_zl = c_rkv + RW_LORA_W
    row = lambda p: p.reshape(1, -1).astype(F32)

    def lora_pad(w2, d, n_lora):
        z = jnp.zeros((2 * n_lora, RW_W), F32)
        return z.at[d * n_lora:(d + 1) * n_lora].set(w2[d]).astype(BF16)

    mu = jnp.stack([rw_mu_prev, rw_mu_next]).astype(F32)
    p = dict(
        g_mix_pre=row(norm_mix_pre), g_mix_post=row(norm_mix_post),
        g_ffn_pre=row(norm_ffn_pre), g_ffn_post=row(norm_ffn_post),
        wq=w_in[:, :c_q].astype(BF16), wkv=w_in[:, c_q:c_kv].astype(BF16),
        wrkv=w_in[:, c_kv:c_rkv].astype(BF16), wzl=w_in[:, c_rkv:c_zl].astype(BF16),
        wg=w_in[:, c_zl:].astype(BF16),
        sink=attn_sink.astype(F32), bias=_attn_bias(),
        mu_rkv=mu[:, :3 * RW_W], mu_zl=mu[:, 3 * RW_W:],
        w0=[row(rw_w0[d]) for d in range(2)],
        w2=[lora_pad(rw_w2, d, DECAY_LORA) for d in range(2)],
        a0=[row(rw_a0[d]) for d in range(2)],
        a2=[lora_pad(rw_a2, d, AAA_LORA) for d in range(2)],
        g2=rw_g2.astype(BF16), k_k=row(rw_k_k), k_a=row(rw_k_a), r_k=row(rw_r_k),
        ln_w=row(rw_ln_w), ln_b=row(rw_ln_b),
        wa=w_branch_attn.astype(BF16), wb=w_branch_rwkv.astype(BF16), wo=w_out.astype(BF16),
        wup=w_ffn_up.astype(BF16), cw=ffn_conv_w.astype(F32), cb=row(ffn_conv_b),
        wdn=w_ffn_down.astype(BF16),
    )
    return p


def _layer(x, p):
    b, t, _ = x.shape
    assert t % RW_TILE == 0 and t % ATT_TILE == 0 and t % FFN_TILE == 0
    assert (b * t) % PROJ_TILE == 0 and (b * t) % MERGE_TILE == 0
    x2 = x.reshape(b * t, D_MODEL)
    q, kv, rkv, zl = _in_proj(x2, p["g_mix_pre"], p["wq"], p["wkv"], p["wrkv"], p["wzl"],
                              p["mu_rkv"], p["mu_zl"], b, t)
    o_attn = _attention(q, kv, p["sink"], p["bias"], b, t)

    def consts(d):
        return [p["w0"][d], p["w2"][d], p["a0"][d], p["a2"][d],
                p["k_k"], p["k_a"]]

    y_fwd = _rwkv_pass(False, rkv, zl, consts(0), None, b, t)
    o_rwkv = _rwkv_pass(True, rkv, zl, consts(1),
                        [y_fwd, p["a0"][0], p["a2"][0], p["g2"], p["r_k"], p["ln_w"], p["ln_b"]], b, t)
    h = _merge(x2, o_attn, o_rwkv, p["g_mix_pre"], p["wg"], p["wa"], p["wb"], p["wo"], p["g_mix_post"])
    out = _ffn(h, p["g_ffn_pre"], p["wup"], p["cw"], p["cb"], p["wdn"], p["g_ffn_post"], b, t)
    return out.reshape(b, t, D_MODEL)


def kernel(x_prompt, x_sample, norm_mix_pre, norm_mix_post, norm_ffn_pre, norm_ffn_post, w_in, attn_sink, rw_mu_prev, rw_mu_next, rw_w0, rw_w2, rw_a0, rw_a2, rw_g2, rw_k_k, rw_k_a, rw_r_k, rw_ln_w, rw_ln_b, w_branch_attn, w_branch_rwkv, w_out, w_ffn_up, ffn_conv_w, ffn_conv_b, w_ffn_down):
    weights = (norm_mix_pre, norm_mix_post, norm_ffn_pre, norm_ffn_post, w_in, attn_sink,
               rw_mu_prev, rw_mu_next, rw_w0, rw_w2, rw_a0, rw_a2, rw_g2, rw_k_k, rw_k_a,
               rw_r_k, rw_ln_w, rw_ln_b, w_branch_attn, w_branch_rwkv, w_out,
               w_ffn_up, ffn_conv_w, ffn_conv_b, w_ffn_down)
    depth = w_in.shape[0]
    layers = [_prepare(*(w[l] for w in weights)) for l in range(depth)]

    def run(x):
        for p in layers:
            x = _layer(x, p)
        return x

    return (run(x_prompt), run(x_sample))
```

```python
import functools
import math

import numpy as np
import jax
import jax.numpy as jnp
from jax import lax
from jax.experimental import pallas as pl
from jax.experimental.pallas import tpu as pltpu

F32 = jnp.float32
BF16 = jnp.bfloat16

D_MODEL = 1024
ATT_H = 8
ATT_KV = 2
ATT_G = ATT_H // ATT_KV
ATT_HD = 64
ATT_W = ATT_H * ATT_HD
ATT_KV_W = ATT_KV * ATT_HD
WINDOW = 128
BLK = 128
ATT_SCALE = 1.0 / math.sqrt(ATT_HD)
RW_H = 8
RW_N = 64
RW_W = RW_H * RW_N
DECAY_LORA = 64
AAA_LORA = 64
GATE_LORA = 160
RW_LORA_W = 2 * DECAY_LORA + 2 * AAA_LORA + GATE_LORA
RW_MIX_W = 3 * RW_W + RW_LORA_W
GATE_W = 2 * D_MODEL
D_FF = 2816
NORM_EPS = 1e-6
GN_EPS = 64e-5

VMEM_LIMIT_BYTES = 56 * 2**20

CHUNK = 64
GROUP_H = 4
GROUP_W = GROUP_H * RW_N
N_GROUPS = RW_H // GROUP_H
RW_TILE = 512

PROJ_TILE = 512
PROJ_HALO = 8
ATT_TILE = 512
MERGE_TILE = 512
FFN_TILE = 512
FFN_HALO = 8
FFN_COLS = 256


def _dot(a, b):
    return jnp.dot(a, b, preferred_element_type=F32)


def _dot_nt(a, b):
    return lax.dot_general(a, b, (((1,), (1,)), ((), ())), preferred_element_type=F32)


def _dot_tn(a, b):
    return lax.dot_general(a, b, (((0,), (0,)), ((), ())), preferred_element_type=F32)


def _sigmoid(x):
    return 1.0 / (1.0 + jnp.exp(-x))


def _rms(x, g):
    return x * lax.rsqrt(jnp.mean(x * x, axis=-1, keepdims=True) + NORM_EPS) * g


def _const_spec(shape):
    nd = len(shape)
    return pl.BlockSpec(shape, lambda *_: (0,) * nd, pipeline_mode=pl.Buffered(1))


def _params(n_axes):
    return pltpu.CompilerParams(dimension_semantics=("arbitrary",) * n_axes,
                                vmem_limit_bytes=VMEM_LIMIT_BYTES)


def _in_proj_kernel(xc_ref, xp_ref, xn_ref, g_ref, wq_ref, wkv_ref, wrkv_ref, wzl_ref, mu_rkv_ref, mu_zl_ref,
                    q_ref, kv_ref, rkv_ref, zl_ref):
    j = pl.program_id(1)
    nj = pl.num_programs(1)
    tm = PROJ_TILE
    ext = tm + 2 * PROJ_HALO
    x = jnp.concatenate([xp_ref[...], xc_ref[...], xn_ref[...]], axis=0)
    rid = lax.broadcasted_iota(jnp.int32, (ext, 1), 0)
    keep = jnp.logical_and(jnp.logical_or(rid >= PROJ_HALO, j > 0),
                           jnp.logical_or(rid < PROJ_HALO + tm, j < nj - 1))
    u = jnp.where(keep, _rms(x, g_ref[...]), 0.0).astype(BF16)
    um = u[PROJ_HALO:PROJ_HALO + tm]
    q_ref[...] = (_dot(um, wq_ref[...]) * ATT_SCALE).astype(BF16)
    kv_ref[...] = _dot(um, wkv_ref[...]).astype(BF16)

    def shifted(w_ref, mu_ref, o_ref, cols):
        z = _dot(u, w_ref[:, cols])
        c = z[PROJ_HALO:PROJ_HALO + tm]
        up = pltpu.roll(z, 1, 0)[PROJ_HALO:PROJ_HALO + tm]
        dn = pltpu.roll(z, ext - 1, 0)[PROJ_HALO:PROJ_HALO + tm]
        o_ref[:, cols] = (c + mu_ref[0:1, cols] * (up - c) + mu_ref[1:2, cols] * (dn - c)).astype(BF16)

    for c0 in range(0, 3 * RW_W, RW_W):
        shifted(wrkv_ref, mu_rkv_ref, rkv_ref, slice(c0, c0 + RW_W))
    shifted(wzl_ref, mu_zl_ref, zl_ref, slice(0, RW_LORA_W))


def _in_proj(x, g, wq, wkv, wrkv, wzl, mu_rkv, mu_zl, b, t):
    n = x.shape[0]
    tm = PROJ_TILE
    nj = t // tm
    per = tm // PROJ_HALO
    nhalo = n // PROJ_HALO

    def prev_map(i, j):
        return (jnp.maximum((i * nj + j) * per - 1, 0), 0)

    def next_map(i, j):
        return (jnp.minimum((i * nj + j + 1) * per, nhalo - 1), 0)

    row = lambda w: pl.BlockSpec((tm, w), lambda i, j: (i * nj + j, 0))
    consts = [g, wq, wkv, wrkv, wzl, mu_rkv, mu_zl]
    return pl.pallas_call(
        _in_proj_kernel,
        grid=(b, nj),
        in_specs=[row(D_MODEL), pl.BlockSpec((PROJ_HALO, D_MODEL), prev_map),
                  pl.BlockSpec((PROJ_HALO, D_MODEL), next_map)] + [_const_spec(c.shape) for c in consts],
        out_specs=[row(ATT_W), row(2 * ATT_KV_W), row(3 * RW_W), row(RW_LORA_W)],
        out_shape=[jax.ShapeDtypeStruct((n, ATT_W), BF16),
                   jax.ShapeDtypeStruct((n, 2 * ATT_KV_W), BF16),
                   jax.ShapeDtypeStruct((n, 3 * RW_W), BF16),
                   jax.ShapeDtypeStruct((n, RW_LORA_W), BF16)],
        compiler_params=_params(2),
        name="in_proj",
    )(x, x, x, *consts)


def _attn_bias():
    slopes = np.exp2(-8.0 / ATT_H * np.arange(1, ATT_H + 1, dtype=np.float64))
    dist = np.abs(np.arange(BLK)[:, None] - (np.arange(3 * BLK)[None, :] - BLK))
    bias = -(slopes[:, None, None] * dist[None])
    bias = np.where(dist[None] <= WINDOW, bias, -np.inf)
    bias = bias.reshape(ATT_KV, ATT_G * BLK, 3 * BLK).transpose(0, 2, 1)
    return jnp.asarray(bias, F32)


def _attn_kernel(sink_ref, q_ref, kvp_ref, kvc_ref, kvn_ref, bias_ref, o_ref):
    j = pl.program_id(1)
    nj = pl.num_programs(1)
    has_next = jnp.where(j < nj - 1, 1.0, 0.0).astype(F32)
    kvwin = jnp.concatenate([kvp_ref[...].astype(F32), kvc_ref[...].astype(F32),
                             kvn_ref[...].astype(F32) * has_next], axis=0)
    kwin = kvwin[:, :ATT_KV_W].astype(BF16)
    v_t = kvwin[:, ATT_KV_W:].T.astype(BF16)
    krow = lax.broadcasted_iota(jnp.int32, (3 * BLK, 1), 0)
    n_qb = ATT_TILE // BLK
    gw = ATT_G * ATT_HD
    no_q = jnp.zeros((ATT_HD, ATT_G * BLK), F32)

    pairs = [(qb, kv) for qb in range(n_qb) for kv in range(ATT_KV)]
    every = range(len(pairs))

    def scores(x):
        qb, kv = pairs[x]
        q_t = q_ref[qb * BLK:(qb + 1) * BLK, kv * gw:(kv + 1) * gw].astype(F32).T
        q_t = jnp.concatenate([q_t[g * ATT_HD:(g + 1) * ATT_HD] for g in range(ATT_G)], axis=1)
        q_t = jnp.concatenate([q_t, no_q] if kv == 0 else [no_q, q_t], axis=0).astype(BF16)
        return _dot(kwin[qb * BLK:qb * BLK + 3 * BLK], q_t)

    def softmax(x, s):
        qb, kv = pairs[x]
        key_ok = krow >= jnp.maximum(2 - (j * n_qb + qb), 0) * BLK
        s = jnp.where(key_ok, s + bias_ref[kv], -jnp.inf)
        sink = jnp.concatenate(
            [jnp.full((1, BLK), sink_ref[kv * ATT_G + g], F32) for g in range(ATT_G)], axis=1)
        mx = jnp.maximum(jnp.max(s, axis=0, keepdims=True), sink)
        p = jnp.exp(s - mx)
        den = jnp.sum(p, axis=0, keepdims=True) + jnp.exp(sink - mx)
        return p.astype(BF16), 1.0 / den

    def values(x, p, inv_den):
        qb, kv = pairs[x]
        o_t = _dot(v_t[kv * ATT_HD:(kv + 1) * ATT_HD, qb * BLK:qb * BLK + 3 * BLK], p) * inv_den
        o_t = jnp.concatenate([o_t[:, g * BLK:(g + 1) * BLK] for g in range(ATT_G)], axis=0)
        o_ref[qb * BLK:(qb + 1) * BLK, kv * gw:(kv + 1) * gw] = o_t.T.astype(BF16)

    s_all = [scores(x) for x in every]
    p_all = [softmax(x, s_all[x]) for x in every]
    for x in every:
        values(x, *p_all[x])


def _attention(q, kv, sink, bias, b, t):
    n = q.shape[0]
    tq = ATT_TILE
    nj = t // tq
    per = tq // BLK
    nblk = n // BLK

    def prev_map(i, j):
        return (jnp.maximum((i * nj + j) * per - 1, 0), 0)

    def next_map(i, j):
        return (jnp.minimum((i * nj + j + 1) * per, nblk - 1), 0)

    return pl.pallas_call(
        _attn_kernel,
        grid=(b, nj),
        in_specs=[pl.BlockSpec(memory_space=pltpu.SMEM),
                  pl.BlockSpec((tq, ATT_W), lambda i, j: (i * nj + j, 0)),
                  pl.BlockSpec((BLK, 2 * ATT_KV_W), prev_map),
                  pl.BlockSpec((tq, 2 * ATT_KV_W), lambda i, j: (i * nj + j, 0)),
                  pl.BlockSpec((BLK, 2 * ATT_KV_W), next_map),
                  _const_spec(bias.shape)],
        out_specs=pl.BlockSpec((tq, ATT_W), lambda i, j: (i * nj + j, 0)),
        out_shape=jax.ShapeDtypeStruct((n, ATT_W), BF16),
        compiler_params=_params(2),
        name="attention",
    )(sink, q, kv, kv, kv, bias)


def _block_diag(x, mask):
    tiled = jnp.concatenate([x.astype(F32)] * GROUP_H, axis=0)
    return jnp.where(mask, tiled, 0.0).astype(BF16)


def _head_sum(x):
    lanes = 2 * RW_N
    low = lax.broadcasted_iota(jnp.int32, (x.shape[0], lanes), 1) < RW_N
    out = []
    for c0 in range(0, RW_W, lanes):
        xc = x[:, c0:c0 + lanes]
        lo = jnp.sum(jnp.where(low, xc, 0.0), axis=1, keepdims=True)
        hi = jnp.sum(jnp.where(low, 0.0, xc), axis=1, keepdims=True)
        out.append(jnp.where(low, lo, hi))
    return jnp.concatenate(out, axis=1)


def _rwkv_kernel(reverse, final, *refs):
    if final:
        (rkv_ref, zl_ref, w0, w2, a0, a2, k_k, k_a,
         y0, a0o, a2o, g2, r_k, ln_w, ln_b,
         out_ref, s_ref, at_s, rt_s, bt_s, kt_s, v_s, bp_s, kp_s, pc_s, y_s) = refs
    else:
        (rkv_ref, zl_ref, w0, w2, a0, a2, k_k, k_a,
         out_ref, s_ref, at_s, rt_s, bt_s, kt_s, v_s, bp_s, kp_s, pc_s, y_s) = refs

    j = pl.program_id(1)

    @pl.when(j == 0)
    def _():
        s_ref[...] = jnp.zeros_like(s_ref)

    r = rkv_ref[:, :RW_W].astype(F32)
    k = rkv_ref[:, RW_W:2 * RW_W].astype(F32)
    v_bf = rkv_ref[:, 2 * RW_W:]
    v = v_bf.astype(F32)
    wd = zl_ref[:, :2 * DECAY_LORA].astype(F32)
    ad = zl_ref[:, 2 * DECAY_LORA:2 * DECAY_LORA + 2 * AAA_LORA]

    w = w0[...] + _dot(jnp.tanh(wd).astype(BF16), w2[...])
    lw = -math.exp(-0.5) * _sigmoid(w)
    a = _sigmoid(a0[...] + _dot(ad, a2[...]))
    kkk = k * k_k[...]
    ss = _head_sum(kkk * kkk)
    kk = kkk / jnp.maximum(jnp.sqrt(ss), 1e-12)
    kdir = k * (1.0 + (a - 1.0) * k_a[...])
    bb = kk * a

    rid = lax.broadcasted_iota(jnp.int32, (CHUNK, CHUNK), 0)
    cid = lax.broadcasted_iota(jnp.int32, (CHUNK, CHUNK), 1)
    lmat = jnp.where((cid >= rid) if reverse else (cid <= rid), 1.0, 0.0).astype(BF16)
    lw_hi = lw.astype(BF16)
    lw_lo = (lw - lw_hi.astype(F32)).astype(BF16)
    cum = jnp.concatenate(
        [_dot(lmat, lw_hi[c0:c0 + CHUNK]) + _dot(lmat, lw_lo[c0:c0 + CHUNK])
         for c0 in range(0, RW_TILE, CHUNK)], axis=0)

    at_s[...] = (-kk * jnp.exp(cum - lw)).astype(BF16)
    rt_s[...] = (r * jnp.exp(cum)).astype(BF16)
    e_neg = jnp.exp(-cum)
    bt_s[...] = (bb * e_neg).astype(BF16)
    kt_s[...] = (kdir * e_neg).astype(BF16)
    v_s[...] = v_bf
    for ci in range(RW_TILE // CHUNK):
        rows = slice(ci * CHUNK, (ci + 1) * CHUNK)
        end = ci * CHUNK if reverse else (ci + 1) * CHUNK - 1
        cum_end = cum[end:end + 1, :]
        e_end = jnp.exp(cum_end - cum[rows])
        bp_s[rows, :] = (bb[rows] * e_end).astype(BF16)
        kp_s[rows, :] = (kdir[rows] * e_end).astype(BF16)
        pc_s[ci:ci + 1, :] = jnp.exp(cum_end)

    trow = lax.broadcasted_iota(jnp.int32, (CHUNK, GROUP_W), 0)
    scol = lax.broadcasted_iota(jnp.int32, (CHUNK, GROUP_W), 1) % CHUNK
    strict = (scol > trow) if reverse else (scol < trow)
    incl = (scol >= trow) if reverse else (scol <= trow)
    eye = jnp.where(scol == trow, 1.0, 0.0).astype(F32)
    bdr = lax.broadcasted_iota(jnp.int32, (GROUP_H * CHUNK, GROUP_W), 0) // CHUNK
    bdc = lax.broadcasted_iota(jnp.int32, (GROUP_H * CHUNK, GROUP_W), 1) // RW_N
    bdmask = bdr == bdc
    bd = functools.partial(_block_diag, mask=bdmask)

    n_ch = RW_TILE // CHUNK
    scan = [n_ch - 1 - s if reverse else s for s in range(n_ch)]
    pairs = [(ci, g) for ci in scan for g in range(N_GROUPS)]
    every = range(len(pairs))

    def ld(ref, x):
        ci, g = pairs[x]
        return ref[ci * CHUNK:(ci + 1) * CHUNK, g * GROUP_W:(g + 1) * GROUP_W]

    a_t = [ld(at_s, x) for x in every]
    r_t = [ld(rt_s, x) for x in every]
    v_c = [ld(v_s, x) for x in every]
    ar = [jnp.concatenate([a_t[x], r_t[x]], axis=0) for x in every]
    g1 = [_dot_nt(ar[x], bd(ld(bt_s, x))) for x in every]
    g2_ = [_dot_nt(ar[x], bd(ld(kt_s, x))) for x in every]
    ab = [jnp.where(strict, g1[x][:CHUNK], 0.0) for x in every]
    rb = [jnp.where(incl, g1[x][CHUNK:], 0.0) for x in every]
    ak = [jnp.where(strict, g2_[x][:CHUNK], 0.0) for x in every]
    rk = [jnp.where(incl, g2_[x][CHUNK:], 0.0) for x in every]

    t_inv = [eye + ab[x] for x in every]
    pw = [_dot(ab[x].astype(BF16), bd(ab[x])) for x in every]
    n_lvl = int(math.log2(CHUNK))
    for lvl in range(1, n_lvl):
        if lvl < n_lvl - 1:
            res = [_dot(jnp.concatenate([pw[x], t_inv[x]], axis=0).astype(BF16), bd(pw[x]))
                   for x in every]
            pw = [res[x][:CHUNK] for x in every]
            t_inv = [t_inv[x] + res[x][CHUNK:] for x in every]
        else:
            t_inv = [t_inv[x] + _dot(t_inv[x].astype(BF16), bd(pw[x])) for x in every]

    wv = [_dot(ak[x].astype(BF16), bd(v_c[x])) for x in every]
    atw = [_dot(t_inv[x].astype(BF16), jnp.concatenate([bd(a_t[x]), bd(wv[x])], axis=1))
           for x in every]
    a_p = [atw[x][:, :GROUP_W] for x in every]
    w_p = [atw[x][:, GROUP_W:] for x in every]
    b_p = [ld(bp_s, x) for x in every]
    m_bd = [jnp.where(bdmask, _dot_tn(a_p[x].astype(BF16), b_p[x]), 0.0).astype(BF16)
            for x in every]
    n_bd = [jnp.where(bdmask, _dot_tn(jnp.concatenate([w_p[x].astype(BF16), v_c[x]], axis=0),
                                      jnp.concatenate([b_p[x], ld(kp_s, x)], axis=0)), 0.0)
            for x in every]
    r_p = [r_t[x].astype(F32) + _dot(rb[x].astype(BF16), bd(a_p[x])) for x in every]

    state = [s_ref[g] for g in range(N_GROUPS)]
    for c in range(n_ch):
        xs = range(c * N_GROUPS, (c + 1) * N_GROUPS)
        y_c = {}
        for x in xs:
            ci, g = pairs[x]
            s_bf = state[g].astype(BF16)
            y_c[x] = _dot_nt(r_p[x].astype(BF16), s_bf)
            state[g] = (state[g] * pc_s[ci:ci + 1, g * GROUP_W:(g + 1) * GROUP_W]
                        + _dot(s_bf, m_bd[x]) + n_bd[x])
        for x in xs:
            ci, g = pairs[x]
            y_i = _dot(jnp.concatenate([rb[x], rk[x]], axis=1).astype(BF16),
                       jnp.concatenate([bd(w_p[x]), bd(v_c[x])], axis=0))
            y_s[ci * CHUNK:(ci + 1) * CHUNK, g * GROUP_W:(g + 1) * GROUP_W] = y_c[x] + y_i
    for g in range(N_GROUPS):
        s_ref[g] = state[g]

    y = y_s[...]
    if not final:
        out_ref[...] = y
    else:
        ytot = y + y0[...]
        inv_n = 1.0 / RW_N
        mu = _head_sum(ytot) * inv_n
        d = ytot - mu
        var = _head_sum(d * d) * inv_n
        yn = d * lax.rsqrt(var + GN_EPS) * ln_w[...] + ln_b[...]
        a_other = _sigmoid(a0o[...] + _dot(ad, a2o[...]))
        ksum = k * (2.0 + (a + a_other - 2.0) * k_a[...])
        bonus = _head_sum(r * ksum * r_k[...]) * v
        gd = zl_ref[:, 2 * DECAY_LORA + 2 * AAA_LORA:].astype(F32)
        gate = _dot(_sigmoid(gd).astype(BF16), g2[...])
        out_ref[...] = ((yn + bonus) * gate).astype(BF16)


def _rwkv_pass(reverse, rkv, zl, consts, extra, b, t):
    n = rkv.shape[0]
    tb = RW_TILE
    nj = t // tb
    final = reverse

    def cur_map(i, j):
        return (i * nj + (nj - 1 - j if reverse else j), 0)

    in_specs = ([pl.BlockSpec((tb, 3 * RW_W), cur_map), pl.BlockSpec((tb, RW_LORA_W), cur_map)]
                + [_const_spec(c.shape) for c in consts])
    args = [rkv, zl] + list(consts)
    if final:
        y0 = extra[0]
        in_specs += [pl.BlockSpec((tb, RW_W), cur_map)] + [_const_spec(c.shape) for c in extra[1:]]
        args += list(extra)
    out_dtype = BF16 if final else F32
    act = lambda: pltpu.VMEM((tb, RW_W), BF16)
    return pl.pallas_call(
        functools.partial(_rwkv_kernel, reverse, final),
        grid=(b, nj),
        in_specs=in_specs,
        out_specs=pl.BlockSpec((tb, RW_W), cur_map),
        out_shape=jax.ShapeDtypeStruct((n, RW_W), out_dtype),
        scratch_shapes=[pltpu.VMEM((N_GROUPS, GROUP_W, GROUP_W), F32),
                        act(), act(), act(), act(), act(), act(), act(),
                        pltpu.VMEM((tb // CHUNK, RW_W), F32),
                        pltpu.VMEM((tb, RW_W), F32)],
        compiler_params=_params(2),
        name="rwkv_bwd" if reverse else "rwkv_fwd",
    )(*args)


def _merge_kernel(x_ref, oa_ref, ob_ref, gpre_ref, wg_ref, wa_ref, wb_ref, wo_ref, gpost_ref, h_ref):
    x = x_ref[...]
    u = _rms(x, gpre_ref[...]).astype(BF16)
    gates = _sigmoid(_dot(u, wg_ref[...]))
    merged = (gates[:, :D_MODEL] * _dot(oa_ref[...], wa_ref[...])
              + gates[:, D_MODEL:] * _dot(ob_ref[...], wb_ref[...]))
    m = _dot(merged.astype(BF16), wo_ref[...])
    h_ref[...] = x + _rms(m, gpost_ref[...])


def _merge(x, oa, ob, gpre, wg, wa, wb, wo, gpost):
    n = x.shape[0]
    tm = MERGE_TILE
    row = lambda w: pl.BlockSpec((tm, w), lambda i: (i, 0))
    consts = [gpre, wg, wa, wb, wo, gpost]
    return pl.pallas_call(
        _merge_kernel,
        grid=(n // tm,),
        in_specs=[row(D_MODEL), row(ATT_W), row(RW_W)] + [_const_spec(c.shape) for c in consts],
        out_specs=row(D_MODEL),
        out_shape=jax.ShapeDtypeStruct((n, D_MODEL), F32),
        compiler_params=_params(1),
        name="merge",
    )(x, oa, ob, *consts)


def _gelu_tanh(x):
    return 0.5 * x * (1.0 + jnp.tanh(math.sqrt(2.0 / math.pi) * (x + 0.044715 * (x * x * x))))


def _ffn_kernel(hc_ref, hp_ref, hn_ref, gpre_ref, wup_ref, cw_ref, cb_ref, wdn_ref, gpost_ref, o_ref):
    j = pl.program_id(1)
    nj = pl.num_programs(1)
    tm = FFN_TILE
    ext = tm + 2 * FFN_HALO
    hc = hc_ref[...]
    hx = jnp.concatenate([hp_ref[...], hc, hn_ref[...]], axis=0)
    rid = lax.broadcasted_iota(jnp.int32, (ext, 1), 0)
    keep = jnp.logical_and(jnp.logical_or(rid >= FFN_HALO, j > 0),
                           jnp.logical_or(rid < FFN_HALO + tm, j < nj - 1))
    u = jnp.where(keep, _rms(hx, gpre_ref[...]), 0.0).astype(BF16)

    def up(c0):
        return [_dot(u, wup_ref[:, off:off + FFN_COLS]) for off in (c0, D_FF + c0)]

    def conv(hh, off):
        cols = slice(off, off + FFN_COLS)
        prev = pltpu.roll(hh, 1, 0)[FFN_HALO:FFN_HALO + tm]
        nxt = pltpu.roll(hh, ext - 1, 0)[FFN_HALO:FFN_HALO + tm]
        cur = hh[FFN_HALO:FFN_HALO + tm]
        return (prev * cw_ref[0:1, cols] + cur * cw_ref[1:2, cols]
                + nxt * cw_ref[2:3, cols] + cb_ref[:, cols])

    starts = list(range(0, D_FF, FFN_COLS))
    pending = up(starts[0])
    acts = []
    for i, c0 in enumerate(starts):
        hh = pending
        if i + 1 < len(starts):
            pending = up(starts[i + 1])
        acts.append((_gelu_tanh(conv(hh[0], c0)) * conv(hh[1], D_FF + c0)).astype(BF16))
    f = _dot(jnp.concatenate(acts, axis=1), wdn_ref[...])
    o_ref[...] = hc + _rms(f, gpost_ref[...])


def _ffn(h, gpre, wup, cw, cb, wdn, gpost, b, t):
    n = h.shape[0]
    tm = FFN_TILE
    nj = t // tm
    per = tm // FFN_HALO
    nhalo = n // FFN_HALO

    def prev_map(i, j):
        return (jnp.maximum((i * nj + j) * per - 1, 0), 0)

    def next_map(i, j):
        return (jnp.minimum((i * nj + j + 1) * per, nhalo - 1), 0)

    consts = [gpre, wup, cw, cb, wdn, gpost]
    return pl.pallas_call(
        _ffn_kernel,
        grid=(b, nj),
        in_specs=[pl.BlockSpec((tm, D_MODEL), lambda i, j: (i * nj + j, 0)),
                  pl.BlockSpec((FFN_HALO, D_MODEL), prev_map),
                  pl.BlockSpec((FFN_HALO, D_MODEL), next_map)]
                 + [_const_spec(c.shape) for c in consts],
        out_specs=pl.BlockSpec((tm, D_MODEL), lambda i, j: (i * nj + j, 0)),
        out_shape=jax.ShapeDtypeStruct((n, D_MODEL), F32),
        compiler_params=_params(2),
        name="ffn",
    )(h, h, h, *consts)


def _prepare(norm_mix_pre, norm_mix_post, norm_ffn_pre, norm_ffn_post, w_in, attn_sink,
             rw_mu_prev, rw_mu_next, rw_w0, rw_w2, rw_a0, rw_a2, rw_g2, rw_k_k, rw_k_a,
             rw_r_k, rw_ln_w, rw_ln_b, w_branch_attn, w_branch_rwkv, w_out,
             w_ffn_up, ffn_conv_w, ffn_conv_b, w_ffn_down):
    c_q = ATT_W
    c_kv = c_q + 2 * ATT_KV_W
    c_rkv = c_kv + 3 * RW_W
    c_zl = c_rkv + RW_LORA_W
    row = lambda p: p.reshape(1, -1).astype(F32)

    def lora_pad(w2, d, n_lora):
        z = jnp.zeros((2 * n_lora, RW_W), F32)
        return z.at[d * n_lora:(d + 1) * n_lora].set(w2[d]).astype(BF16)

    mu = jnp.stack([rw_mu_prev, rw_mu_next]).astype(F32)
    p = dict(
        g_mix_pre=row(norm_mix_pre), g_mix_post=row(norm_mix_post),
        g_ffn_pre=row(norm_ffn_pre), g_ffn_post=row(norm_ffn_post),
        wq=w_in[:, :c_q].astype(BF16), wkv=w_in[:, c_q:c_kv].astype(BF16),
        wrkv=w_in[:, c_kv:c_rkv].astype(BF16), wzl=w_in[:, c_rkv:c_zl].astype(BF16),
        wg=w_in[:, c_zl:].astype(BF16),
        sink=attn_sink.astype(F32), bias=_attn_bias(),
        mu_rkv=mu[:, :3 * RW_W], mu_zl=mu[:, 3 * RW_W:],
        w0=[row(rw_w0[d]) for d in range(2)],
        w2=[lora_pad(rw_w2, d, DECAY_LORA) for d in range(2)],
        a0=[row(rw_a0[d]) for d in range(2)],
        a2=[lora_pad(rw_a2, d, AAA_LORA) for d in range(2)],
        g2=rw_g2.astype(BF16), k_k=row(rw_k_k), k_a=row(rw_k_a), r_k=row(rw_r_k),
        ln_w=row(rw_ln_w), ln_b=row(rw_ln_b),
        wa=w_branch_attn.astype(BF16), wb=w_branch_rwkv.astype(BF16), wo=w_out.astype(BF16),
        wup=w_ffn_up.astype(BF16), cw=ffn_conv_w.astype(F32), cb=row(ffn_conv_b),
        wdn=w_ffn_down.astype(BF16),
    )
    return p


def _layer(x, p):
    b, t, _ = x.shape
    assert t % RW_TILE == 0 and t % ATT_TILE == 0 and t % FFN_TILE == 0
    assert (b * t) % PROJ_TILE == 0 and (b * t) % MERGE_TILE == 0
    x2 = x.reshape(b * t, D_MODEL)
    q, kv, rkv, zl = _in_proj(x2, p["g_mix_pre"], p["wq"], p["wkv"], p["wrkv"], p["wzl"],
                              p["mu_rkv"], p["mu_zl"], b, t)
    o_attn = _attention(q, kv, p["sink"], p["bias"], b, t)

    def consts(d):
        return [p["w0"][d], p["w2"][d], p["a0"][d], p["a2"][d],
                p["k_k"], p["k_a"]]

    y_fwd = _rwkv_pass(False, rkv, zl, consts(0), None, b, t)
    o_rwkv = _rwkv_pass(True, rkv, zl, consts(1),
                        [y_fwd, p["a0"][0], p["a2"][0], p["g2"], p["r_k"], p["ln_w"], p["ln_b"]], b, t)
    h = _merge(x2, o_attn, o_rwkv, p["g_mix_pre"], p["wg"], p["wa"], p["wb"], p["wo"], p["g_mix_post"])
    out = _ffn(h, p["g_ffn_pre"], p["wup"], p["cw"], p["cb"], p["wdn"], p["g_ffn_post"], b, t)
    return out.reshape(b, t, D_MODEL)


def kernel(x_prompt, x_sample, norm_mix_pre, norm_mix_post, norm_ffn_pre, norm_ffn_post, w_in, attn_sink, rw_mu_prev, rw_mu_next, rw_w0, rw_w2, rw_a0, rw_a2, rw_g2, rw_k_k, rw_k_a, rw_r_k, rw_ln_w, rw_ln_b, w_branch_attn, w_branch_rwkv, w_out, w_ffn_up, ffn_conv_w, ffn_conv_b, w_ffn_down):
    weights = (norm_mix_pre, norm_mix_post, norm_ffn_pre, norm_ffn_post, w_in, attn_sink,
               rw_mu_prev, rw_mu_next, rw_w0, rw_w2, rw_a0, rw_a2, rw_g2, rw_k_k, rw_k_a,
               rw_r_k, rw_ln_w, rw_ln_b, w_branch_attn, w_branch_rwkv, w_out,
               w_ffn_up, ffn_conv_w, ffn_conv_b, w_ffn_down)
    depth = w_in.shape[0]
    layers = [_prepare(*(w[l] for w in weights)) for l in range(depth)]

    def run(x):
        for p in layers:
            x = _layer(x, p)
        return x

    return (run(x_prompt), run(x_sample))
```

```python
import functools
import math

import numpy as np
import jax
import jax.numpy as jnp
from jax import lax
from jax.experimental import pallas as pl
from jax.experimental.pallas import tpu as pltpu

F32 = jnp.float32
BF16 = jnp.bfloat16

D_MODEL = 1024
ATT_H = 8
ATT_KV = 2
ATT_G = ATT_H // ATT_KV
ATT_HD = 64
ATT_W = ATT_H * ATT_HD
ATT_KV_W = ATT_KV * ATT_HD
WINDOW = 128
BLK = 128
ATT_SCALE = 1.0 / math.sqrt(ATT_HD)
RW_H = 8
RW_N = 64
RW_W = RW_H * RW_N
DECAY_LORA = 64
AAA_LORA = 64
GATE_LORA = 160
RW_LORA_W = 2 * DECAY_LORA + 2 * AAA_LORA + GATE_LORA
RW_MIX_W = 3 * RW_W + RW_LORA_W
GATE_W = 2 * D_MODEL
D_FF = 2816
NORM_EPS = 1e-6
GN_EPS = 64e-5

VMEM_LIMIT_BYTES = 56 * 2**20

CHUNK = 64
GROUP_H = 4
GROUP_W = GROUP_H * RW_N
N_GROUPS = RW_H // GROUP_H
RW_TILE = 512

PROJ_TILE = 512
PROJ_HALO = 8
ATT_TILE = 512
MERGE_TILE = 512
FFN_TILE = 512
FFN_HALO = 8
FFN_COLS = 256


def _dot(a, b):
    return jnp.dot(a, b, preferred_element_type=F32)


def _dot_nt(a, b):
    return lax.dot_general(a, b, (((1,), (1,)), ((), ())), preferred_element_type=F32)


def _dot_tn(a, b):
    return lax.dot_general(a, b, (((0,), (0,)), ((), ())), preferred_element_type=F32)


def _sigmoid(x):
    return 1.0 / (1.0 + jnp.exp(-x))


def _rms(x, g):
    return x * lax.rsqrt(jnp.mean(x * x, axis=-1, keepdims=True) + NORM_EPS) * g


def _const_spec(shape):
    nd = len(shape)
    return pl.BlockSpec(shape, lambda *_: (0,) * nd, pipeline_mode=pl.Buffered(1))


def _params(n_axes):
    return pltpu.CompilerParams(dimension_semantics=("arbitrary",) * n_axes,
                                vmem_limit_bytes=VMEM_LIMIT_BYTES)


def _in_proj_kernel(xc_ref, xp_ref, xn_ref, g_ref, wq_ref, wkv_ref, wrkv_ref, wzl_ref, mu_rkv_ref, mu_zl_ref,
                    q_ref, kv_ref, rkv_ref, zl_ref):
    j = pl.program_id(1)
    nj = pl.num_programs(1)
    tm = PROJ_TILE
    ext = tm + 2 * PROJ_HALO
    x = jnp.concatenate([xp_ref[...], xc_ref[...], xn_ref[...]], axis=0)
    rid = lax.broadcasted_iota(jnp.int32, (ext, 1), 0)
    keep = jnp.logical_and(jnp.logical_or(rid >= PROJ_HALO, j > 0),
                           jnp.logical_or(rid < PROJ_HALO + tm, j < nj - 1))
    u = jnp.where(keep, _rms(x, g_ref[...]), 0.0).astype(BF16)
    um = u[PROJ_HALO:PROJ_HALO + tm]
    q_ref[...] = (_dot(um, wq_ref[...]) * ATT_SCALE).astype(BF16)
    kv_ref[...] = _dot(um, wkv_ref[...]).astype(BF16)

    def shifted(w_ref, mu_ref, o_ref, cols):
        z = _dot(u, w_ref[:, cols])
        c = z[PROJ_HALO:PROJ_HALO + tm]
        up = pltpu.roll(z, 1, 0)[PROJ_HALO:PROJ_HALO + tm]
        dn = pltpu.roll(z, ext - 1, 0)[PROJ_HALO:PROJ_HALO + tm]
        o_ref[:, cols] = (c + mu_ref[0:1, cols] * (up - c) + mu_ref[1:2, cols] * (dn - c)).astype(BF16)

    for c0 in range(0, 3 * RW_W, RW_W):
        shifted(wrkv_ref, mu_rkv_ref, rkv_ref, slice(c0, c0 + RW_W))
    shifted(wzl_ref, mu_zl_ref, zl_ref, slice(0, RW_LORA_W))


def _in_proj(x, g, wq, wkv, wrkv, wzl, mu_rkv, mu_zl, b, t):
    n = x.shape[0]
    tm = PROJ_TILE
    nj = t // tm
    per = tm // PROJ_HALO
    nhalo = n // PROJ_HALO

    def prev_map(i, j):
        return (jnp.maximum((i * nj + j) * per - 1, 0), 0)

    def next_map(i, j):
        return (jnp.minimum((i * nj + j + 1) * per, nhalo - 1), 0)

    row = lambda w: pl.BlockSpec((tm, w), lambda i, j: (i * nj + j, 0))
    consts = [g, wq, wkv, wrkv, wzl, mu_rkv, mu_zl]
    return pl.pallas_call(
        _in_proj_kernel,
        grid=(b, nj),
        in_specs=[row(D_MODEL), pl.BlockSpec((PROJ_HALO, D_MODEL), prev_map),
                  pl.BlockSpec((PROJ_HALO, D_MODEL), next_map)] + [_const_spec(c.shape) for c in consts],
        out_specs=[row(ATT_W), row(2 * ATT_KV_W), row(3 * RW_W), row(RW_LORA_W)],
        out_shape=[jax.ShapeDtypeStruct((n, ATT_W), BF16),
                   jax.ShapeDtypeStruct((n, 2 * ATT_KV_W), BF16),
                   jax.ShapeDtypeStruct((n, 3 * RW_W), BF16),
                   jax.ShapeDtypeStruct((n, RW_LORA_W), BF16)],
        compiler_params=_params(2),
        name="in_proj",
    )(x, x, x, *consts)


def _attn_bias():
    slopes = np.exp2(-8.0 / ATT_H * np.arange(1, ATT_H + 1, dtype=np.float64))
    dist = np.abs(np.arange(BLK)[:, None] - (np.arange(3 * BLK)[None, :] - BLK))
    bias = -(slopes[:, None, None] * dist[None])
    bias = np.where(dist[None] <= WINDOW, bias, -np.inf)
    bias = bias.reshape(ATT_KV, ATT_G * BLK, 3 * BLK).transpose(0, 2, 1)
    return jnp.asarray(bias, F32)


def _attn_kernel(sink_ref, q_ref, kvp_ref, kvc_ref, kvn_ref, bias_ref, o_ref):
    j = pl.program_id(1)
    nj = pl.num_programs(1)
    has_next = jnp.where(j < nj - 1, 1.0, 0.0).astype(F32)
    kvwin = jnp.concatenate([kvp_ref[...].astype(F32), kvc_ref[...].astype(F32),
                             kvn_ref[...].astype(F32) * has_next], axis=0)
    kwin = kvwin[:, :ATT_KV_W].astype(BF16)
    v_t = kvwin[:, ATT_KV_W:].T.astype(BF16)
    krow = lax.broadcasted_iota(jnp.int32, (3 * BLK, 1), 0)
    n_qb = ATT_TILE // BLK
    gw = ATT_G * ATT_HD
    no_q = jnp.zeros((ATT_HD, ATT_G * BLK), F32)

    pairs = [(qb, kv) for qb in range(n_qb) for kv in range(ATT_KV)]
    every = range(len(pairs))

    def scores(x):
        qb, kv = pairs[x]
        q_t = q_ref[qb * BLK:(qb + 1) * BLK, kv * gw:(kv + 1) * gw].astype(F32).T
        q_t = jnp.concatenate([q_t[g * ATT_HD:(g + 1) * ATT_HD] for g in range(ATT_G)], axis=1)
        q_t = jnp.concatenate([q_t, no_q] if kv == 0 else [no_q, q_t], axis=0).astype(BF16)
        return _dot(kwin[qb * BLK:qb * BLK + 3 * BLK], q_t)

    def softmax(x, s):
        qb, kv = pairs[x]
        key_ok = krow >= jnp.maximum(2 - (j * n_qb + qb), 0) * BLK
        s = jnp.where(key_ok, s + bias_ref[kv], -jnp.inf)
        sink = jnp.concatenate(
            [jnp.full((1, BLK), sink_ref[kv * ATT_G + g], F32) for g in range(ATT_G)], axis=1)
        mx = jnp.maximum(jnp.max(s, axis=0, keepdims=True), sink)
        p = jnp.exp(s - mx)
        den = jnp.sum(p, axis=0, keepdims=True) + jnp.exp(sink - mx)
        return p.astype(BF16), 1.0 / den

    def values(x, p, inv_den):
        qb, kv = pairs[x]
        o_t = _dot(v_t[kv * ATT_HD:(kv + 1) * ATT_HD, qb * BLK:qb * BLK + 3 * BLK], p) * inv_den
        o_t = jnp.concatenate([o_t[:, g * BLK:(g + 1) * BLK] for g in range(ATT_G)], axis=0)
        o_ref[qb * BLK:(qb + 1) * BLK, kv * gw:(kv + 1) * gw] = o_t.T.astype(BF16)

    s_all = [scores(x) for x in every]
    p_all = [softmax(x, s_all[x]) for x in every]
    for x in every:
        values(x, *p_all[x])


def _attention(q, kv, sink, bias, b, t):
    n = q.shape[0]
    tq = ATT_TILE
    nj = t // tq
    per = tq // BLK
    nblk = n // BLK

    def prev_map(i, j):
        return (jnp.maximum((i * nj + j) * per - 1, 0), 0)

    def next_map(i, j):
        return (jnp.minimum((i * nj + j + 1) * per, nblk - 1), 0)

    return pl.pallas_call(
        _attn_kernel,
        grid=(b, nj),
        in_specs=[pl.BlockSpec(memory_space=pltpu.SMEM),
                  pl.BlockSpec((tq, ATT_W), lambda i, j: (i * nj + j, 0)),
                  pl.BlockSpec((BLK, 2 * ATT_KV_W), prev_map),
                  pl.BlockSpec((tq, 2 * ATT_KV_W), lambda i, j: (i * nj + j, 0)),
                  pl.BlockSpec((BLK, 2 * ATT_KV_W), next_map),
                  _const_spec(bias.shape)],
        out_specs=pl.BlockSpec((tq, ATT_W), lambda i, j: (i * nj + j, 0)),
        out_shape=jax.ShapeDtypeStruct((n, ATT_W), BF16),
        compiler_params=_params(2),
        name="attention",
    )(sink, q, kv, kv, kv, bias)


def _block_diag(x, mask):
    tiled = jnp.concatenate([x.astype(F32)] * GROUP_H, axis=0)
    return jnp.where(mask, tiled, 0.0).astype(BF16)


def _head_sum(x):
    lanes = 2 * RW_N
    low = lax.broadcasted_iota(jnp.int32, (x.shape[0], lanes), 1) < RW_N
    out = []
    for c0 in range(0, RW_W, lanes):
        xc = x[:, c0:c0 + lanes]
        lo = jnp.sum(jnp.where(low, xc, 0.0), axis=1, keepdims=True)
        hi = jnp.sum(jnp.where(low, 0.0, xc), axis=1, keepdims=True)
        out.append(jnp.where(low, lo, hi))
    return jnp.concatenate(out, axis=1)


def _rwkv_kernel(reverse, final, *refs):
    if final:
        (rkv_ref, zl_ref, w0, w2, a0, a2, k_k, k_a,
         y0, a0o, a2o, g2, r_k, ln_w, ln_b,
         out_ref, s_ref, at_s, rt_s, bt_s, kt_s, v_s, bp_s, kp_s, pc_s, y_s) = refs
    else:
        (rkv_ref, zl_ref, w0, w2, a0, a2, k_k, k_a,
         out_ref, s_ref, at_s, rt_s, bt_s, kt_s, v_s, bp_s, kp_s, pc_s, y_s) = refs

    j = pl.program_id(1)

    @pl.when(j == 0)
    def _():
        s_ref[...] = jnp.zeros_like(s_ref)

    r = rkv_ref[:, :RW_W].astype(F32)
    k = rkv_ref[:, RW_W:2 * RW_W].astype(F32)
    v_bf = rkv_ref[:, 2 * RW_W:]
    v = v_bf.astype(F32)
    wd = zl_ref[:, :2 * DECAY_LORA].astype(F32)
    ad = zl_ref[:, 2 * DECAY_LORA:2 * DECAY_LORA + 2 * AAA_LORA]

    w = w0[...] + _dot(jnp.tanh(wd).astype(BF16), w2[...])
    lw = -math.exp(-0.5) * _sigmoid(w)
    a = _sigmoid(a0[...] + _dot(ad, a2[...]))
    kkk = k * k_k[...]
    ss = _head_sum(kkk * kkk)
    kk = kkk / jnp.maximum(jnp.sqrt(ss), 1e-12)
    kdir = k * (1.0 + (a - 1.0) * k_a[...])
    bb = kk * a

    rid = lax.broadcasted_iota(jnp.int32, (CHUNK, CHUNK), 0)
    cid = lax.broadcasted_iota(jnp.int32, (CHUNK, CHUNK), 1)
    lmat = jnp.where((cid >= rid) if reverse else (cid <= rid), 1.0, 0.0).astype(BF16)
    lw_hi = lw.astype(BF16)
    lw_lo = (lw - lw_hi.astype(F32)).astype(BF16)
    cum = jnp.concatenate(
        [_dot(lmat, lw_hi[c0:c0 + CHUNK]) + _dot(lmat, lw_lo[c0:c0 + CHUNK])
         for c0 in range(0, RW_TILE, CHUNK)], axis=0)

    at_s[...] = (-kk * jnp.exp(cum - lw)).astype(BF16)
    rt_s[...] = (r * jnp.exp(cum)).astype(BF16)
    e_neg = jnp.exp(-cum)
    bt_s[...] = (bb * e_neg).astype(BF16)
    kt_s[...] = (kdir * e_neg).astype(BF16)
    v_s[...] = v_bf
    for ci in range(RW_TILE // CHUNK):
        rows = slice(ci * CHUNK, (ci + 1) * CHUNK)
        end = ci * CHUNK if reverse else (ci + 1) * CHUNK - 1
        cum_end = cum[end:end + 1, :]
        e_end = jnp.exp(cum_end - cum[rows])
        bp_s[rows, :] = (bb[rows] * e_end).astype(BF16)
        kp_s[rows, :] = (kdir[rows] * e_end).astype(BF16)
        pc_s[ci:ci + 1, :] = jnp.exp(cum_end)

    trow = lax.broadcasted_iota(jnp.int32, (CHUNK, GROUP_W), 0)
    scol = lax.broadcasted_iota(jnp.int32, (CHUNK, GROUP_W), 1) % CHUNK
    strict = (scol > trow) if reverse else (scol < trow)
    incl = (scol >= trow) if reverse else (scol <= trow)
    eye = jnp.where(scol == trow, 1.0, 0.0).astype(F32)
    bdr = lax.broadcasted_iota(jnp.int32, (GROUP_H * CHUNK, GROUP_W), 0) // CHUNK
    bdc = lax.broadcasted_iota(jnp.int32, (GROUP_H * CHUNK, GROUP_W), 1) // RW_N
    bdmask = bdr == bdc
    bd = functools.partial(_block_diag, mask=bdmask)

    n_ch = RW_TILE // CHUNK
    scan = [n_ch - 1 - s if reverse else s for s in range(n_ch)]
    pairs = [(ci, g) for ci in scan for g in range(N_GROUPS)]
    every = range(len(pairs))

    def ld(ref, x):
        ci, g = pairs[x]
        return ref[ci * CHUNK:(ci + 1) * CHUNK, g * GROUP_W:(g + 1) * GROUP_W]

    a_t = [ld(at_s, x) for x in every]
    r_t = [ld(rt_s, x) for x in every]
    v_c = [ld(v_s, x) for x in every]
    ar = [jnp.concatenate([a_t[x], r_t[x]], axis=0) for x in every]
    g1 = [_dot_nt(ar[x], bd(ld(bt_s, x))) for x in every]
    g2_ = [_dot_nt(ar[x], bd(ld(kt_s, x))) for x in every]
    ab = [jnp.where(strict, g1[x][:CHUNK], 0.0) for x in every]
    rb = [jnp.where(incl, g1[x][CHUNK:], 0.0) for x in every]
    ak = [jnp.where(strict, g2_[x][:CHUNK], 0.0) for x in every]
    rk = [jnp.where(incl, g2_[x][CHUNK:], 0.0) for x in every]

    t_inv = [eye + ab[x] for x in every]
    pw = [_dot(ab[x].astype(BF16), bd(ab[x])) for x in every]
    n_lvl = int(math.log2(CHUNK))
    for lvl in range(1, n_lvl):
        if lvl < n_lvl - 1:
            res = [_dot(jnp.concatenate([pw[x], t_inv[x]], axis=0).astype(BF16), bd(pw[x]))
                   for x in every]
            pw = [res[x][:CHUNK] for x in every]
            t_inv = [t_inv[x] + res[x][CHUNK:] for x in every]
        else:
            t_inv = [t_inv[x] + _dot(t_inv[x].astype(BF16), bd(pw[x])) for x in every]

    wv = [_dot(ak[x].astype(BF16), bd(v_c[x])) for x in every]
    atw = [_dot(t_inv[x].astype(BF16), jnp.concatenate([bd(a_t[x]), bd(wv[x])], axis=1))
           for x in every]
    a_p = [atw[x][:, :GROUP_W] for x in every]
    w_p = [atw[x][:, GROUP_W:] for x in every]
    b_p = [ld(bp_s, x) for x in every]
    m_bd = [jnp.where(bdmask, _dot_tn(a_p[x].astype(BF16), b_p[x]), 0.0).astype(BF16)
            for x in every]
    n_bd = [jnp.where(bdmask, _dot_tn(jnp.concatenate([w_p[x].astype(BF16), v_c[x]], axis=0),
                                      jnp.concatenate([b_p[x], ld(kp_s, x)], axis=0)), 0.0)
            for x in every]
    r_p = [r_t[x].astype(F32) + _dot(rb[x].astype(BF16), bd(a_p[x])) for x in every]
    y_i = [_dot(jnp.concatenate([rb[x], rk[x]], axis=1).astype(BF16),
                jnp.concatenate([bd(w_p[x]), bd(v_c[x])], axis=0)) for x in every]

    state = [s_ref[g] for g in range(N_GROUPS)]
    for x in every:
        ci, g = pairs[x]
        s_bf = state[g].astype(BF16)
        y_s[ci * CHUNK:(ci + 1) * CHUNK, g * GROUP_W:(g + 1) * GROUP_W] = (
            _dot_nt(r_p[x].astype(BF16), s_bf) + y_i[x])
        state[g] = (state[g] * pc_s[ci:ci + 1, g * GROUP_W:(g + 1) * GROUP_W]
                    + _dot(s_bf, m_bd[x]) + n_bd[x])
    for g in range(N_GROUPS):
        s_ref[g] = state[g]

    y = y_s[...]
    if not final:
        out_ref[...] = y
    else:
        ytot = y + y0[...]
        inv_n = 1.0 / RW_N
        mu = _head_sum(ytot) * inv_n
        d = ytot - mu
        var = _head_sum(d * d) * inv_n
        yn = d * lax.rsqrt(var + GN_EPS) * ln_w[...] + ln_b[...]
        a_other = _sigmoid(a0o[...] + _dot(ad, a2o[...]))
        ksum = k * (2.0 + (a + a_other - 2.0) * k_a[...])
        bonus = _head_sum(r * ksum * r_k[...]) * v
        gd = zl_ref[:, 2 * DECAY_LORA + 2 * AAA_LORA:].astype(F32)
        gate = _dot(_sigmoid(gd).astype(BF16), g2[...])
        out_ref[...] = ((yn + bonus) * gate).astype(BF16)


def _rwkv_pass(reverse, rkv, zl, consts, extra, b, t):
    n = rkv.shape[0]
    tb = RW_TILE
    nj = t // tb
    final = reverse

    def cur_map(i, j):
        return (i * nj + (nj - 1 - j if reverse else j), 0)

    in_specs = ([pl.BlockSpec((tb, 3 * RW_W), cur_map), pl.BlockSpec((tb, RW_LORA_W), cur_map)]
                + [_const_spec(c.shape) for c in consts])
    args = [rkv, zl] + list(consts)
    if final:
        y0 = extra[0]
        in_specs += [pl.BlockSpec((tb, RW_W), cur_map)] + [_const_spec(c.shape) for c in extra[1:]]
        args += list(extra)
    out_dtype = BF16 if final else F32
    act = lambda: pltpu.VMEM((tb, RW_W), BF16)
    return pl.pallas_call(
        functools.partial(_rwkv_kernel, reverse, final),
        grid=(b, nj),
        in_specs=in_specs,
        out_specs=pl.BlockSpec((tb, RW_W), cur_map),
        out_shape=jax.ShapeDtypeStruct((n, RW_W), out_dtype),
        scratch_shapes=[pltpu.VMEM((N_GROUPS, GROUP_W, GROUP_W), F32),
                        act(), act(), act(), act(), act(), act(), act(),
                        pltpu.VMEM((tb // CHUNK, RW_W), F32),
                        pltpu.VMEM((tb, RW_W), F32)],
        compiler_params=_params(2),
        name="rwkv_bwd" if reverse else "rwkv_fwd",
    )(*args)


def _merge_kernel(x_ref, oa_ref, ob_ref, gpre_ref, wg_ref, wa_ref, wb_ref, wo_ref, gpost_ref, h_ref):
    x = x_ref[...]
    u = _rms(x, gpre_ref[...]).astype(BF16)
    gates = _sigmoid(_dot(u, wg_ref[...]))
    merged = (gates[:, :D_MODEL] * _dot(oa_ref[...], wa_ref[...])
              + gates[:, D_MODEL:] * _dot(ob_ref[...], wb_ref[...]))
    m = _dot(merged.astype(BF16), wo_ref[...])
    h_ref[...] = x + _rms(m, gpost_ref[...])


def _merge(x, oa, ob, gpre, wg, wa, wb, wo, gpost):
    n = x.shape[0]
    tm = MERGE_TILE
    row = lambda w: pl.BlockSpec((tm, w), lambda i: (i, 0))
    consts = [gpre, wg, wa, wb, wo, gpost]
    return pl.pallas_call(
        _merge_kernel,
        grid=(n // tm,),
        in_specs=[row(D_MODEL), row(ATT_W), row(RW_W)] + [_const_spec(c.shape) for c in consts],
        out_specs=row(D_MODEL),
        out_shape=jax.ShapeDtypeStruct((n, D_MODEL), F32),
        compiler_params=_params(1),
        name="merge",
    )(x, oa, ob, *consts)


def _gelu_tanh(x):
    return 0.5 * x * (1.0 + jnp.tanh(math.sqrt(2.0 / math.pi) * (x + 0.044715 * (x * x * x))))


def _ffn_kernel(hc_ref, hp_ref, hn_ref, gpre_ref, wup_ref, cw_ref, cb_ref, wdn_ref, gpost_ref, o_ref):
    j = pl.program_id(1)
    nj = pl.num_programs(1)
    tm = FFN_TILE
    ext = tm + 2 * FFN_HALO
    hc = hc_ref[...]
    hx = jnp.concatenate([hp_ref[...], hc, hn_ref[...]], axis=0)
    rid = lax.broadcasted_iota(jnp.int32, (ext, 1), 0)
    keep = jnp.logical_and(jnp.logical_or(rid >= FFN_HALO, j > 0),
                           jnp.logical_or(rid < FFN_HALO + tm, j < nj - 1))
    u = jnp.where(keep, _rms(hx, gpre_ref[...]), 0.0).astype(BF16)

    def up(c0):
        return [_dot(u, wup_ref[:, off:off + FFN_COLS]) for off in (c0, D_FF + c0)]

    def conv(hh, off):
        cols = slice(off, off + FFN_COLS)
        prev = pltpu.roll(hh, 1, 0)[FFN_HALO:FFN_HALO + tm]
        nxt = pltpu.roll(hh, ext - 1, 0)[FFN_HALO:FFN_HALO + tm]
        cur = hh[FFN_HALO:FFN_HALO + tm]
        return (prev * cw_ref[0:1, cols] + cur * cw_ref[1:2, cols]
                + nxt * cw_ref[2:3, cols] + cb_ref[:, cols])

    starts = list(range(0, D_FF, FFN_COLS))
    pending = up(starts[0])
    acts = []
    for i, c0 in enumerate(starts):
        hh = pending
        if i + 1 < len(starts):
            pending = up(starts[i + 1])
        acts.append((_gelu_tanh(conv(hh[0], c0)) * conv(hh[1], D_FF + c0)).astype(BF16))
    f = _dot(jnp.concatenate(acts, axis=1), wdn_ref[...])
    o_ref[...] = hc + _rms(f, gpost_ref[...])


def _ffn(h, gpre, wup, cw, cb, wdn, gpost, b, t):
    n = h.shape[0]
    tm = FFN_TILE
    nj = t // tm
    per = tm // FFN_HALO
    nhalo = n // FFN_HALO

    def prev_map(i, j):
        return (jnp.maximum((i * nj + j) * per - 1, 0), 0)

    def next_map(i, j):
        return (jnp.minimum((i * nj + j + 1) * per, nhalo - 1), 0)

    consts = [gpre, wup, cw, cb, wdn, gpost]
    return pl.pallas_call(
        _ffn_kernel,
        grid=(b, nj),
        in_specs=[pl.BlockSpec((tm, D_MODEL), lambda i, j: (i * nj + j, 0)),
                  pl.BlockSpec((FFN_HALO, D_MODEL), prev_map),
                  pl.BlockSpec((FFN_HALO, D_MODEL), next_map)]
                 + [_const_spec(c.shape) for c in consts],
        out_specs=pl.BlockSpec((tm, D_MODEL), lambda i, j: (i * nj + j, 0)),
        out_shape=jax.ShapeDtypeStruct((n, D_MODEL), F32),
        compiler_params=_params(2),
        name="ffn",
    )(h, h, h, *consts)


def _prepare(norm_mix_pre, norm_mix_post, norm_ffn_pre, norm_ffn_post, w_in, attn_sink,
             rw_mu_prev, rw_mu_next, rw_w0, rw_w2, rw_a0, rw_a2, rw_g2, rw_k_k, rw_k_a,
             rw_r_k, rw_ln_w, rw_ln_b, w_branch_attn, w_branch_rwkv, w_out,
             w_ffn_up, ffn_conv_w, ffn_conv_b, w_ffn_down):
    c_q = ATT_W
    c_kv = c_q + 2 * ATT_KV_W
    c_rkv = c_kv + 3 * RW_W
    c_zl = c_rkv + RW_LORA_W
    row = lambda p: p.reshape(1, -1).astype(F32)

    def lora_pad(w2, d, n_lora):
        z = jnp.zeros((2 * n_lora, RW_W), F32)
        return z.at[d * n_lora:(d + 1) * n_lora].set(w2[d]).astype(BF16)

    mu = jnp.stack([rw_mu_prev, rw_mu_next]).astype(F32)
    p = dict(
        g_mix_pre=row(norm_mix_pre), g_mix_post=row(norm_mix_post),
        g_ffn_pre=row(norm_ffn_pre), g_ffn_post=row(norm_ffn_post),
        wq=w_in[:, :c_q].astype(BF16), wkv=w_in[:, c_q:c_kv].astype(BF16),
        wrkv=w_in[:, c_kv:c_rkv].astype(BF16), wzl=w_in[:, c_rkv:c_zl].astype(BF16),
        wg=w_in[:, c_zl:].astype(BF16),
        sink=attn_sink.astype(F32), bias=_attn_bias(),
        mu_rkv=mu[:, :3 * RW_W], mu_zl=mu[:, 3 * RW_W:],
        w0=[row(rw_w0[d]) for d in range(2)],
        w2=[lora_pad(rw_w2, d, DECAY_LORA) for d in range(2)],
        a0=[row(rw_a0[d]) for d in range(2)],
        a2=[lora_pad(rw_a2, d, AAA_LORA) for d in range(2)],
        g2=rw_g2.astype(BF16), k_k=row(rw_k_k), k_a=row(rw_k_a), r_k=row(rw_r_k),
        ln_w=row(rw_ln_w), ln_b=row(rw_ln_b),
        wa=w_branch_attn.astype(BF16), wb=w_branch_rwkv.astype(BF16), wo=w_out.astype(BF16),
        wup=w_ffn_up.astype(BF16), cw=ffn_conv_w.astype(F32), cb=row(ffn_conv_b),
        wdn=w_ffn_down.astype(BF16),
    )
    return p


def _layer(x, p):
    b, t, _ = x.shape
    assert t % RW_TILE == 0 and t % ATT_TILE == 0 and t % FFN_TILE == 0
    assert (b * t) % PROJ_TILE == 0 and (b * t) % MERGE_TILE == 0
    x2 = x.reshape(b * t, D_MODEL)
    q, kv, rkv, zl = _in_proj(x2, p["g_mix_pre"], p["wq"], p["wkv"], p["wrkv"], p["wzl"],
                              p["mu_rkv"], p["mu_zl"], b, t)
    o_attn = _attention(q, kv, p["sink"], p["bias"], b, t)

    def consts(d):
        return [p["w0"][d], p["w2"][d], p["a0"][d], p["a2"][d],
                p["k_k"], p["k_a"]]

    y_fwd = _rwkv_pass(False, rkv, zl, consts(0), None, b, t)
    o_rwkv = _rwkv_pass(True, rkv, zl, consts(1),
                        [y_fwd, p["a0"][0], p["a2"][0], p["g2"], p["r_k"], p["ln_w"], p["ln_b"]], b, t)
    h = _merge(x2, o_attn, o_rwkv, p["g_mix_pre"], p["wg"], p["wa"], p["wb"], p["wo"], p["g_mix_post"])
    out = _ffn(h, p["g_ffn_pre"], p["wup"], p["cw"], p["cb"], p["wdn"], p["g_ffn_post"], b, t)
    return out.reshape(b, t, D_MODEL)


def kernel(x_prompt, x_sample, norm_mix_pre, norm_mix_post, norm_ffn_pre, norm_ffn_post, w_in, attn_sink, rw_mu_prev, rw_mu_next, rw_w0, rw_w2, rw_a0, rw_a2, rw_g2, rw_k_k, rw_k_a, rw_r_k, rw_ln_w, rw_ln_b, w_branch_attn, w_branch_rwkv, w_out, w_ffn_up, ffn_conv_w, ffn_conv_b, w_ffn_down):
    weights = (norm_mix_pre, norm_mix_post, norm_ffn_pre, norm_ffn_post, w_in, attn_sink,
               rw_mu_prev, rw_mu_next, rw_w0, rw_w2, rw_a0, rw_a2, rw_g2, rw_k_k, rw_k_a,
               rw_r_k, rw_ln_w, rw_ln_b, w_branch_attn, w_branch_rwkv, w_out,
               w_ffn_up, ffn_conv_w, ffn_conv_b, w_ffn_down)
    depth = w_in.shape[0]
    layers = [_prepare(*(w[l] for w in weights)) for l in range(depth)]

    def run(x):
        for p in layers:
            x = _layer(x, p)
        return x

    return (run(x_prompt), run(x_sample))
```

```python
import functools
import math

import numpy as np
import jax
import jax.numpy as jnp
from jax import lax
from jax.experimental import pallas as pl
from jax.experimental.pallas import tpu as pltpu

F32 = jnp.float32
BF16 = jnp.bfloat16

D_MODEL = 1024
ATT_H = 8
ATT_KV = 2
ATT_G = ATT_H // ATT_KV
ATT_HD = 64
ATT_W = ATT_H * ATT_HD
ATT_KV_W = ATT_KV * ATT_HD
WINDOW = 128
BLK = 128
ATT_SCALE = 1.0 / math.sqrt(ATT_HD)
RW_H = 8
RW_N = 64
RW_W = RW_H * RW_N
DECAY_LORA = 64
AAA_LORA = 64
GATE_LORA = 160
RW_LORA_W = 2 * DECAY_LORA + 2 * AAA_LORA + GATE_LORA
RW_MIX_W = 3 * RW_W + RW_LORA_W
GATE_W = 2 * D_MODEL
D_FF = 2816
NORM_EPS = 1e-6
GN_EPS = 64e-5

VMEM_LIMIT_BYTES = 56 * 2**20

CHUNK = 64
GROUP_H = 4
GROUP_W = GROUP_H * RW_N
N_GROUPS = RW_H // GROUP_H
RW_TILE = 512

PROJ_TILE = 512
PROJ_HALO = 8
ATT_TILE = 512
MERGE_TILE = 512
FFN_TILE = 512
FFN_HALO = 8
FFN_COLS = 256


def _dot(a, b):
    return jnp.dot(a, b, preferred_element_type=F32)


def _dot_nt(a, b):
    return lax.dot_general(a, b, (((1,), (1,)), ((), ())), preferred_element_type=F32)


def _dot_tn(a, b):
    return lax.dot_general(a, b, (((0,), (0,)), ((), ())), preferred_element_type=F32)


def _sigmoid(x):
    return 1.0 / (1.0 + jnp.exp(-x))


def _rms(x, g):
    return x * lax.rsqrt(jnp.mean(x * x, axis=-1, keepdims=True) + NORM_EPS) * g


def _const_spec(shape):
    nd = len(shape)
    return pl.BlockSpec(shape, lambda *_: (0,) * nd, pipeline_mode=pl.Buffered(1))


def _params(n_axes):
    return pltpu.CompilerParams(dimension_semantics=("arbitrary",) * n_axes,
                                vmem_limit_bytes=VMEM_LIMIT_BYTES)


def _in_proj_kernel(xc_ref, xp_ref, xn_ref, g_ref, wq_ref, wkv_ref, wrkv_ref, wzl_ref, mu_rkv_ref, mu_zl_ref,
                    q_ref, kv_ref, rkv_ref, zl_ref):
    j = pl.program_id(1)
    nj = pl.num_programs(1)
    tm = PROJ_TILE
    ext = tm + 2 * PROJ_HALO
    x = jnp.concatenate([xp_ref[...], xc_ref[...], xn_ref[...]], axis=0)
    rid = lax.broadcasted_iota(jnp.int32, (ext, 1), 0)
    keep = jnp.logical_and(jnp.logical_or(rid >= PROJ_HALO, j > 0),
                           jnp.logical_or(rid < PROJ_HALO + tm, j < nj - 1))
    u = jnp.where(keep, _rms(x, g_ref[...]), 0.0).astype(BF16)
    um = u[PROJ_HALO:PROJ_HALO + tm]
    q_ref[...] = (_dot(um, wq_ref[...]) * ATT_SCALE).astype(BF16)
    kv_ref[...] = _dot(um, wkv_ref[...]).astype(BF16)

    def shifted(w_ref, mu_ref, o_ref, cols):
        z = _dot(u, w_ref[:, cols])
        c = z[PROJ_HALO:PROJ_HALO + tm]
        up = pltpu.roll(z, 1, 0)[PROJ_HALO:PROJ_HALO + tm]
        dn = pltpu.roll(z, ext - 1, 0)[PROJ_HALO:PROJ_HALO + tm]
        o_ref[:, cols] = (c + mu_ref[0:1, cols] * (up - c) + mu_ref[1:2, cols] * (dn - c)).astype(BF16)

    for c0 in range(0, 3 * RW_W, RW_W):
        shifted(wrkv_ref, mu_rkv_ref, rkv_ref, slice(c0, c0 + RW_W))
    shifted(wzl_ref, mu_zl_ref, zl_ref, slice(0, RW_LORA_W))


def _in_proj(x, g, wq, wkv, wrkv, wzl, mu_rkv, mu_zl, b, t):
    n = x.shape[0]
    tm = PROJ_TILE
    nj = t // tm
    per = tm // PROJ_HALO
    nhalo = n // PROJ_HALO

    def prev_map(i, j):
        return (jnp.maximum((i * nj + j) * per - 1, 0), 0)

    def next_map(i, j):
        return (jnp.minimum((i * nj + j + 1) * per, nhalo - 1), 0)

    row = lambda w: pl.BlockSpec((tm, w), lambda i, j: (i * nj + j, 0))
    consts = [g, wq, wkv, wrkv, wzl, mu_rkv, mu_zl]
    return pl.pallas_call(
        _in_proj_kernel,
        grid=(b, nj),
        in_specs=[row(D_MODEL), pl.BlockSpec((PROJ_HALO, D_MODEL), prev_map),
                  pl.BlockSpec((PROJ_HALO, D_MODEL), next_map)] + [_const_spec(c.shape) for c in consts],
        out_specs=[row(ATT_W), row(2 * ATT_KV_W), row(3 * RW_W), row(RW_LORA_W)],
        out_shape=[jax.ShapeDtypeStruct((n, ATT_W), BF16),
                   jax.ShapeDtypeStruct((n, 2 * ATT_KV_W), BF16),
                   jax.ShapeDtypeStruct((n, 3 * RW_W), BF16),
                   jax.ShapeDtypeStruct((n, RW_LORA_W), BF16)],
        compiler_params=_params(2),
        name="in_proj",
    )(x, x, x, *consts)


def _attn_bias():
    slopes = np.exp2(-8.0 / ATT_H * np.arange(1, ATT_H + 1, dtype=np.float64))
    dist = np.abs(np.arange(BLK)[:, None] - (np.arange(3 * BLK)[None, :] - BLK))
    bias = -(slopes[:, None, None] * dist[None])
    bias = np.where(dist[None] <= WINDOW, bias, -np.inf)
    bias = bias.reshape(ATT_KV, ATT_G * BLK, 3 * BLK).transpose(0, 2, 1)
    return jnp.asarray(bias, F32)


def _attn_kernel(sink_ref, q_ref, kvp_ref, kvc_ref, kvn_ref, bias_ref, o_ref):
    j = pl.program_id(1)
    nj = pl.num_programs(1)
    has_next = jnp.where(j < nj - 1, 1.0, 0.0).astype(F32)
    kvwin = jnp.concatenate([kvp_ref[...].astype(F32), kvc_ref[...].astype(F32),
                             kvn_ref[...].astype(F32) * has_next], axis=0)
    kwin = kvwin[:, :ATT_KV_W].astype(BF16)
    v_t = kvwin[:, ATT_KV_W:].T.astype(BF16)
    krow = lax.broadcasted_iota(jnp.int32, (3 * BLK, 1), 0)
    n_qb = ATT_TILE // BLK
    gw = ATT_G * ATT_HD
    no_q = jnp.zeros((ATT_HD, ATT_G * BLK), F32)

    pairs = [(qb, kv) for qb in range(n_qb) for kv in range(ATT_KV)]
    every = range(len(pairs))

    def scores(x):
        qb, kv = pairs[x]
        q_t = q_ref[qb * BLK:(qb + 1) * BLK, kv * gw:(kv + 1) * gw].astype(F32).T
        q_t = jnp.concatenate([q_t[g * ATT_HD:(g + 1) * ATT_HD] for g in range(ATT_G)], axis=1)
        q_t = jnp.concatenate([q_t, no_q] if kv == 0 else [no_q, q_t], axis=0).astype(BF16)
        return _dot(kwin[qb * BLK:qb * BLK + 3 * BLK], q_t)

    def softmax(x, s):
        qb, kv = pairs[x]
        key_ok = krow >= jnp.maximum(2 - (j * n_qb + qb), 0) * BLK
        s = jnp.where(key_ok, s + bias_ref[kv], -jnp.inf)
        sink = jnp.concatenate(
            [jnp.full((1, BLK), sink_ref[kv * ATT_G + g], F32) for g in range(ATT_G)], axis=1)
        mx = jnp.maximum(jnp.max(s, axis=0, keepdims=True), sink)
        p = jnp.exp(s - mx)
        den = jnp.sum(p, axis=0, keepdims=True) + jnp.exp(sink - mx)
        return p.astype(BF16), 1.0 / den

    def values(x, p, inv_den):
        qb, kv = pairs[x]
        o_t = _dot(v_t[kv * ATT_HD:(kv + 1) * ATT_HD, qb * BLK:qb * BLK + 3 * BLK], p) * inv_den
        o_t = jnp.concatenate([o_t[:, g * BLK:(g + 1) * BLK] for g in range(ATT_G)], axis=0)
        o_ref[qb * BLK:(qb + 1) * BLK, kv * gw:(kv + 1) * gw] = o_t.T.astype(BF16)

    s_all = [scores(x) for x in every]
    p_all = [softmax(x, s_all[x]) for x in every]
    for x in every:
        values(x, *p_all[x])


def _attention(q, kv, sink, bias, b, t):
    n = q.shape[0]
    tq = ATT_TILE
    nj = t // tq
    per = tq // BLK
    nblk = n // BLK

    def prev_map(i, j):
        return (jnp.maximum((i * nj + j) * per - 1, 0), 0)

    def next_map(i, j):
        return (jnp.minimum((i * nj + j + 1) * per, nblk - 1), 0)

    return pl.pallas_call(
        _attn_kernel,
        grid=(b, nj),
        in_specs=[pl.BlockSpec(memory_space=pltpu.SMEM),
                  pl.BlockSpec((tq, ATT_W), lambda i, j: (i * nj + j, 0)),
                  pl.BlockSpec((BLK, 2 * ATT_KV_W), prev_map),
                  pl.BlockSpec((tq, 2 * ATT_KV_W), lambda i, j: (i * nj + j, 0)),
                  pl.BlockSpec((BLK, 2 * ATT_KV_W), next_map),
                  _const_spec(bias.shape)],
        out_specs=pl.BlockSpec((tq, ATT_W), lambda i, j: (i * nj + j, 0)),
        out_shape=jax.ShapeDtypeStruct((n, ATT_W), BF16),
        compiler_params=_params(2),
        name="attention",
    )(sink, q, kv, kv, kv, bias)


def _block_diag(x, mask):
    tiled = jnp.concatenate([x.astype(F32)] * GROUP_H, axis=0)
    return jnp.where(mask, tiled, 0.0).astype(BF16)


def _head_sum(x):
    lanes = 2 * RW_N
    low = lax.broadcasted_iota(jnp.int32, (x.shape[0], lanes), 1) < RW_N
    out = []
    for c0 in range(0, RW_W, lanes):
        xc = x[:, c0:c0 + lanes]
        lo = jnp.sum(jnp.where(low, xc, 0.0), axis=1, keepdims=True)
        hi = jnp.sum(jnp.where(low, 0.0, xc), axis=1, keepdims=True)
        out.append(jnp.where(low, lo, hi))
    return jnp.concatenate(out, axis=1)


def _rwkv_kernel(reverse, final, *refs):
    if final:
        (rkv_ref, zl_ref, w0, w2, a0, a2, k_k, k_a,
         y0, a0o, a2o, g2, r_k, ln_w, ln_b, out_ref, s_ref, y_s) = refs
    else:
        (rkv_ref, zl_ref, w0, w2, a0, a2, k_k, k_a, out_ref, s_ref, y_s) = refs

    j = pl.program_id(1)

    @pl.when(j == 0)
    def _():
        s_ref[...] = jnp.zeros_like(s_ref)

    rid = lax.broadcasted_iota(jnp.int32, (CHUNK, CHUNK), 0)
    cid = lax.broadcasted_iota(jnp.int32, (CHUNK, CHUNK), 1)
    lmat = jnp.where((cid >= rid) if reverse else (cid <= rid), 1.0, 0.0).astype(BF16)

    def prepare(chunks, res):
        lo = min(chunks) * CHUNK
        rows = slice(lo, (max(chunks) + 1) * CHUNK)
        r = rkv_ref[rows, :RW_W].astype(F32)
        k = rkv_ref[rows, RW_W:2 * RW_W].astype(F32)
        v_bf = rkv_ref[rows, 2 * RW_W:]
        wd = zl_ref[rows, :2 * DECAY_LORA].astype(F32)
        ad = zl_ref[rows, 2 * DECAY_LORA:2 * DECAY_LORA + 2 * AAA_LORA]
        w = w0[...] + _dot(jnp.tanh(wd).astype(BF16), w2[...])
        lw = -math.exp(-0.5) * _sigmoid(w)
        a = _sigmoid(a0[...] + _dot(ad, a2[...]))
        yield
        lw_hi = lw.astype(BF16)
        lw_lo = (lw - lw_hi.astype(F32)).astype(BF16)
        cum = jnp.concatenate(
            [_dot(lmat, lw_hi[c0:c0 + CHUNK]) + _dot(lmat, lw_lo[c0:c0 + CHUNK])
             for c0 in range(0, len(chunks) * CHUNK, CHUNK)], axis=0)
        yield
        kkk = k * k_k[...]
        kk = kkk / jnp.maximum(jnp.sqrt(_head_sum(kkk * kkk)), 1e-12)
        kdir = k * (1.0 + (a - 1.0) * k_a[...])
        bb = kk * a
        yield
        res["at"] = (-kk * jnp.exp(cum - lw)).astype(BF16)
        res["rt"] = (r * jnp.exp(cum)).astype(BF16)
        yield
        e_neg = jnp.exp(-cum)
        res["bt"] = (bb * e_neg).astype(BF16)
        res["kt"] = (kdir * e_neg).astype(BF16)
        res["v"] = v_bf
        yield
        bp, kp = [], []
        for ci in sorted(chunks):
            local = slice(ci * CHUNK - lo, (ci + 1) * CHUNK - lo)
            end = local.start if reverse else local.stop - 1
            cum_end = cum[end:end + 1, :]
            e_end = jnp.exp(cum_end - cum[local])
            bp.append((bb[local] * e_end).astype(BF16))
            kp.append((kdir[local] * e_end).astype(BF16))
            res["pc", ci] = jnp.exp(cum_end)
        res["bp"] = jnp.concatenate(bp, axis=0)
        res["kp"] = jnp.concatenate(kp, axis=0)
        yield
        if final:
            a_other = _sigmoid(a0o[...] + _dot(ad, a2o[...]))
            ksum = k * (2.0 + (a + a_other - 2.0) * k_a[...])
            res["bonus"] = _head_sum(r * ksum * r_k[...]) * v_bf.astype(F32)
            yield
            gd = zl_ref[rows, 2 * DECAY_LORA + 2 * AAA_LORA:].astype(F32)
            res["gate"] = _dot(_sigmoid(gd).astype(BF16), g2[...])
            yield

    trow = lax.broadcasted_iota(jnp.int32, (CHUNK, GROUP_W), 0)
    scol = lax.broadcasted_iota(jnp.int32, (CHUNK, GROUP_W), 1) % CHUNK
    strict = (scol > trow) if reverse else (scol < trow)
    incl = (scol >= trow) if reverse else (scol <= trow)
    eye = jnp.where(scol == trow, 1.0, 0.0).astype(F32)
    bdr = lax.broadcasted_iota(jnp.int32, (GROUP_H * CHUNK, GROUP_W), 0) // CHUNK
    bdc = lax.broadcasted_iota(jnp.int32, (GROUP_H * CHUNK, GROUP_W), 1) // RW_N
    bdmask = bdr == bdc
    bd = functools.partial(_block_diag, mask=bdmask)

    def solve(chunks, tok, res):
        lo = min(chunks) * CHUNK
        pairs = [(ci, g) for ci in chunks for g in range(N_GROUPS)]

        def ld(name):
            return {(ci, g): tok[name][ci * CHUNK - lo:(ci + 1) * CHUNK - lo, g * GROUP_W:(g + 1) * GROUP_W]
                    for ci, g in pairs}

        def stage(fn):
            return {x: fn(x) for x in pairs}

        a_t, r_t, v_c, b_t, k_t = ld("at"), ld("rt"), ld("v"), ld("bt"), ld("kt")
        ar = stage(lambda x: jnp.concatenate([a_t[x], r_t[x]], axis=0))
        g1 = stage(lambda x: _dot_nt(ar[x], bd(b_t[x])))
        yield
        g2_ = stage(lambda x: _dot_nt(ar[x], bd(k_t[x])))
        yield
        ab = stage(lambda x: jnp.where(strict, g1[x][:CHUNK], 0.0))
        rb = stage(lambda x: jnp.where(incl, g1[x][CHUNK:], 0.0))
        ak = stage(lambda x: jnp.where(strict, g2_[x][:CHUNK], 0.0))
        rk = stage(lambda x: jnp.where(incl, g2_[x][CHUNK:], 0.0))

        t_inv = stage(lambda x: eye + ab[x])
        pw = stage(lambda x: _dot(ab[x].astype(BF16), bd(ab[x])))
        yield
        n_lvl = int(math.log2(CHUNK))
        for lvl in range(1, n_lvl):
            if lvl < n_lvl - 1:
                both = stage(lambda x: _dot(jnp.concatenate([pw[x], t_inv[x]], axis=0).astype(BF16),
                                            bd(pw[x])))
                pw = stage(lambda x: both[x][:CHUNK])
                t_inv = stage(lambda x, t=t_inv: t[x] + both[x][CHUNK:])
            else:
                t_inv = stage(lambda x, t=t_inv: t[x] + _dot(t[x].astype(BF16), bd(pw[x])))
            yield

        wv = stage(lambda x: _dot(ak[x].astype(BF16), bd(v_c[x])))
        yield
        atw = stage(lambda x: _dot(t_inv[x].astype(BF16),
                                   jnp.concatenate([bd(a_t[x]), bd(wv[x])], axis=1)))
        yield
        a_p = stage(lambda x: atw[x][:, :GROUP_W])
        w_p = stage(lambda x: atw[x][:, GROUP_W:])
        b_p, k_p = ld("bp"), ld("kp")
        res["m"] = stage(lambda x: jnp.where(bdmask, _dot_tn(a_p[x].astype(BF16), b_p[x]), 0.0)
                         .astype(BF16))
        yield
        res["n"] = stage(lambda x: jnp.where(
            bdmask, _dot_tn(jnp.concatenate([w_p[x].astype(BF16), v_c[x]], axis=0),
                            jnp.concatenate([b_p[x], k_p[x]], axis=0)), 0.0))
        yield
        res["r"] = stage(lambda x: r_t[x].astype(F32) + _dot(rb[x].astype(BF16), bd(a_p[x])))
        yield
        res["y"] = stage(lambda x: _dot(jnp.concatenate([rb[x], rk[x]], axis=1).astype(BF16),
                                        jnp.concatenate([bd(w_p[x]), bd(v_c[x])], axis=0)))
        yield

    def run(gen, fill=None):
        for _ in gen:
            if fill is not None:
                next(fill, None)
        if fill is not None:
            for _ in fill:
                pass

    n_ch = RW_TILE // CHUNK
    scan = [n_ch - 1 - s if reverse else s for s in range(n_ch)]
    halves = [scan[:n_ch // 2], scan[n_ch // 2:]]
    tok = [{}, {}]
    sol = [{}, {}]
    run(prepare(halves[0], tok[0]))
    run(solve(halves[0], tok[0], sol[0]), fill=prepare(halves[1], tok[1]))
    run(solve(halves[1], tok[1], sol[1]))

    state = [s_ref[g] for g in range(N_GROUPS)]
    for h, chunks in enumerate(halves):
        for ci in chunks:
            for g in range(N_GROUPS):
                x = (ci, g)
                s_bf = state[g].astype(BF16)
                y_s[ci * CHUNK:(ci + 1) * CHUNK, g * GROUP_W:(g + 1) * GROUP_W] = (
                    _dot_nt(sol[h]["r"][x].astype(BF16), s_bf) + sol[h]["y"][x])
                state[g] = (state[g] * tok[h]["pc", ci][:, g * GROUP_W:(g + 1) * GROUP_W]
                            + _dot(s_bf, sol[h]["m"][x]) + sol[h]["n"][x])
    for g in range(N_GROUPS):
        s_ref[g] = state[g]

    y = y_s[...]
    if not final:
        out_ref[...] = y
    else:
        in_rows = sorted(range(2), key=lambda h: min(halves[h]))
        bonus = jnp.concatenate([tok[h]["bonus"] for h in in_rows], axis=0)
        gate = jnp.concatenate([tok[h]["gate"] for h in in_rows], axis=0)
        ytot = y + y0[...]
        inv_n = 1.0 / RW_N
        mu = _head_sum(ytot) * inv_n
        d = ytot - mu
        var = _head_sum(d * d) * inv_n
        yn = d * lax.rsqrt(var + GN_EPS) * ln_w[...] + ln_b[...]
        out_ref[...] = ((yn + bonus) * gate).astype(BF16)


def _rwkv_pass(reverse, rkv, zl, consts, extra, b, t):
    n = rkv.shape[0]
    tb = RW_TILE
    nj = t // tb
    final = reverse

    def cur_map(i, j):
        return (i * nj + (nj - 1 - j if reverse else j), 0)

    in_specs = ([pl.BlockSpec((tb, 3 * RW_W), cur_map), pl.BlockSpec((tb, RW_LORA_W), cur_map)]
                + [_const_spec(c.shape) for c in consts])
    args = [rkv, zl] + list(consts)
    if final:
        y0 = extra[0]
        in_specs += [pl.BlockSpec((tb, RW_W), cur_map)] + [_const_spec(c.shape) for c in extra[1:]]
        args += list(extra)
    out_dtype = BF16 if final else F32
    return pl.pallas_call(
        functools.partial(_rwkv_kernel, reverse, final),
        grid=(b, nj),
        in_specs=in_specs,
        out_specs=pl.BlockSpec((tb, RW_W), cur_map),
        out_shape=jax.ShapeDtypeStruct((n, RW_W), out_dtype),
        scratch_shapes=[pltpu.VMEM((N_GROUPS, GROUP_W, GROUP_W), F32),
                        pltpu.VMEM((tb, RW_W), F32)],
        compiler_params=_params(2),
        name="rwkv_bwd" if reverse else "rwkv_fwd",
    )(*args)


def _merge_kernel(x_ref, oa_ref, ob_ref, gpre_ref, wg_ref, wa_ref, wb_ref, wo_ref, gpost_ref, h_ref):
    x = x_ref[...]
    u = _rms(x, gpre_ref[...]).astype(BF16)
    gates = _sigmoid(_dot(u, wg_ref[...]))
    merged = (gates[:, :D_MODEL] * _dot(oa_ref[...], wa_ref[...])
              + gates[:, D_MODEL:] * _dot(ob_ref[...], wb_ref[...]))
    m = _dot(merged.astype(BF16), wo_ref[...])
    h_ref[...] = x + _rms(m, gpost_ref[...])


def _merge(x, oa, ob, gpre, wg, wa, wb, wo, gpost):
    n = x.shape[0]
    tm = MERGE_TILE
    row = lambda w: pl.BlockSpec((tm, w), lambda i: (i, 0))
    consts = [gpre, wg, wa, wb, wo, gpost]
    return pl.pallas_call(
        _merge_kernel,
        grid=(n // tm,),
        in_specs=[row(D_MODEL), row(ATT_W), row(RW_W)] + [_const_spec(c.shape) for c in consts],
        out_specs=row(D_MODEL),
        out_shape=jax.ShapeDtypeStruct((n, D_MODEL), F32),
        compiler_params=_params(1),
        name="merge",
    )(x, oa, ob, *consts)


def _gelu_tanh(x):
    return 0.5 * x * (1.0 + jnp.tanh(math.sqrt(2.0 / math.pi) * (x + 0.044715 * (x * x * x))))


def _ffn_kernel(hc_ref, hp_ref, hn_ref, gpre_ref, wup_ref, cw_ref, cb_ref, wdn_ref, gpost_ref, o_ref):
    j = pl.program_id(1)
    nj = pl.num_programs(1)
    tm = FFN_TILE
    ext = tm + 2 * FFN_HALO
    hc = hc_ref[...]
    hx = jnp.concatenate([hp_ref[...], hc, hn_ref[...]], axis=0)
    rid = lax.broadcasted_iota(jnp.int32, (ext, 1), 0)
    keep = jnp.logical_and(jnp.logical_or(rid >= FFN_HALO, j > 0),
                           jnp.logical_or(rid < FFN_HALO + tm, j < nj - 1))
    u = jnp.where(keep, _rms(hx, gpre_ref[...]), 0.0).astype(BF16)

    def up(c0):
        return [_dot(u, wup_ref[:, off:off + FFN_COLS]) for off in (c0, D_FF + c0)]

    def conv(hh, off):
        cols = slice(off, off + FFN_COLS)
        prev = pltpu.roll(hh, 1, 0)[FFN_HALO:FFN_HALO + tm]
        nxt = pltpu.roll(hh, ext - 1, 0)[FFN_HALO:FFN_HALO + tm]
        cur = hh[FFN_HALO:FFN_HALO + tm]
        return (prev * cw_ref[0:1, cols] + cur * cw_ref[1:2, cols]
                + nxt * cw_ref[2:3, cols] + cb_ref[:, cols])

    starts = list(range(0, D_FF, FFN_COLS))
    pending = up(starts[0])
    acts = []
    for i, c0 in enumerate(starts):
        hh = pending
        if i + 1 < len(starts):
            pending = up(starts[i + 1])
        acts.append((_gelu_tanh(conv(hh[0], c0)) * conv(hh[1], D_FF + c0)).astype(BF16))
    f = _dot(jnp.concatenate(acts, axis=1), wdn_ref[...])
    o_ref[...] = hc + _rms(f, gpost_ref[...])


def _ffn(h, gpre, wup, cw, cb, wdn, gpost, b, t):
    n = h.shape[0]
    tm = FFN_TILE
    nj = t // tm
    per = tm // FFN_HALO
    nhalo = n // FFN_HALO

    def prev_map(i, j):
        return (jnp.maximum((i * nj + j) * per - 1, 0), 0)

    def next_map(i, j):
        return (jnp.minimum((i * nj + j + 1) * per, nhalo - 1), 0)

    consts = [gpre, wup, cw, cb, wdn, gpost]
    return pl.pallas_call(
        _ffn_kernel,
        grid=(b, nj),
        in_specs=[pl.BlockSpec((tm, D_MODEL), lambda i, j: (i * nj + j, 0)),
                  pl.BlockSpec((FFN_HALO, D_MODEL), prev_map),
                  pl.BlockSpec((FFN_HALO, D_MODEL), next_map)]
                 + [_const_spec(c.shape) for c in consts],
        out_specs=pl.BlockSpec((tm, D_MODEL), lambda i, j: (i * nj + j, 0)),
        out_shape=jax.ShapeDtypeStruct((n, D_MODEL), F32),
        compiler_params=_params(2),
        name="ffn",
    )(h, h, h, *consts)


def _prepare(norm_mix_pre, norm_mix_post, norm_ffn_pre, norm_ffn_post, w_in, attn_sink,
             rw_mu_prev, rw_mu_next, rw_w0, rw_w2, rw_a0, rw_a2, rw_g2, rw_k_k, rw_k_a,
             rw_r_k, rw_ln_w, rw_ln_b, w_branch_attn, w_branch_rwkv, w_out,
             w_ffn_up, ffn_conv_w, ffn_conv_b, w_ffn_down):
    c_q = ATT_W
    c_kv = c_q + 2 * ATT_KV_W
    c_rkv = c_kv + 3 * RW_W
    c_zl = c_rkv + RW_LORA_W
    row = lambda p: p.reshape(1, -1).astype(F32)

    def lora_pad(w2, d, n_lora):
        z = jnp.zeros((2 * n_lora, RW_W), F32)
        return z.at[d * n_lora:(d + 1) * n_lora].set(w2[d]).astype(BF16)

    mu = jnp.stack([rw_mu_prev, rw_mu_next]).astype(F32)
    p = dict(
        g_mix_pre=row(norm_mix_pre), g_mix_post=row(norm_mix_post),
        g_ffn_pre=row(norm_ffn_pre), g_ffn_post=row(norm_ffn_post),
        wq=w_in[:, :c_q].astype(BF16), wkv=w_in[:, c_q:c_kv].astype(BF16),
        wrkv=w_in[:, c_kv:c_rkv].astype(BF16), wzl=w_in[:, c_rkv:c_zl].astype(BF16),
        wg=w_in[:, c_zl:].astype(BF16),
        sink=attn_sink.astype(F32), bias=_attn_bias(),
        mu_rkv=mu[:, :3 * RW_W], mu_zl=mu[:, 3 * RW_W:],
        w0=[row(rw_w0[d]) for d in range(2)],
        w2=[lora_pad(rw_w2, d, DECAY_LORA) for d in range(2)],
        a0=[row(rw_a0[d]) for d in range(2)],
        a2=[lora_pad(rw_a2, d, AAA_LORA) for d in range(2)],
        g2=rw_g2.astype(BF16), k_k=row(rw_k_k), k_a=row(rw_k_a), r_k=row(rw_r_k),
        ln_w=row(rw_ln_w), ln_b=row(rw_ln_b),
        wa=w_branch_attn.astype(BF16), wb=w_branch_rwkv.astype(BF16), wo=w_out.astype(BF16),
        wup=w_ffn_up.astype(BF16), cw=ffn_conv_w.astype(F32), cb=row(ffn_conv_b),
        wdn=w_ffn_down.astype(BF16),
    )
    return p


def _layer(x, p):
    b, t, _ = x.shape
    assert t % RW_TILE == 0 and t % ATT_TILE == 0 and t % FFN_TILE == 0
    assert (b * t) % PROJ_TILE == 0 and (b * t) % MERGE_TILE == 0
    x2 = x.reshape(b * t, D_MODEL)
    q, kv, rkv, zl = _in_proj(x2, p["g_mix_pre"], p["wq"], p["wkv"], p["wrkv"], p["wzl"],
                              p["mu_rkv"], p["mu_zl"], b, t)
    o_attn = _attention(q, kv, p["sink"], p["bias"], b, t)

    def consts(d):
        return [p["w0"][d], p["w2"][d], p["a0"][d], p["a2"][d],
                p["k_k"], p["k_a"]]

    y_fwd = _rwkv_pass(False, rkv, zl, consts(0), None, b, t)
    o_rwkv = _rwkv_pass(True, rkv, zl, consts(1),
                        [y_fwd, p["a0"][0], p["a2"][0], p["g2"], p["r_k"], p["ln_w"], p["ln_b"]], b, t)
    h = _merge(x2, o_attn, o_rwkv, p["g_mix_pre"], p["wg"], p["wa"], p["wb"], p["wo"], p["g_mix_post"])
    out = _ffn(h, p["g_ffn_pre"], p["wup"], p["cw"], p["cb"], p["wdn"], p["g_ffn_post"], b, t)
    return out.reshape(b, t, D_MODEL)


def kernel(x_prompt, x_sample, norm_mix_pre, norm_mix_post, norm_ffn_pre, norm_ffn_post, w_in, attn_sink, rw_mu_prev, rw_mu_next, rw_w0, rw_w2, rw_a0, rw_a2, rw_g2, rw_k_k, rw_k_a, rw_r_k, rw_ln_w, rw_ln_b, w_branch_attn, w_branch_rwkv, w_out, w_ffn_up, ffn_conv_w, ffn_conv_b, w_ffn_down):
    weights = (norm_mix_pre, norm_mix_post, norm_ffn_pre, norm_ffn_post, w_in, attn_sink,
               rw_mu_prev, rw_mu_next, rw_w0, rw_w2, rw_a0, rw_a2, rw_g2, rw_k_k, rw_k_a,
               rw_r_k, rw_ln_w, rw_ln_b, w_branch_attn, w_branch_rwkv, w_out,
               w_ffn_up, ffn_conv_w, ffn_conv_b, w_ffn_down)
    depth = w_in.shape[0]
    layers = [_prepare(*(w[l] for w in weights)) for l in range(depth)]

    def run(x):
        for p in layers:
            x = _layer(x, p)
        return x

    return (run(x_prompt), run(x_sample))
```

```python
import functools
import math

import numpy as np
import jax
import jax.numpy as jnp
from jax import lax
from jax.experimental import pallas as pl
from jax.experimental.pallas import tpu as pltpu

F32 = jnp.float32
BF16 = jnp.bfloat16

D_MODEL = 1024
ATT_H = 8
ATT_KV = 2
ATT_G = ATT_H // ATT_KV
ATT_HD = 64
ATT_W = ATT_H * ATT_HD
ATT_KV_W = ATT_KV * ATT_HD
WINDOW = 128
BLK = 128
ATT_SCALE = 1.0 / math.sqrt(ATT_HD)
RW_H = 8
RW_N = 64
RW_W = RW_H * RW_N
DECAY_LORA = 64
AAA_LORA = 64
GATE_LORA = 160
RW_LORA_W = 2 * DECAY_LORA + 2 * AAA_LORA + GATE_LORA
RW_MIX_W = 3 * RW_W + RW_LORA_W
GATE_W = 2 * D_MODEL
D_FF = 2816
NORM_EPS = 1e-6
GN_EPS = 64e-5

VMEM_LIMIT_BYTES = 56 * 2**20

CHUNK = 64
GROUP_H = 4
GROUP_W = GROUP_H * RW_N
N_GROUPS = RW_H // GROUP_H
RW_TILE = 512
RW_PARTS = 4

PROJ_TILE = 512
PROJ_HALO = 8
ATT_TILE = 512
MERGE_TILE = 512
FFN_TILE = 512
FFN_HALO = 8
FFN_COLS = 256


def _dot(a, b):
    return jnp.dot(a, b, preferred_element_type=F32)


def _dot_nt(a, b):
    return lax.dot_general(a, b, (((1,), (1,)), ((), ())), preferred_element_type=F32)


def _dot_tn(a, b):
    return lax.dot_general(a, b, (((0,), (0,)), ((), ())), preferred_element_type=F32)


def _sigmoid(x):
    return 1.0 / (1.0 + jnp.exp(-x))


def _rms(x, g):
    return x * lax.rsqrt(jnp.mean(x * x, axis=-1, keepdims=True) + NORM_EPS) * g


def _const_spec(shape):
    nd = len(shape)
    return pl.BlockSpec(shape, lambda *_: (0,) * nd, pipeline_mode=pl.Buffered(1))


def _params(n_axes):
    return pltpu.CompilerParams(dimension_semantics=("arbitrary",) * n_axes,
                                vmem_limit_bytes=VMEM_LIMIT_BYTES)


def _in_proj_kernel(xc_ref, xp_ref, xn_ref, g_ref, wq_ref, wkv_ref, wrkv_ref, wzl_ref, mu_rkv_ref, mu_zl_ref,
                    q_ref, kv_ref, rkv_ref, zl_ref):
    j = pl.program_id(1)
    nj = pl.num_programs(1)
    tm = PROJ_TILE
    ext = tm + 2 * PROJ_HALO
    x = jnp.concatenate([xp_ref[...], xc_ref[...], xn_ref[...]], axis=0)
    rid = lax.broadcasted_iota(jnp.int32, (ext, 1), 0)
    keep = jnp.logical_and(jnp.logical_or(rid >= PROJ_HALO, j > 0),
                           jnp.logical_or(rid < PROJ_HALO + tm, j < nj - 1))
    u = jnp.where(keep, _rms(x, g_ref[...]), 0.0).astype(BF16)
    um = u[PROJ_HALO:PROJ_HALO + tm]
    q_ref[...] = (_dot(um, wq_ref[...]) * ATT_SCALE).astype(BF16)
    kv_ref[...] = _dot(um, wkv_ref[...]).astype(BF16)

    def shifted(w_ref, mu_ref, o_ref, cols):
        z = _dot(u, w_ref[:, cols])
        c = z[PROJ_HALO:PROJ_HALO + tm]
        up = pltpu.roll(z, 1, 0)[PROJ_HALO:PROJ_HALO + tm]
        dn = pltpu.roll(z, ext - 1, 0)[PROJ_HALO:PROJ_HALO + tm]
        o_ref[:, cols] = (c + mu_ref[0:1, cols] * (up - c) + mu_ref[1:2, cols] * (dn - c)).astype(BF16)

    for c0 in range(0, 3 * RW_W, RW_W):
        shifted(wrkv_ref, mu_rkv_ref, rkv_ref, slice(c0, c0 + RW_W))
    shifted(wzl_ref, mu_zl_ref, zl_ref, slice(0, RW_LORA_W))


def _in_proj(x, g, wq, wkv, wrkv, wzl, mu_rkv, mu_zl, b, t):
    n = x.shape[0]
    tm = PROJ_TILE
    nj = t // tm
    per = tm // PROJ_HALO
    nhalo = n // PROJ_HALO

    def prev_map(i, j):
        return (jnp.maximum((i * nj + j) * per - 1, 0), 0)

    def next_map(i, j):
        return (jnp.minimum((i * nj + j + 1) * per, nhalo - 1), 0)

    row = lambda w: pl.BlockSpec((tm, w), lambda i, j: (i * nj + j, 0))
    consts = [g, wq, wkv, wrkv, wzl, mu_rkv, mu_zl]
    return pl.pallas_call(
        _in_proj_kernel,
        grid=(b, nj),
        in_specs=[row(D_MODEL), pl.BlockSpec((PROJ_HALO, D_MODEL), prev_map),
                  pl.BlockSpec((PROJ_HALO, D_MODEL), next_map)] + [_const_spec(c.shape) for c in consts],
        out_specs=[row(ATT_W), row(2 * ATT_KV_W), row(3 * RW_W), row(RW_LORA_W)],
        out_shape=[jax.ShapeDtypeStruct((n, ATT_W), BF16),
                   jax.ShapeDtypeStruct((n, 2 * ATT_KV_W), BF16),
                   jax.ShapeDtypeStruct((n, 3 * RW_W), BF16),
                   jax.ShapeDtypeStruct((n, RW_LORA_W), BF16)],
        compiler_params=_params(2),
        name="in_proj",
    )(x, x, x, *consts)


def _attn_bias():
    slopes = np.exp2(-8.0 / ATT_H * np.arange(1, ATT_H + 1, dtype=np.float64))
    dist = np.abs(np.arange(BLK)[:, None] - (np.arange(3 * BLK)[None, :] - BLK))
    bias = -(slopes[:, None, None] * dist[None])
    bias = np.where(dist[None] <= WINDOW, bias, -np.inf)
    bias = bias.reshape(ATT_KV, ATT_G * BLK, 3 * BLK).transpose(0, 2, 1)
    return jnp.asarray(bias, F32)


def _attn_kernel(sink_ref, q_ref, kvp_ref, kvc_ref, kvn_ref, bias_ref, o_ref):
    j = pl.program_id(1)
    nj = pl.num_programs(1)
    has_next = jnp.where(j < nj - 1, 1.0, 0.0).astype(F32)
    kvwin = jnp.concatenate([kvp_ref[...].astype(F32), kvc_ref[...].astype(F32),
                             kvn_ref[...].astype(F32) * has_next], axis=0)
    kwin = kvwin[:, :ATT_KV_W].astype(BF16)
    v_t = kvwin[:, ATT_KV_W:].T.astype(BF16)
    krow = lax.broadcasted_iota(jnp.int32, (3 * BLK, 1), 0)
    n_qb = ATT_TILE // BLK
    gw = ATT_G * ATT_HD
    no_q = jnp.zeros((ATT_HD, ATT_G * BLK), F32)

    pairs = [(qb, kv) for qb in range(n_qb) for kv in range(ATT_KV)]
    every = range(len(pairs))

    def scores(x):
        qb, kv = pairs[x]
        q_t = q_ref[qb * BLK:(qb + 1) * BLK, kv * gw:(kv + 1) * gw].astype(F32).T
        q_t = jnp.concatenate([q_t[g * ATT_HD:(g + 1) * ATT_HD] for g in range(ATT_G)], axis=1)
        q_t = jnp.concatenate([q_t, no_q] if kv == 0 else [no_q, q_t], axis=0).astype(BF16)
        return _dot(kwin[qb * BLK:qb * BLK + 3 * BLK], q_t)

    def softmax(x, s):
        qb, kv = pairs[x]
        key_ok = krow >= jnp.maximum(2 - (j * n_qb + qb), 0) * BLK
        s = jnp.where(key_ok, s + bias_ref[kv], -jnp.inf)
        sink = jnp.concatenate(
            [jnp.full((1, BLK), sink_ref[kv * ATT_G + g], F32) for g in range(ATT_G)], axis=1)
        mx = jnp.maximum(jnp.max(s, axis=0, keepdims=True), sink)
        p = jnp.exp(s - mx)
        den = jnp.sum(p, axis=0, keepdims=True) + jnp.exp(sink - mx)
        return p.astype(BF16), 1.0 / den

    def values(x, p, inv_den):
        qb, kv = pairs[x]
        o_t = _dot(v_t[kv * ATT_HD:(kv + 1) * ATT_HD, qb * BLK:qb * BLK + 3 * BLK], p) * inv_den
        o_t = jnp.concatenate([o_t[:, g * BLK:(g + 1) * BLK] for g in range(ATT_G)], axis=0)
        o_ref[qb * BLK:(qb + 1) * BLK, kv * gw:(kv + 1) * gw] = o_t.T.astype(BF16)

    s_all = [scores(x) for x in every]
    p_all = [softmax(x, s_all[x]) for x in every]
    for x in every:
        values(x, *p_all[x])


def _attention(q, kv, sink, bias, b, t):
    n = q.shape[0]
    tq = ATT_TILE
    nj = t // tq
    per = tq // BLK
    nblk = n // BLK

    def prev_map(i, j):
        return (jnp.maximum((i * nj + j) * per - 1, 0), 0)

    def next_map(i, j):
        return (jnp.minimum((i * nj + j + 1) * per, nblk - 1), 0)

    return pl.pallas_call(
        _attn_kernel,
        grid=(b, nj),
        in_specs=[pl.BlockSpec(memory_space=pltpu.SMEM),
                  pl.BlockSpec((tq, ATT_W), lambda i, j: (i * nj + j, 0)),
                  pl.BlockSpec((BLK, 2 * ATT_KV_W), prev_map),
                  pl.BlockSpec((tq, 2 * ATT_KV_W), lambda i, j: (i * nj + j, 0)),
                  pl.BlockSpec((BLK, 2 * ATT_KV_W), next_map),
                  _const_spec(bias.shape)],
        out_specs=pl.BlockSpec((tq, ATT_W), lambda i, j: (i * nj + j, 0)),
        out_shape=jax.ShapeDtypeStruct((n, ATT_W), BF16),
        compiler_params=_params(2),
        name="attention",
    )(sink, q, kv, kv, kv, bias)


def _block_diag(x, mask):
    tiled = jnp.concatenate([x.astype(F32)] * GROUP_H, axis=0)
    return jnp.where(mask, tiled, 0.0).astype(BF16)


def _head_sum(x):
    lanes = 2 * RW_N
    low = lax.broadcasted_iota(jnp.int32, (x.shape[0], lanes), 1) < RW_N
    out = []
    for c0 in range(0, RW_W, lanes):
        xc = x[:, c0:c0 + lanes]
        lo = jnp.sum(jnp.where(low, xc, 0.0), axis=1, keepdims=True)
        hi = jnp.sum(jnp.where(low, 0.0, xc), axis=1, keepdims=True)
        out.append(jnp.where(low, lo, hi))
    return jnp.concatenate(out, axis=1)


def _rwkv_kernel(reverse, final, *refs):
    if final:
        (rkv_ref, zl_ref, w0, w2, a0, a2, k_k, k_a,
         y0, a0o, a2o, g2, r_k, ln_w, ln_b, out_ref, s_ref, y_s) = refs
    else:
        (rkv_ref, zl_ref, w0, w2, a0, a2, k_k, k_a, out_ref, s_ref, y_s) = refs

    j = pl.program_id(1)

    @pl.when(j == 0)
    def _():
        s_ref[...] = jnp.zeros_like(s_ref)

    rid = lax.broadcasted_iota(jnp.int32, (CHUNK, CHUNK), 0)
    cid = lax.broadcasted_iota(jnp.int32, (CHUNK, CHUNK), 1)
    lmat = jnp.where((cid >= rid) if reverse else (cid <= rid), 1.0, 0.0).astype(BF16)

    def prepare(chunks, res):
        lo = min(chunks) * CHUNK
        rows = slice(lo, (max(chunks) + 1) * CHUNK)
        r = rkv_ref[rows, :RW_W].astype(F32)
        k = rkv_ref[rows, RW_W:2 * RW_W].astype(F32)
        v_bf = rkv_ref[rows, 2 * RW_W:]
        wd = zl_ref[rows, :2 * DECAY_LORA].astype(F32)
        ad = zl_ref[rows, 2 * DECAY_LORA:2 * DECAY_LORA + 2 * AAA_LORA]
        w = w0[...] + _dot(jnp.tanh(wd).astype(BF16), w2[...])
        lw = -math.exp(-0.5) * _sigmoid(w)
        a = _sigmoid(a0[...] + _dot(ad, a2[...]))
        yield
        lw_hi = lw.astype(BF16)
        lw_lo = (lw - lw_hi.astype(F32)).astype(BF16)
        cum = jnp.concatenate(
            [_dot(lmat, lw_hi[c0:c0 + CHUNK]) + _dot(lmat, lw_lo[c0:c0 + CHUNK])
             for c0 in range(0, len(chunks) * CHUNK, CHUNK)], axis=0)
        yield
        kkk = k * k_k[...]
        kk = kkk / jnp.maximum(jnp.sqrt(_head_sum(kkk * kkk)), 1e-12)
        kdir = k * (1.0 + (a - 1.0) * k_a[...])
        bb = kk * a
        yield
        res["at"] = (-kk * jnp.exp(cum - lw)).astype(BF16)
        res["rt"] = (r * jnp.exp(cum)).astype(BF16)
        yield
        e_neg = jnp.exp(-cum)
        res["bt"] = (bb * e_neg).astype(BF16)
        res["kt"] = (kdir * e_neg).astype(BF16)
        res["v"] = v_bf
        yield
        bp, kp = [], []
        for ci in sorted(chunks):
            local = slice(ci * CHUNK - lo, (ci + 1) * CHUNK - lo)
            end = local.start if reverse else local.stop - 1
            cum_end = cum[end:end + 1, :]
            e_end = jnp.exp(cum_end - cum[local])
            bp.append((bb[local] * e_end).astype(BF16))
            kp.append((kdir[local] * e_end).astype(BF16))
            res["pc", ci] = jnp.exp(cum_end)
        res["bp"] = jnp.concatenate(bp, axis=0)
        res["kp"] = jnp.concatenate(kp, axis=0)
        yield
        if final:
            a_other = _sigmoid(a0o[...] + _dot(ad, a2o[...]))
            ksum = k * (2.0 + (a + a_other - 2.0) * k_a[...])
            res["bonus"] = _head_sum(r * ksum * r_k[...]) * v_bf.astype(F32)
            yield
            gd = zl_ref[rows, 2 * DECAY_LORA + 2 * AAA_LORA:].astype(F32)
            res["gate"] = _dot(_sigmoid(gd).astype(BF16), g2[...])
            yield

    trow = lax.broadcasted_iota(jnp.int32, (CHUNK, GROUP_W), 0)
    scol = lax.broadcasted_iota(jnp.int32, (CHUNK, GROUP_W), 1) % CHUNK
    strict = (scol > trow) if reverse else (scol < trow)
    incl = (scol >= trow) if reverse else (scol <= trow)
    eye = jnp.where(scol == trow, 1.0, 0.0).astype(F32)
    bdr = lax.broadcasted_iota(jnp.int32, (GROUP_H * CHUNK, GROUP_W), 0) // CHUNK
    bdc = lax.broadcasted_iota(jnp.int32, (GROUP_H * CHUNK, GROUP_W), 1) // RW_N
    bdmask = bdr == bdc
    bd = functools.partial(_block_diag, mask=bdmask)

    def solve(chunks, tok, res):
        lo = min(chunks) * CHUNK
        pairs = [(ci, g) for ci in chunks for g in range(N_GROUPS)]

        def ld(name):
            return {(ci, g): tok[name][ci * CHUNK - lo:(ci + 1) * CHUNK - lo, g * GROUP_W:(g + 1) * GROUP_W]
                    for ci, g in pairs}

        def stage(fn):
            return {x: fn(x) for x in pairs}

        a_t, r_t, v_c, b_t, k_t = ld("at"), ld("rt"), ld("v"), ld("bt"), ld("kt")
        ar = stage(lambda x: jnp.concatenate([a_t[x], r_t[x]], axis=0))
        g1 = stage(lambda x: _dot_nt(ar[x], bd(b_t[x])))
        yield
        g2_ = stage(lambda x: _dot_nt(ar[x], bd(k_t[x])))
        yield
        ab = stage(lambda x: jnp.where(strict, g1[x][:CHUNK], 0.0))
        rb = stage(lambda x: jnp.where(incl, g1[x][CHUNK:], 0.0))
        ak = stage(lambda x: jnp.where(strict, g2_[x][:CHUNK], 0.0))
        rk = stage(lambda x: jnp.where(incl, g2_[x][CHUNK:], 0.0))

        t_inv = stage(lambda x: eye + ab[x])
        pw = stage(lambda x: _dot(ab[x].astype(BF16), bd(ab[x])))
        yield
        n_lvl = int(math.log2(CHUNK))
        for lvl in range(1, n_lvl):
            if lvl < n_lvl - 1:
                both = stage(lambda x: _dot(jnp.concatenate([pw[x], t_inv[x]], axis=0).astype(BF16),
                                            bd(pw[x])))
                pw = stage(lambda x: both[x][:CHUNK])
                t_inv = stage(lambda x, t=t_inv: t[x] + both[x][CHUNK:])
            else:
                t_inv = stage(lambda x, t=t_inv: t[x] + _dot(t[x].astype(BF16), bd(pw[x])))
            yield

        res["wv"] = stage(lambda x: _dot(ak[x].astype(BF16), bd(v_c[x])))
        b_p, k_p = ld("bp"), ld("kp")
        res["ar"] = ar
        res["t"] = stage(lambda x: t_inv[x].astype(BF16))
        res["rbk"] = stage(lambda x: jnp.concatenate([rb[x], rk[x]], axis=1).astype(BF16))
        res["v"] = v_c
        res["bkp"] = stage(lambda x: jnp.concatenate([b_p[x], k_p[x]], axis=0))
        yield

    def carry(chunks, tok, res, state):
        groups = range(N_GROUPS)
        for ci in chunks:
            a_s = [_dot_nt(res["ar"][ci, g], state[g].astype(BF16)) for g in groups]
            yield
            u = [_dot(res["t"][ci, g], bd(a_s[g][:CHUNK] + res["wv"][ci, g])) for g in groups]
            yield
            for g in groups:
                x = (ci, g)
                lanes = slice(g * GROUP_W, (g + 1) * GROUP_W)
                y_s[ci * CHUNK:(ci + 1) * CHUNK, lanes] = a_s[g][CHUNK:] + _dot(
                    res["rbk"][x], jnp.concatenate([bd(u[g]), bd(res["v"][x])], axis=0))
                upd = _dot_tn(jnp.concatenate([u[g].astype(BF16), res["v"][x]], axis=0), res["bkp"][x])
                state[g] = state[g] * tok["pc", ci][:, lanes] + jnp.where(bdmask, upd, 0.0)
            yield

    def run(gen, *fill):
        for _ in gen:
            for f in fill:
                next(f, None)
        for f in fill:
            for _ in f:
                pass

    n_ch = RW_TILE // CHUNK
    scan = [n_ch - 1 - s if reverse else s for s in range(n_ch)]
    per = n_ch // RW_PARTS
    parts = [scan[p * per:(p + 1) * per] for p in range(RW_PARTS)]
    tok = [{} for _ in parts]
    sol = [{} for _ in parts]
    state = [s_ref[g] for g in range(N_GROUPS)]
    run(prepare(parts[0], tok[0]))
    for p in range(RW_PARTS):
        fill = []
        if p > 0:
            fill.append(carry(parts[p - 1], tok[p - 1], sol[p - 1], state))
        if p + 1 < RW_PARTS:
            fill.append(prepare(parts[p + 1], tok[p + 1]))
        run(solve(parts[p], tok[p], sol[p]), *fill)
    run(carry(parts[-1], tok[-1], sol[-1], state))
    for g in range(N_GROUPS):
        s_ref[g] = state[g]

    y = y_s[...]
    if not final:
        out_ref[...] = y
    else:
        in_rows = sorted(range(RW_PARTS), key=lambda p: min(parts[p]))
        bonus = jnp.concatenate([tok[p]["bonus"] for p in in_rows], axis=0)
        gate = jnp.concatenate([tok[p]["gate"] for p in in_rows], axis=0)
        ytot = y + y0[...]
        inv_n = 1.0 / RW_N
        mu = _head_sum(ytot) * inv_n
        d = ytot - mu
        var = _head_sum(d * d) * inv_n
        yn = d * lax.rsqrt(var + GN_EPS) * ln_w[...] + ln_b[...]
        out_ref[...] = ((yn + bonus) * gate).astype(BF16)


def _rwkv_pass(reverse, rkv, zl, consts, extra, b, t):
    n = rkv.shape[0]
    tb = RW_TILE
    nj = t // tb
    final = reverse

    def cur_map(i, j):
        return (i * nj + (nj - 1 - j if reverse else j), 0)

    in_specs = ([pl.BlockSpec((tb, 3 * RW_W), cur_map), pl.BlockSpec((tb, RW_LORA_W), cur_map)]
                + [_const_spec(c.shape) for c in consts])
    args = [rkv, zl] + list(consts)
    if final:
        y0 = extra[0]
        in_specs += [pl.BlockSpec((tb, RW_W), cur_map)] + [_const_spec(c.shape) for c in extra[1:]]
        args += list(extra)
    out_dtype = BF16 if final else F32
    return pl.pallas_call(
        functools.partial(_rwkv_kernel, reverse, final),
        grid=(b, nj),
        in_specs=in_specs,
        out_specs=pl.BlockSpec((tb, RW_W), cur_map),
        out_shape=jax.ShapeDtypeStruct((n, RW_W), out_dtype),
        scratch_shapes=[pltpu.VMEM((N_GROUPS, GROUP_W, GROUP_W), F32),
                        pltpu.VMEM((tb, RW_W), F32)],
        compiler_params=_params(2),
        name="rwkv_bwd" if reverse else "rwkv_fwd",
    )(*args)


def _merge_kernel(x_ref, oa_ref, ob_ref, gpre_ref, wg_ref, wa_ref, wb_ref, wo_ref, gpost_ref, h_ref):
    x = x_ref[...]
    u = _rms(x, gpre_ref[...]).astype(BF16)
    gates = _sigmoid(_dot(u, wg_ref[...]))
    merged = (gates[:, :D_MODEL] * _dot(oa_ref[...], wa_ref[...])
              + gates[:, D_MODEL:] * _dot(ob_ref[...], wb_ref[...]))
    m = _dot(merged.astype(BF16), wo_ref[...])
    h_ref[...] = x + _rms(m, gpost_ref[...])


def _merge(x, oa, ob, gpre, wg, wa, wb, wo, gpost):
    n = x.shape[0]
    tm = MERGE_TILE
    row = lambda w: pl.BlockSpec((tm, w), lambda i: (i, 0))
    consts = [gpre, wg, wa, wb, wo, gpost]
    return pl.pallas_call(
        _merge_kernel,
        grid=(n // tm,),
        in_specs=[row(D_MODEL), row(ATT_W), row(RW_W)] + [_const_spec(c.shape) for c in consts],
        out_specs=row(D_MODEL),
        out_shape=jax.ShapeDtypeStruct((n, D_MODEL), F32),
        compiler_params=_params(1),
        name="merge",
    )(x, oa, ob, *consts)


def _gelu_tanh(x):
    return 0.5 * x * (1.0 + jnp.tanh(math.sqrt(2.0 / math.pi) * (x + 0.044715 * (x * x * x))))


def _ffn_kernel(hc_ref, hp_ref, hn_ref, gpre_ref, wup_ref, cw_ref, cb_ref, wdn_ref, gpost_ref, o_ref):
    j = pl.program_id(1)
    nj = pl.num_programs(1)
    tm = FFN_TILE
    ext = tm + 2 * FFN_HALO
    hc = hc_ref[...]
    hx = jnp.concatenate([hp_ref[...], hc, hn_ref[...]], axis=0)
    rid = lax.broadcasted_iota(jnp.int32, (ext, 1), 0)
    keep = jnp.logical_and(jnp.logical_or(rid >= FFN_HALO, j > 0),
                           jnp.logical_or(rid < FFN_HALO + tm, j < nj - 1))
    u = jnp.where(keep, _rms(hx, gpre_ref[...]), 0.0).astype(BF16)

    def up(c0):
        return [_dot(u, wup_ref[:, off:off + FFN_COLS]) for off in (c0, D_FF + c0)]

    def conv(hh, off):
        cols = slice(off, off + FFN_COLS)
        prev = pltpu.roll(hh, 1, 0)[FFN_HALO:FFN_HALO + tm]
        nxt = pltpu.roll(hh, ext - 1, 0)[FFN_HALO:FFN_HALO + tm]
        cur = hh[FFN_HALO:FFN_HALO + tm]
        return (prev * cw_ref[0:1, cols] + cur * cw_ref[1:2, cols]
                + nxt * cw_ref[2:3, cols] + cb_ref[:, cols])

    starts = list(range(0, D_FF, FFN_COLS))
    pending = up(starts[0])
    acts = []
    for i, c0 in enumerate(starts):
        hh = pending
        if i + 1 < len(starts):
            pending = up(starts[i + 1])
        acts.append((_gelu_tanh(conv(hh[0], c0)) * conv(hh[1], D_FF + c0)).astype(BF16))
    f = _dot(jnp.concatenate(acts, axis=1), wdn_ref[...])
    o_ref[...] = hc + _rms(f, gpost_ref[...])


def _ffn(h, gpre, wup, cw, cb, wdn, gpost, b, t):
    n = h.shape[0]
    tm = FFN_TILE
    nj = t // tm
    per = tm // FFN_HALO
    nhalo = n // FFN_HALO

    def prev_map(i, j):
        return (jnp.maximum((i * nj + j) * per - 1, 0), 0)

    def next_map(i, j):
        return (jnp.minimum((i * nj + j + 1) * per, nhalo - 1), 0)

    consts = [gpre, wup, cw, cb, wdn, gpost]
    return pl.pallas_call(
        _ffn_kernel,
        grid=(b, nj),
        in_specs=[pl.BlockSpec((tm, D_MODEL), lambda i, j: (i * nj + j, 0)),
                  pl.BlockSpec((FFN_HALO, D_MODEL), prev_map),
                  pl.BlockSpec((FFN_HALO, D_MODEL), next_map)]
                 + [_const_spec(c.shape) for c in consts],
        out_specs=pl.BlockSpec((tm, D_MODEL), lambda i, j: (i * nj + j, 0)),
        out_shape=jax.ShapeDtypeStruct((n, D_MODEL), F32),
        compiler_params=_params(2),
        name="ffn",
    )(h, h, h, *consts)


def _prepare(norm_mix_pre, norm_mix_post, norm_ffn_pre, norm_ffn_post, w_in, attn_sink,
             rw_mu_prev, rw_mu_next, rw_w0, rw_w2, rw_a0, rw_a2, rw_g2, rw_k_k, rw_k_a,
             rw_r_k, rw_ln_w, rw_ln_b, w_branch_attn, w_branch_rwkv, w_out,
             w_ffn_up, ffn_conv_w, ffn_conv_b, w_ffn_down):
    c_q = ATT_W
    c_kv = c_q + 2 * ATT_KV_W
    c_rkv = c_kv + 3 * RW_W
    c_zl = c_rkv + RW_LORA_W
    row = lambda p: p.reshape(1, -1).astype(F32)

    def lora_pad(w2, d, n_lora):
        z = jnp.zeros((2 * n_lora, RW_W), F32)
        return z.at[d * n_lora:(d + 1) * n_lora].set(w2[d]).astype(BF16)

    mu = jnp.stack([rw_mu_prev, rw_mu_next]).astype(F32)
    p = dict(
        g_mix_pre=row(norm_mix_pre), g_mix_post=row(norm_mix_post),
        g_ffn_pre=row(norm_ffn_pre), g_ffn_post=row(norm_ffn_post),
        wq=w_in[:, :c_q].astype(BF16), wkv=w_in[:, c_q:c_kv].astype(BF16),
        wrkv=w_in[:, c_kv:c_rkv].astype(BF16), wzl=w_in[:, c_rkv:c_zl].astype(BF16),
        wg=w_in[:, c_zl:].astype(BF16),
        sink=attn_sink.astype(F32), bias=_attn_bias(),
        mu_rkv=mu[:, :3 * RW_W], mu_zl=mu[:, 3 * RW_W:],
        w0=[row(rw_w0[d]) for d in range(2)],
        w2=[lora_pad(rw_w2, d, DECAY_LORA) for d in range(2)],
        a0=[row(rw_a0[d]) for d in range(2)],
        a2=[lora_pad(rw_a2, d, AAA_LORA) for d in range(2)],
        g2=rw_g2.astype(BF16), k_k=row(rw_k_k), k_a=row(rw_k_a), r_k=row(rw_r_k),
        ln_w=row(rw_ln_w), ln_b=row(rw_ln_b),
        wa=w_branch_attn.astype(BF16), wb=w_branch_rwkv.astype(BF16), wo=w_out.astype(BF16),
        wup=w_ffn_up.astype(BF16), cw=ffn_conv_w.astype(F32), cb=row(ffn_conv_b),
        wdn=w_ffn_down.astype(BF16),
    )
    return p


def _layer(x, p):
    b, t, _ = x.shape
    assert t % RW_TILE == 0 and t % ATT_TILE == 0 and t % FFN_TILE == 0
    assert (b * t) % PROJ_TILE == 0 and (b * t) % MERGE_TILE == 0
    x2 = x.reshape(b * t, D_MODEL)
    q, kv, rkv, zl = _in_proj(x2, p["g_mix_pre"], p["wq"], p["wkv"], p["wrkv"], p["wzl"],
                              p["mu_rkv"], p["mu_zl"], b, t)
    o_attn = _attention(q, kv, p["sink"], p["bias"], b, t)

    def consts(d):
        return [p["w0"][d], p["w2"][d], p["a0"][d], p["a2"][d],
                p["k_k"], p["k_a"]]

    y_fwd = _rwkv_pass(False, rkv, zl, consts(0), None, b, t)
    o_rwkv = _rwkv_pass(True, rkv, zl, consts(1),
                        [y_fwd, p["a0"][0], p["a2"][0], p["g2"], p["r_k"], p["ln_w"], p["ln_b"]], b, t)
    h = _merge(x2, o_attn, o_rwkv, p["g_mix_pre"], p["wg"], p["wa"], p["wb"], p["wo"], p["g_mix_post"])
    out = _ffn(h, p["g_ffn_pre"], p["wup"], p["cw"], p["cb"], p["wdn"], p["g_ffn_post"], b, t)
    return out.reshape(b, t, D_MODEL)


def kernel(x_prompt, x_sample, norm_mix_pre, norm_mix_post, norm_ffn_pre, norm_ffn_post, w_in, attn_sink, rw_mu_prev, rw_mu_next, rw_w0, rw_w2, rw_a0, rw_a2, rw_g2, rw_k_k, rw_k_a, rw_r_k, rw_ln_w, rw_ln_b, w_branch_attn, w_branch_rwkv, w_out, w_ffn_up, ffn_conv_w, ffn_conv_b, w_ffn_down):
    weights = (norm_mix_pre, norm_mix_post, norm_ffn_pre, norm_ffn_post, w_in, attn_sink,
               rw_mu_prev, rw_mu_next, rw_w0, rw_w2, rw_a0, rw_a2, rw_g2, rw_k_k, rw_k_a,
               rw_r_k, rw_ln_w, rw_ln_b, w_branch_attn, w_branch_rwkv, w_out,
               w_ffn_up, ffn_conv_w, ffn_conv_b, w_ffn_down)
    depth = w_in.shape[0]
    layers = [_prepare(*(w[l] for w in weights)) for l in range(depth)]

    def run(x):
        for p in layers:
            x = _layer(x, p)
        return x

    return (run(x_prompt), run(x_sample))
```

```python
import functools
import math

import numpy as np
import jax
import jax.numpy as jnp
from jax import lax
from jax.experimental import pallas as pl
from jax.experimental.pallas import tpu as pltpu

F32 = jnp.float32
BF16 = jnp.bfloat16

D_MODEL = 1024
ATT_H = 8
ATT_KV = 2
ATT_G = ATT_H // ATT_KV
ATT_HD = 64
ATT_W = ATT_H * ATT_HD
ATT_KV_W = ATT_KV * ATT_HD
WINDOW = 128
BLK = 128
ATT_SCALE = 1.0 / math.sqrt(ATT_HD)
RW_H = 8
RW_N = 64
RW_W = RW_H * RW_N
DECAY_LORA = 64
AAA_LORA = 64
GATE_LORA = 160
RW_LORA_W = 2 * DECAY_LORA + 2 * AAA_LORA + GATE_LORA
RW_MIX_W = 3 * RW_W + RW_LORA_W
GATE_W = 2 * D_MODEL
D_FF = 2816
NORM_EPS = 1e-6
GN_EPS = 64e-5

VMEM_LIMIT_BYTES = 56 * 2**20

CHUNK = 64
GROUP_H = 4
GROUP_W = GROUP_H * RW_N
N_GROUPS = RW_H // GROUP_H
RW_TILE = 1024
RW_PARTS = 4

PROJ_TILE = 512
PROJ_HALO = 8
ATT_TILE = 512
MERGE_TILE = 512
FFN_TILE = 512
FFN_HALO = 8
FFN_COLS = 256


def _dot(a, b):
    return jnp.dot(a, b, preferred_element_type=F32)


def _dot_nt(a, b):
    return lax.dot_general(a, b, (((1,), (1,)), ((), ())), preferred_element_type=F32)


def _dot_tn(a, b):
    return lax.dot_general(a, b, (((0,), (0,)), ((), ())), preferred_element_type=F32)


def _sigmoid(x):
    return 1.0 / (1.0 + jnp.exp(-x))


def _rms(x, g):
    return x * lax.rsqrt(jnp.mean(x * x, axis=-1, keepdims=True) + NORM_EPS) * g


def _const_spec(shape):
    nd = len(shape)
    return pl.BlockSpec(shape, lambda *_: (0,) * nd, pipeline_mode=pl.Buffered(1))


def _params(n_axes):
    return pltpu.CompilerParams(dimension_semantics=("arbitrary",) * n_axes,
                                vmem_limit_bytes=VMEM_LIMIT_BYTES)


def _in_proj_kernel(xc_ref, xp_ref, xn_ref, g_ref, wq_ref, wkv_ref, wrkv_ref, wzl_ref, mu_rkv_ref, mu_zl_ref,
                    q_ref, kv_ref, rkv_ref, zl_ref):
    j = pl.program_id(1)
    nj = pl.num_programs(1)
    tm = PROJ_TILE
    ext = tm + 2 * PROJ_HALO
    x = jnp.concatenate([xp_ref[...], xc_ref[...], xn_ref[...]], axis=0)
    rid = lax.broadcasted_iota(jnp.int32, (ext, 1), 0)
    keep = jnp.logical_and(jnp.logical_or(rid >= PROJ_HALO, j > 0),
                           jnp.logical_or(rid < PROJ_HALO + tm, j < nj - 1))
    u = jnp.where(keep, _rms(x, g_ref[...]), 0.0).astype(BF16)
    um = u[PROJ_HALO:PROJ_HALO + tm]
    q_ref[...] = (_dot(um, wq_ref[...]) * ATT_SCALE).astype(BF16)
    kv_ref[...] = _dot(um, wkv_ref[...]).astype(BF16)

    def shifted(w_ref, mu_ref, o_ref, cols):
        z = _dot(u, w_ref[:, cols])
        c = z[PROJ_HALO:PROJ_HALO + tm]
        up = pltpu.roll(z, 1, 0)[PROJ_HALO:PROJ_HALO + tm]
        dn = pltpu.roll(z, ext - 1, 0)[PROJ_HALO:PROJ_HALO + tm]
        o_ref[:, cols] = (c + mu_ref[0:1, cols] * (up - c) + mu_ref[1:2, cols] * (dn - c)).astype(BF16)

    for c0 in range(0, 3 * RW_W, RW_W):
        shifted(wrkv_ref, mu_rkv_ref, rkv_ref, slice(c0, c0 + RW_W))
    shifted(wzl_ref, mu_zl_ref, zl_ref, slice(0, RW_LORA_W))


def _in_proj(x, g, wq, wkv, wrkv, wzl, mu_rkv, mu_zl, b, t):
    n = x.shape[0]
    tm = PROJ_TILE
    nj = t // tm
    per = tm // PROJ_HALO
    nhalo = n // PROJ_HALO

    def prev_map(i, j):
        return (jnp.maximum((i * nj + j) * per - 1, 0), 0)

    def next_map(i, j):
        return (jnp.minimum((i * nj + j + 1) * per, nhalo - 1), 0)

    row = lambda w: pl.BlockSpec((tm, w), lambda i, j: (i * nj + j, 0))
    consts = [g, wq, wkv, wrkv, wzl, mu_rkv, mu_zl]
    return pl.pallas_call(
        _in_proj_kernel,
        grid=(b, nj),
        in_specs=[row(D_MODEL), pl.BlockSpec((PROJ_HALO, D_MODEL), prev_map),
                  pl.BlockSpec((PROJ_HALO, D_MODEL), next_map)] + [_const_spec(c.shape) for c in consts],
        out_specs=[row(ATT_W), row(2 * ATT_KV_W), row(3 * RW_W), row(RW_LORA_W)],
        out_shape=[jax.ShapeDtypeStruct((n, ATT_W), BF16),
                   jax.ShapeDtypeStruct((n, 2 * ATT_KV_W), BF16),
                   jax.ShapeDtypeStruct((n, 3 * RW_W), BF16),
                   jax.ShapeDtypeStruct((n, RW_LORA_W), BF16)],
        compiler_params=_params(2),
        name="in_proj",
    )(x, x, x, *consts)


def _attn_bias():
    slopes = np.exp2(-8.0 / ATT_H * np.arange(1, ATT_H + 1, dtype=np.float64))
    dist = np.abs(np.arange(BLK)[:, None] - (np.arange(3 * BLK)[None, :] - BLK))
    bias = -(slopes[:, None, None] * dist[None])
    bias = np.where(dist[None] <= WINDOW, bias, -np.inf)
    bias = bias.reshape(ATT_KV, ATT_G * BLK, 3 * BLK).transpose(0, 2, 1)
    return jnp.asarray(bias, F32)


def _attn_kernel(sink_ref, q_ref, kvp_ref, kvc_ref, kvn_ref, bias_ref, o_ref):
    j = pl.program_id(1)
    nj = pl.num_programs(1)
    has_next = jnp.where(j < nj - 1, 1.0, 0.0).astype(F32)
    kvwin = jnp.concatenate([kvp_ref[...].astype(F32), kvc_ref[...].astype(F32),
                             kvn_ref[...].astype(F32) * has_next], axis=0)
    kwin = kvwin[:, :ATT_KV_W].astype(BF16)
    v_t = kvwin[:, ATT_KV_W:].T.astype(BF16)
    krow = lax.broadcasted_iota(jnp.int32, (3 * BLK, 1), 0)
    n_qb = ATT_TILE // BLK
    gw = ATT_G * ATT_HD
    no_q = jnp.zeros((ATT_HD, ATT_G * BLK), F32)

    pairs = [(qb, kv) for qb in range(n_qb) for kv in range(ATT_KV)]
    every = range(len(pairs))

    def scores(x):
        qb, kv = pairs[x]
        q_t = q_ref[qb * BLK:(qb + 1) * BLK, kv * gw:(kv + 1) * gw].astype(F32).T
        q_t = jnp.concatenate([q_t[g * ATT_HD:(g + 1) * ATT_HD] for g in range(ATT_G)], axis=1)
        q_t = jnp.concatenate([q_t, no_q] if kv == 0 else [no_q, q_t], axis=0).astype(BF16)
        return _dot(kwin[qb * BLK:qb * BLK + 3 * BLK], q_t)

    def softmax(x, s):
        qb, kv = pairs[x]
        key_ok = krow >= jnp.maximum(2 - (j * n_qb + qb), 0) * BLK
        s = jnp.where(key_ok, s + bias_ref[kv], -jnp.inf)
        sink = jnp.concatenate(
            [jnp.full((1, BLK), sink_ref[kv * ATT_G + g], F32) for g in range(ATT_G)], axis=1)
        mx = jnp.maximum(jnp.max(s, axis=0, keepdims=True), sink)
        p = jnp.exp(s - mx)
        den = jnp.sum(p, axis=0, keepdims=True) + jnp.exp(sink - mx)
        return p.astype(BF16), 1.0 / den

    def values(x, p, inv_den):
        qb, kv = pairs[x]
        o_t = _dot(v_t[kv * ATT_HD:(kv + 1) * ATT_HD, qb * BLK:qb * BLK + 3 * BLK], p) * inv_den
        o_t = jnp.concatenate([o_t[:, g * BLK:(g + 1) * BLK] for g in range(ATT_G)], axis=0)
        o_ref[qb * BLK:(qb + 1) * BLK, kv * gw:(kv + 1) * gw] = o_t.T.astype(BF16)

    s_all = [scores(x) for x in every]
    p_all = [softmax(x, s_all[x]) for x in every]
    for x in every:
        values(x, *p_all[x])


def _attention(q, kv, sink, bias, b, t):
    n = q.shape[0]
    tq = ATT_TILE
    nj = t // tq
    per = tq // BLK
    nblk = n // BLK

    def prev_map(i, j):
        return (jnp.maximum((i * nj + j) * per - 1, 0), 0)

    def next_map(i, j):
        return (jnp.minimum((i * nj + j + 1) * per, nblk - 1), 0)

    return pl.pallas_call(
        _attn_kernel,
        grid=(b, nj),
        in_specs=[pl.BlockSpec(memory_space=pltpu.SMEM),
                  pl.BlockSpec((tq, ATT_W), lambda i, j: (i * nj + j, 0)),
                  pl.BlockSpec((BLK, 2 * ATT_KV_W), prev_map),
                  pl.BlockSpec((tq, 2 * ATT_KV_W), lambda i, j: (i * nj + j, 0)),
                  pl.BlockSpec((BLK, 2 * ATT_KV_W), next_map),
                  _const_spec(bias.shape)],
        out_specs=pl.BlockSpec((tq, ATT_W), lambda i, j: (i * nj + j, 0)),
        out_shape=jax.ShapeDtypeStruct((n, ATT_W), BF16),
        compiler_params=_params(2),
        name="attention",
    )(sink, q, kv, kv, kv, bias)


def _block_diag(x, mask):
    tiled = jnp.concatenate([x.astype(F32)] * GROUP_H, axis=0)
    return jnp.where(mask, tiled, 0.0).astype(BF16)


def _head_sum(x):
    lanes = 2 * RW_N
    low = lax.broadcasted_iota(jnp.int32, (x.shape[0], lanes), 1) < RW_N
    out = []
    for c0 in range(0, RW_W, lanes):
        xc = x[:, c0:c0 + lanes]
        lo = jnp.sum(jnp.where(low, xc, 0.0), axis=1, keepdims=True)
        hi = jnp.sum(jnp.where(low, 0.0, xc), axis=1, keepdims=True)
        out.append(jnp.where(low, lo, hi))
    return jnp.concatenate(out, axis=1)


def _rwkv_kernel(reverse, final, *refs):
    if final:
        (rkv_ref, zl_ref, w0, w2, a0, a2, k_k, k_a,
         y0, a0o, a2o, g2, r_k, ln_w, ln_b, out_ref, s_ref, y_s) = refs
    else:
        (rkv_ref, zl_ref, w0, w2, a0, a2, k_k, k_a, out_ref, s_ref, y_s) = refs

    j = pl.program_id(1)

    @pl.when(j == 0)
    def _():
        s_ref[...] = jnp.zeros_like(s_ref)

    rid = lax.broadcasted_iota(jnp.int32, (CHUNK, CHUNK), 0)
    cid = lax.broadcasted_iota(jnp.int32, (CHUNK, CHUNK), 1)
    lmat = jnp.where((cid >= rid) if reverse else (cid <= rid), 1.0, 0.0).astype(BF16)

    def prepare(chunks, res):
        lo = min(chunks) * CHUNK
        rows = slice(lo, (max(chunks) + 1) * CHUNK)
        r = rkv_ref[rows, :RW_W].astype(F32)
        k = rkv_ref[rows, RW_W:2 * RW_W].astype(F32)
        v_bf = rkv_ref[rows, 2 * RW_W:]
        wd = zl_ref[rows, :2 * DECAY_LORA].astype(F32)
        ad = zl_ref[rows, 2 * DECAY_LORA:2 * DECAY_LORA + 2 * AAA_LORA]
        w = w0[...] + _dot(jnp.tanh(wd).astype(BF16), w2[...])
        lw = -math.exp(-0.5) * _sigmoid(w)
        a = _sigmoid(a0[...] + _dot(ad, a2[...]))
        yield
        lw_hi = lw.astype(BF16)
        lw_lo = (lw - lw_hi.astype(F32)).astype(BF16)
        cum = jnp.concatenate(
            [_dot(lmat, lw_hi[c0:c0 + CHUNK]) + _dot(lmat, lw_lo[c0:c0 + CHUNK])
             for c0 in range(0, len(chunks) * CHUNK, CHUNK)], axis=0)
        yield
        kkk = k * k_k[...]
        kk = kkk / jnp.maximum(jnp.sqrt(_head_sum(kkk * kkk)), 1e-12)
        kdir = k * (1.0 + (a - 1.0) * k_a[...])
        bb = kk * a
        yield
        res["at"] = (-kk * jnp.exp(cum - lw)).astype(BF16)
        res["rt"] = (r * jnp.exp(cum)).astype(BF16)
        yield
        e_neg = jnp.exp(-cum)
        res["bt"] = (bb * e_neg).astype(BF16)
        res["kt"] = (kdir * e_neg).astype(BF16)
        res["v"] = v_bf
        yield
        bp, kp = [], []
        for ci in sorted(chunks):
            local = slice(ci * CHUNK - lo, (ci + 1) * CHUNK - lo)
            end = local.start if reverse else local.stop - 1
            cum_end = cum[end:end + 1, :]
            e_end = jnp.exp(cum_end - cum[local])
            bp.append((bb[local] * e_end).astype(BF16))
            kp.append((kdir[local] * e_end).astype(BF16))
            res["pc", ci] = jnp.exp(cum_end)
        res["bp"] = jnp.concatenate(bp, axis=0)
        res["kp"] = jnp.concatenate(kp, axis=0)
        yield
        if final:
            a_other = _sigmoid(a0o[...] + _dot(ad, a2o[...]))
            ksum = k * (2.0 + (a + a_other - 2.0) * k_a[...])
            res["bonus"] = _head_sum(r * ksum * r_k[...]) * v_bf.astype(F32)
            yield
            gd = zl_ref[rows, 2 * DECAY_LORA + 2 * AAA_LORA:].astype(F32)
            res["gate"] = _dot(_sigmoid(gd).astype(BF16), g2[...])
            yield

    trow = lax.broadcasted_iota(jnp.int32, (CHUNK, GROUP_W), 0)
    scol = lax.broadcasted_iota(jnp.int32, (CHUNK, GROUP_W), 1) % CHUNK
    strict = (scol > trow) if reverse else (scol < trow)
    incl = (scol >= trow) if reverse else (scol <= trow)
    eye = jnp.where(scol == trow, 1.0, 0.0).astype(F32)
    bdr = lax.broadcasted_iota(jnp.int32, (GROUP_H * CHUNK, GROUP_W), 0) // CHUNK
    bdc = lax.broadcasted_iota(jnp.int32, (GROUP_H * CHUNK, GROUP_W), 1) // RW_N
    bdmask = bdr == bdc
    bd = functools.partial(_block_diag, mask=bdmask)

    def solve(chunks, tok, res):
        lo = min(chunks) * CHUNK
        pairs = [(ci, g) for ci in chunks for g in range(N_GROUPS)]

        def ld(name):
            return {(ci, g): tok[name][ci * CHUNK - lo:(ci + 1) * CHUNK - lo, g * GROUP_W:(g + 1) * GROUP_W]
                    for ci, g in pairs}

        def stage(fn):
            return {x: fn(x) for x in pairs}

        a_t, r_t, v_c, b_t, k_t = ld("at"), ld("rt"), ld("v"), ld("bt"), ld("kt")
        ar = stage(lambda x: jnp.concatenate([a_t[x], r_t[x]], axis=0))
        g1 = stage(lambda x: _dot_nt(ar[x], bd(b_t[x])))
        yield
        g2_ = stage(lambda x: _dot_nt(ar[x], bd(k_t[x])))
        yield
        ab = stage(lambda x: jnp.where(strict, g1[x][:CHUNK], 0.0))
        rb = stage(lambda x: jnp.where(incl, g1[x][CHUNK:], 0.0))
        ak = stage(lambda x: jnp.where(strict, g2_[x][:CHUNK], 0.0))
        rk = stage(lambda x: jnp.where(incl, g2_[x][CHUNK:], 0.0))

        t_inv = stage(lambda x: eye + ab[x])
        pw = stage(lambda x: _dot(ab[x].astype(BF16), bd(ab[x])))
        yield
        n_lvl = int(math.log2(CHUNK))
        for lvl in range(1, n_lvl):
            if lvl < n_lvl - 1:
                both = stage(lambda x: _dot(jnp.concatenate([pw[x], t_inv[x]], axis=0).astype(BF16),
                                            bd(pw[x])))
                pw = stage(lambda x: both[x][:CHUNK])
                t_inv = stage(lambda x, t=t_inv: t[x] + both[x][CHUNK:])
            else:
                t_inv = stage(lambda x, t=t_inv: t[x] + _dot(t[x].astype(BF16), bd(pw[x])))
            yield

        res["wv"] = stage(lambda x: _dot(ak[x].astype(BF16), bd(v_c[x])))
        b_p, k_p = ld("bp"), ld("kp")
        res["ar"] = ar
        res["t"] = stage(lambda x: t_inv[x].astype(BF16))
        res["rbk"] = stage(lambda x: jnp.concatenate([rb[x], rk[x]], axis=1).astype(BF16))
        res["v"] = v_c
        res["bkp"] = stage(lambda x: jnp.concatenate([b_p[x], k_p[x]], axis=0))
        yield

    def carry(chunks, tok, res, state):
        groups = range(N_GROUPS)
        for ci in chunks:
            a_s = [_dot_nt(res["ar"][ci, g], state[g].astype(BF16)) for g in groups]
            yield
            u = [_dot(res["t"][ci, g], bd(a_s[g][:CHUNK] + res["wv"][ci, g])) for g in groups]
            yield
            for g in groups:
                x = (ci, g)
                lanes = slice(g * GROUP_W, (g + 1) * GROUP_W)
                y_s[ci * CHUNK:(ci + 1) * CHUNK, lanes] = a_s[g][CHUNK:] + _dot(
                    res["rbk"][x], jnp.concatenate([bd(u[g]), bd(res["v"][x])], axis=0))
                upd = _dot_tn(jnp.concatenate([u[g].astype(BF16), res["v"][x]], axis=0), res["bkp"][x])
                state[g] = state[g] * tok["pc", ci][:, lanes] + jnp.where(bdmask, upd, 0.0)
            yield

    def run(gen, *fill):
        for _ in gen:
            for f in fill:
                next(f, None)
        for f in fill:
            for _ in f:
                pass

    n_ch = RW_TILE // CHUNK
    scan = [n_ch - 1 - s if reverse else s for s in range(n_ch)]
    per = n_ch // RW_PARTS
    parts = [scan[p * per:(p + 1) * per] for p in range(RW_PARTS)]
    tok = [{} for _ in parts]
    sol = [{} for _ in parts]
    state = [s_ref[g] for g in range(N_GROUPS)]
    run(prepare(parts[0], tok[0]))
    for p in range(RW_PARTS):
        fill = []
        if p > 0:
            fill.append(carry(parts[p - 1], tok[p - 1], sol[p - 1], state))
        if p + 1 < RW_PARTS:
            fill.append(prepare(parts[p + 1], tok[p + 1]))
        run(solve(parts[p], tok[p], sol[p]), *fill)
    run(carry(parts[-1], tok[-1], sol[-1], state))
    for g in range(N_GROUPS):
        s_ref[g] = state[g]

    y = y_s[...]
    if not final:
        out_ref[...] = y
    else:
        in_rows = sorted(range(RW_PARTS), key=lambda p: min(parts[p]))
        bonus = jnp.concatenate([tok[p]["bonus"] for p in in_rows], axis=0)
        gate = jnp.concatenate([tok[p]["gate"] for p in in_rows], axis=0)
        ytot = y + y0[...]
        inv_n = 1.0 / RW_N
        mu = _head_sum(ytot) * inv_n
        d = ytot - mu
        var = _head_sum(d * d) * inv_n
        yn = d * lax.rsqrt(var + GN_EPS) * ln_w[...] + ln_b[...]
        out_ref[...] = ((yn + bonus) * gate).astype(BF16)


def _rwkv_pass(reverse, rkv, zl, consts, extra, b, t):
    n = rkv.shape[0]
    tb = RW_TILE
    nj = t // tb
    final = reverse

    def cur_map(i, j):
        return (i * nj + (nj - 1 - j if reverse else j), 0)

    in_specs = ([pl.BlockSpec((tb, 3 * RW_W), cur_map), pl.BlockSpec((tb, RW_LORA_W), cur_map)]
                + [_const_spec(c.shape) for c in consts])
    args = [rkv, zl] + list(consts)
    if final:
        y0 = extra[0]
        in_specs += [pl.BlockSpec((tb, RW_W), cur_map)] + [_const_spec(c.shape) for c in extra[1:]]
        args += list(extra)
    out_dtype = BF16 if final else F32
    return pl.pallas_call(
        functools.partial(_rwkv_kernel, reverse, final),
        grid=(b, nj),
        in_specs=in_specs,
        out_specs=pl.BlockSpec((tb, RW_W), cur_map),
        out_shape=jax.ShapeDtypeStruct((n, RW_W), out_dtype),
        scratch_shapes=[pltpu.VMEM((N_GROUPS, GROUP_W, GROUP_W), F32),
                        pltpu.VMEM((tb, RW_W), F32)],
        compiler_params=_params(2),
        name="rwkv_bwd" if reverse else "rwkv_fwd",
    )(*args)


def _merge_kernel(x_ref, oa_ref, ob_ref, gpre_ref, wg_ref, wa_ref, wb_ref, wo_ref, gpost_ref, h_ref):
    x = x_ref[...]
    u = _rms(x, gpre_ref[...]).astype(BF16)
    gates = _sigmoid(_dot(u, wg_ref[...]))
    merged = (gates[:, :D_MODEL] * _dot(oa_ref[...], wa_ref[...])
              + gates[:, D_MODEL:] * _dot(ob_ref[...], wb_ref[...]))
    m = _dot(merged.astype(BF16), wo_ref[...])
    h_ref[...] = x + _rms(m, gpost_ref[...])


def _merge(x, oa, ob, gpre, wg, wa, wb, wo, gpost):
    n = x.shape[0]
    tm = MERGE_TILE
    row = lambda w: pl.BlockSpec((tm, w), lambda i: (i, 0))
    consts = [gpre, wg, wa, wb, wo, gpost]
    return pl.pallas_call(
        _merge_kernel,
        grid=(n // tm,),
        in_specs=[row(D_MODEL), row(ATT_W), row(RW_W)] + [_const_spec(c.shape) for c in consts],
        out_specs=row(D_MODEL),
        out_shape=jax.ShapeDtypeStruct((n, D_MODEL), F32),
        compiler_params=_params(1),
        name="merge",
    )(x, oa, ob, *consts)


def _gelu_tanh(x):
    return 0.5 * x * (1.0 + jnp.tanh(math.sqrt(2.0 / math.pi) * (x + 0.044715 * (x * x * x))))


def _ffn_kernel(hc_ref, hp_ref, hn_ref, gpre_ref, wup_ref, cw_ref, cb_ref, wdn_ref, gpost_ref, o_ref):
    j = pl.program_id(1)
    nj = pl.num_programs(1)
    tm = FFN_TILE
    ext = tm + 2 * FFN_HALO
    hc = hc_ref[...]
    hx = jnp.concatenate([hp_ref[...], hc, hn_ref[...]], axis=0)
    rid = lax.broadcasted_iota(jnp.int32, (ext, 1), 0)
    keep = jnp.logical_and(jnp.logical_or(rid >= FFN_HALO, j > 0),
                           jnp.logical_or(rid < FFN_HALO + tm, j < nj - 1))
    u = jnp.where(keep, _rms(hx, gpre_ref[...]), 0.0).astype(BF16)

    def up(c0):
        return [_dot(u, wup_ref[:, off:off + FFN_COLS]) for off in (c0, D_FF + c0)]

    def conv(hh, off):
        cols = slice(off, off + FFN_COLS)
        prev = pltpu.roll(hh, 1, 0)[FFN_HALO:FFN_HALO + tm]
        nxt = pltpu.roll(hh, ext - 1, 0)[FFN_HALO:FFN_HALO + tm]
        cur = hh[FFN_HALO:FFN_HALO + tm]
        return (prev * cw_ref[0:1, cols] + cur * cw_ref[1:2, cols]
                + nxt * cw_ref[2:3, cols] + cb_ref[:, cols])

    starts = list(range(0, D_FF, FFN_COLS))
    pending = up(starts[0])
    acts = []
    for i, c0 in enumerate(starts):
        hh = pending
        if i + 1 < len(starts):
            pending = up(starts[i + 1])
        acts.append((_gelu_tanh(conv(hh[0], c0)) * conv(hh[1], D_FF + c0)).astype(BF16))
    f = _dot(jnp.concatenate(acts, axis=1), wdn_ref[...])
    o_ref[...] = hc + _rms(f, gpost_ref[...])


def _ffn(h, gpre, wup, cw, cb, wdn, gpost, b, t):
    n = h.shape[0]
    tm = FFN_TILE
    nj = t // tm
    per = tm // FFN_HALO
    nhalo = n // FFN_HALO

    def prev_map(i, j):
        return (jnp.maximum((i * nj + j) * per - 1, 0), 0)

    def next_map(i, j):
        return (jnp.minimum((i * nj + j + 1) * per, nhalo - 1), 0)

    consts = [gpre, wup, cw, cb, wdn, gpost]
    return pl.pallas_call(
        _ffn_kernel,
        grid=(b, nj),
        in_specs=[pl.BlockSpec((tm, D_MODEL), lambda i, j: (i * nj + j, 0)),
                  pl.BlockSpec((FFN_HALO, D_MODEL), prev_map),
                  pl.BlockSpec((FFN_HALO, D_MODEL), next_map)]
                 + [_const_spec(c.shape) for c in consts],
        out_specs=pl.BlockSpec((tm, D_MODEL), lambda i, j: (i * nj + j, 0)),
        out_shape=jax.ShapeDtypeStruct((n, D_MODEL), F32),
        compiler_params=_params(2),
        name="ffn",
    )(h, h, h, *consts)


def _prepare(norm_mix_pre, norm_mix_post, norm_ffn_pre, norm_ffn_post, w_in, attn_sink,
             rw_mu_prev, rw_mu_next, rw_w0, rw_w2, rw_a0, rw_a2, rw_g2, rw_k_k, rw_k_a,
             rw_r_k, rw_ln_w, rw_ln_b, w_branch_attn, w_branch_rwkv, w_out,
             w_ffn_up, ffn_conv_w, ffn_conv_b, w_ffn_down):
    c_q = ATT_W
    c_kv = c_q + 2 * ATT_KV_W
    c_rkv = c_kv + 3 * RW_W
    c_zl = c_rkv + RW_LORA_W
    row = lambda p: p.reshape(1, -1).astype(F32)

    def lora_pad(w2, d, n_lora):
        z = jnp.zeros((2 * n_lora, RW_W), F32)
        return z.at[d * n_lora:(d + 1) * n_lora].set(w2[d]).astype(BF16)

    mu = jnp.stack([rw_mu_prev, rw_mu_next]).astype(F32)
    p = dict(
        g_mix_pre=row(norm_mix_pre), g_mix_post=row(norm_mix_post),
        g_ffn_pre=row(norm_ffn_pre), g_ffn_post=row(norm_ffn_post),
        wq=w_in[:, :c_q].astype(BF16), wkv=w_in[:, c_q:c_kv].astype(BF16),
        wrkv=w_in[:, c_kv:c_rkv].astype(BF16), wzl=w_in[:, c_rkv:c_zl].astype(BF16),
        wg=w_in[:, c_zl:].astype(BF16),
        sink=attn_sink.astype(F32), bias=_attn_bias(),
        mu_rkv=mu[:, :3 * RW_W], mu_zl=mu[:, 3 * RW_W:],
        w0=[row(rw_w0[d]) for d in range(2)],
        w2=[lora_pad(rw_w2, d, DECAY_LORA) for d in range(2)],
        a0=[row(rw_a0[d]) for d in range(2)],
        a2=[lora_pad(rw_a2, d, AAA_LORA) for d in range(2)],
        g2=rw_g2.astype(BF16), k_k=row(rw_k_k), k_a=row(rw_k_a), r_k=row(rw_r_k),
        ln_w=row(rw_ln_w), ln_b=row(rw_ln_b),
        wa=w_branch_attn.astype(BF16), wb=w_branch_rwkv.astype(BF16), wo=w_out.astype(BF16),
        wup=w_ffn_up.astype(BF16), cw=ffn_conv_w.astype(F32), cb=row(ffn_conv_b),
        wdn=w_ffn_down.astype(BF16),
    )
    return p


def _layer(x, p):
    b, t, _ = x.shape
    assert t % RW_TILE == 0 and t % ATT_TILE == 0 and t % FFN_TILE == 0
    assert (b * t) % PROJ_TILE == 0 and (b * t) % MERGE_TILE == 0
    x2 = x.reshape(b * t, D_MODEL)
    q, kv, rkv, zl = _in_proj(x2, p["g_mix_pre"], p["wq"], p["wkv"], p["wrkv"], p["wzl"],
                              p["mu_rkv"], p["mu_zl"], b, t)
    o_attn = _attention(q, kv, p["sink"], p["bias"], b, t)

    def consts(d):
        return [p["w0"][d], p["w2"][d], p["a0"][d], p["a2"][d],
                p["k_k"], p["k_a"]]

    y_fwd = _rwkv_pass(False, rkv, zl, consts(0), None, b, t)
    o_rwkv = _rwkv_pass(True, rkv, zl, consts(1),
                        [y_fwd, p["a0"][0], p["a2"][0], p["g2"], p["r_k"], p["ln_w"], p["ln_b"]], b, t)
    h = _merge(x2, o_attn, o_rwkv, p["g_mix_pre"], p["wg"], p["wa"], p["wb"], p["wo"], p["g_mix_post"])
    out = _ffn(h, p["g_ffn_pre"], p["wup"], p["cw"], p["cb"], p["wdn"], p["g_ffn_post"], b, t)
    return out.reshape(b, t, D_MODEL)


def kernel(x_prompt, x_sample, norm_mix_pre, norm_mix_post, norm_ffn_pre, norm_ffn_post, w_in, attn_sink, rw_mu_prev, rw_mu_next, rw_w0, rw_w2, rw_a0, rw_a2, rw_g2, rw_k_k, rw_k_a, rw_r_k, rw_ln_w, rw_ln_b, w_branch_attn, w_branch_rwkv, w_out, w_ffn_up, ffn_conv_w, ffn_conv_b, w_ffn_down):
    weights = (norm_mix_pre, norm_mix_post, norm_ffn_pre, norm_ffn_post, w_in, attn_sink,
               rw_mu_prev, rw_mu_next, rw_w0, rw_w2, rw_a0, rw_a2, rw_g2, rw_k_k, rw_k_a,
               rw_r_k, rw_ln_w, rw_ln_b, w_branch_attn, w_branch_rwkv, w_out,
               w_ffn_up, ffn_conv_w, ffn_conv_b, w_ffn_down)
    depth = w_in.shape[0]
    layers = [_prepare(*(w[l] for w in weights)) for l in range(depth)]

    def run(x):
        for p in layers:
            x = _layer(x, p)
        return x

    return (run(x_prompt), run(x_sample))
```

```python
import functools
import math

import numpy as np
import jax
import jax.numpy as jnp
from jax import lax
from jax.experimental import pallas as pl
from jax.experimental.pallas import tpu as pltpu

F32 = jnp.float32
BF16 = jnp.bfloat16

D_MODEL = 1024
ATT_H = 8
ATT_KV = 2
ATT_G = ATT_H // ATT_KV
ATT_HD = 64
ATT_W = ATT_H * ATT_HD
ATT_KV_W = ATT_KV * ATT_HD
WINDOW = 128
BLK = 128
ATT_SCALE = 1.0 / math.sqrt(ATT_HD)
RW_H = 8
RW_N = 64
RW_W = RW_H * RW_N
DECAY_LORA = 64
AAA_LORA = 64
GATE_LORA = 160
RW_LORA_W = 2 * DECAY_LORA + 2 * AAA_LORA + GATE_LORA
RW_MIX_W = 3 * RW_W + RW_LORA_W
GATE_W = 2 * D_MODEL
D_FF = 2816
NORM_EPS = 1e-6
GN_EPS = 64e-5

VMEM_LIMIT_BYTES = 56 * 2**20

CHUNK = 64
GROUP_H = 4
GROUP_W = GROUP_H * RW_N
N_GROUPS = RW_H // GROUP_H
RW_TILE = 1024
RW_PARTS = 4

PROJ_TILE = 1024
PROJ_HALO = 8
ATT_TILE = 512
MERGE_TILE = 1024
FFN_TILE = 512
FFN_HALO = 8
FFN_SPAN = FFN_HALO * FFN_HALO
FFN_COLS = 256


def _dot(a, b):
    return jnp.dot(a, b, preferred_element_type=F32)


def _dot_nt(a, b):
    return lax.dot_general(a, b, (((1,), (1,)), ((), ())), preferred_element_type=F32)


def _dot_tn(a, b):
    return lax.dot_general(a, b, (((0,), (0,)), ((), ())), preferred_element_type=F32)


def _sigmoid(x):
    return 1.0 / (1.0 + jnp.exp(-x))


def _rms(x, g):
    return x * lax.rsqrt(jnp.mean(x * x, axis=-1, keepdims=True) + NORM_EPS) * g


def _const_spec(shape):
    nd = len(shape)
    return pl.BlockSpec(shape, lambda *_: (0,) * nd, pipeline_mode=pl.Buffered(1))


def _params(n_axes):
    return pltpu.CompilerParams(dimension_semantics=("arbitrary",) * n_axes,
                                vmem_limit_bytes=VMEM_LIMIT_BYTES)


def _in_proj_kernel(xc_ref, xp_ref, xn_ref, g_ref, wq_ref, wkv_ref, wrkv_ref, wzl_ref, mu_rkv_ref, mu_zl_ref,
                    q_ref, kv_ref, rkv_ref, zl_ref):
    j = pl.program_id(1)
    nj = pl.num_programs(1)
    tm = PROJ_TILE
    ext = tm + 2 * PROJ_HALO
    x = jnp.concatenate([xp_ref[...], xc_ref[...], xn_ref[...]], axis=0)
    rid = lax.broadcasted_iota(jnp.int32, (ext, 1), 0)
    keep = jnp.logical_and(jnp.logical_or(rid >= PROJ_HALO, j > 0),
                           jnp.logical_or(rid < PROJ_HALO + tm, j < nj - 1))
    u = jnp.where(keep, _rms(x, g_ref[...]), 0.0).astype(BF16)
    um = u[PROJ_HALO:PROJ_HALO + tm]
    q_ref[...] = (_dot(um, wq_ref[...]) * ATT_SCALE).astype(BF16)
    kv_ref[...] = _dot(um, wkv_ref[...]).astype(BF16)

    def shifted(w_ref, mu_ref, o_ref, cols):
        z = _dot(u, w_ref[:, cols])
        c = z[PROJ_HALO:PROJ_HALO + tm]
        up = pltpu.roll(z, 1, 0)[PROJ_HALO:PROJ_HALO + tm]
        dn = pltpu.roll(z, ext - 1, 0)[PROJ_HALO:PROJ_HALO + tm]
        o_ref[:, cols] = (c + mu_ref[0:1, cols] * (up - c) + mu_ref[1:2, cols] * (dn - c)).astype(BF16)

    for c0 in range(0, 3 * RW_W, RW_W):
        shifted(wrkv_ref, mu_rkv_ref, rkv_ref, slice(c0, c0 + RW_W))
    shifted(wzl_ref, mu_zl_ref, zl_ref, slice(0, RW_LORA_W))


def _in_proj(x, g, wq, wkv, wrkv, wzl, mu_rkv, mu_zl, b, t):
    n = x.shape[0]
    tm = PROJ_TILE
    nj = t // tm
    per = tm // PROJ_HALO
    nhalo = n // PROJ_HALO

    def prev_map(i, j):
        return (jnp.maximum((i * nj + j) * per - 1, 0), 0)

    def next_map(i, j):
        return (jnp.minimum((i * nj + j + 1) * per, nhalo - 1), 0)

    row = lambda w: pl.BlockSpec((tm, w), lambda i, j: (i * nj + j, 0))
    consts = [g, wq, wkv, wrkv, wzl, mu_rkv, mu_zl]
    return pl.pallas_call(
        _in_proj_kernel,
        grid=(b, nj),
        in_specs=[row(D_MODEL), pl.BlockSpec((PROJ_HALO, D_MODEL), prev_map),
                  pl.BlockSpec((PROJ_HALO, D_MODEL), next_map)] + [_const_spec(c.shape) for c in consts],
        out_specs=[row(ATT_W), row(2 * ATT_KV_W), row(3 * RW_W), row(RW_LORA_W)],
        out_shape=[jax.ShapeDtypeStruct((n, ATT_W), BF16),
                   jax.ShapeDtypeStruct((n, 2 * ATT_KV_W), BF16),
                   jax.ShapeDtypeStruct((n, 3 * RW_W), BF16),
                   jax.ShapeDtypeStruct((n, RW_LORA_W), BF16)],
        compiler_params=_params(2),
        name="in_proj",
    )(x, x, x, *consts)


def _attn_bias():
    slopes = np.exp2(-8.0 / ATT_H * np.arange(1, ATT_H + 1, dtype=np.float64))
    dist = np.abs(np.arange(BLK)[:, None] - (np.arange(3 * BLK)[None, :] - BLK))
    bias = -(slopes[:, None, None] * dist[None])
    bias = np.where(dist[None] <= WINDOW, bias, -np.inf)
    bias = bias.reshape(ATT_KV, ATT_G * BLK, 3 * BLK).transpose(0, 2, 1)
    return jnp.asarray(bias, F32)


def _attn_kernel(sink_ref, q_ref, kvp_ref, kvc_ref, kvn_ref, bias_ref, o_ref):
    j = pl.program_id(1)
    nj = pl.num_programs(1)
    has_next = jnp.where(j < nj - 1, 1.0, 0.0).astype(F32)
    kvwin = jnp.concatenate([kvp_ref[...].astype(F32), kvc_ref[...].astype(F32),
                             kvn_ref[...].astype(F32) * has_next], axis=0)
    kwin = kvwin[:, :ATT_KV_W].astype(BF16)
    v_t = kvwin[:, ATT_KV_W:].T.astype(BF16)
    krow = lax.broadcasted_iota(jnp.int32, (3 * BLK, 1), 0)
    n_qb = ATT_TILE // BLK
    gw = ATT_G * ATT_HD
    no_q = jnp.zeros((ATT_HD, ATT_G * BLK), F32)

    pairs = [(qb, kv) for qb in range(n_qb) for kv in range(ATT_KV)]
    every = range(len(pairs))

    def scores(x):
        qb, kv = pairs[x]
        q_t = q_ref[qb * BLK:(qb + 1) * BLK, kv * gw:(kv + 1) * gw].astype(F32).T
        q_t = jnp.concatenate([q_t[g * ATT_HD:(g + 1) * ATT_HD] for g in range(ATT_G)], axis=1)
        q_t = jnp.concatenate([q_t, no_q] if kv == 0 else [no_q, q_t], axis=0).astype(BF16)
        return _dot(kwin[qb * BLK:qb * BLK + 3 * BLK], q_t)

    def softmax(x, s):
        qb, kv = pairs[x]
        key_ok = krow >= jnp.maximum(2 - (j * n_qb + qb), 0) * BLK
        s = jnp.where(key_ok, s + bias_ref[kv], -jnp.inf)
        sink = jnp.concatenate(
            [jnp.full((1, BLK), sink_ref[kv * ATT_G + g], F32) for g in range(ATT_G)], axis=1)
        mx = jnp.maximum(jnp.max(s, axis=0, keepdims=True), sink)
        p = jnp.exp(s - mx)
        den = jnp.sum(p, axis=0, keepdims=True) + jnp.exp(sink - mx)
        return p.astype(BF16), 1.0 / den

    def values(x, p, inv_den):
        qb, kv = pairs[x]
        o_t = _dot(v_t[kv * ATT_HD:(kv + 1) * ATT_HD, qb * BLK:qb * BLK + 3 * BLK], p) * inv_den
        o_t = jnp.concatenate([o_t[:, g * BLK:(g + 1) * BLK] for g in range(ATT_G)], axis=0)
        o_ref[qb * BLK:(qb + 1) * BLK, kv * gw:(kv + 1) * gw] = o_t.T.astype(BF16)

    s_all = [scores(x) for x in every]
    p_all = [softmax(x, s_all[x]) for x in every]
    for x in every:
        values(x, *p_all[x])


def _attention(q, kv, sink, bias, b, t):
    n = q.shape[0]
    tq = ATT_TILE
    nj = t // tq
    per = tq // BLK
    nblk = n // BLK

    def prev_map(i, j):
        return (jnp.maximum((i * nj + j) * per - 1, 0), 0)

    def next_map(i, j):
        return (jnp.minimum((i * nj + j + 1) * per, nblk - 1), 0)

    return pl.pallas_call(
        _attn_kernel,
        grid=(b, nj),
        in_specs=[pl.BlockSpec(memory_space=pltpu.SMEM),
                  pl.BlockSpec((tq, ATT_W), lambda i, j: (i * nj + j, 0)),
                  pl.BlockSpec((BLK, 2 * ATT_KV_W), prev_map),
                  pl.BlockSpec((tq, 2 * ATT_KV_W), lambda i, j: (i * nj + j, 0)),
                  pl.BlockSpec((BLK, 2 * ATT_KV_W), next_map),
                  _const_spec(bias.shape)],
        out_specs=pl.BlockSpec((tq, ATT_W), lambda i, j: (i * nj + j, 0)),
        out_shape=jax.ShapeDtypeStruct((n, ATT_W), BF16),
        compiler_params=_params(2),
        name="attention",
    )(sink, q, kv, kv, kv, bias)


def _block_diag(x, mask):
    tiled = jnp.concatenate([x.astype(F32)] * GROUP_H, axis=0)
    return jnp.where(mask, tiled, 0.0).astype(BF16)


def _head_sum(x):
    lanes = 2 * RW_N
    low = lax.broadcasted_iota(jnp.int32, (x.shape[0], lanes), 1) < RW_N
    out = []
    for c0 in range(0, RW_W, lanes):
        xc = x[:, c0:c0 + lanes]
        lo = jnp.sum(jnp.where(low, xc, 0.0), axis=1, keepdims=True)
        hi = jnp.sum(jnp.where(low, 0.0, xc), axis=1, keepdims=True)
        out.append(jnp.where(low, lo, hi))
    return jnp.concatenate(out, axis=1)


def _rwkv_kernel(reverse, final, *refs):
    if final:
        (rkv_ref, zl_ref, w0, w2, a0, a2, k_k, k_a,
         y0, a0o, a2o, g2, r_k, ln_w, ln_b, out_ref, s_ref, y_s) = refs
    else:
        (rkv_ref, zl_ref, w0, w2, a0, a2, k_k, k_a, out_ref, s_ref, y_s) = refs

    j = pl.program_id(1)

    @pl.when(j == 0)
    def _():
        s_ref[...] = jnp.zeros_like(s_ref)

    rid = lax.broadcasted_iota(jnp.int32, (CHUNK, CHUNK), 0)
    cid = lax.broadcasted_iota(jnp.int32, (CHUNK, CHUNK), 1)
    lmat = jnp.where((cid >= rid) if reverse else (cid <= rid), 1.0, 0.0).astype(BF16)

    def prepare(chunks, res):
        lo = min(chunks) * CHUNK
        rows = slice(lo, (max(chunks) + 1) * CHUNK)
        r = rkv_ref[rows, :RW_W].astype(F32)
        k = rkv_ref[rows, RW_W:2 * RW_W].astype(F32)
        v_bf = rkv_ref[rows, 2 * RW_W:]
        wd = zl_ref[rows, :2 * DECAY_LORA].astype(F32)
        ad = zl_ref[rows, 2 * DECAY_LORA:2 * DECAY_LORA + 2 * AAA_LORA]
        w = w0[...] + _dot(jnp.tanh(wd).astype(BF16), w2[...])
        lw = -math.exp(-0.5) * _sigmoid(w)
        a = _sigmoid(a0[...] + _dot(ad, a2[...]))
        yield
        lw_hi = lw.astype(BF16)
        lw_lo = (lw - lw_hi.astype(F32)).astype(BF16)
        cum = jnp.concatenate(
            [_dot(lmat, lw_hi[c0:c0 + CHUNK]) + _dot(lmat, lw_lo[c0:c0 + CHUNK])
             for c0 in range(0, len(chunks) * CHUNK, CHUNK)], axis=0)
        yield
        kkk = k * k_k[...]
        kk = kkk / jnp.maximum(jnp.sqrt(_head_sum(kkk * kkk)), 1e-12)
        kdir = k * (1.0 + (a - 1.0) * k_a[...])
        bb = kk * a
        yield
        res["at"] = (-kk * jnp.exp(cum - lw)).astype(BF16)
        res["rt"] = (r * jnp.exp(cum)).astype(BF16)
        yield
        e_neg = jnp.exp(-cum)
        res["bt"] = (bb * e_neg).astype(BF16)
        res["kt"] = (kdir * e_neg).astype(BF16)
        res["v"] = v_bf
        yield
        bp, kp = [], []
        for ci in sorted(chunks):
            local = slice(ci * CHUNK - lo, (ci + 1) * CHUNK - lo)
            end = local.start if reverse else local.stop - 1
            cum_end = cum[end:end + 1, :]
            e_end = jnp.exp(cum_end - cum[local])
            bp.append((bb[local] * e_end).astype(BF16))
            kp.append((kdir[local] * e_end).astype(BF16))
            res["pc", ci] = jnp.exp(cum_end)
        res["bp"] = jnp.concatenate(bp, axis=0)
        res["kp"] = jnp.concatenate(kp, axis=0)
        yield
        if final:
            a_other = _sigmoid(a0o[...] + _dot(ad, a2o[...]))
            ksum = k * (2.0 + (a + a_other - 2.0) * k_a[...])
            res["bonus"] = _head_sum(r * ksum * r_k[...]) * v_bf.astype(F32)
            yield
            gd = zl_ref[rows, 2 * DECAY_LORA + 2 * AAA_LORA:].astype(F32)
            res["gate"] = _dot(_sigmoid(gd).astype(BF16), g2[...])
            yield

    trow = lax.broadcasted_iota(jnp.int32, (CHUNK, GROUP_W), 0)
    scol = lax.broadcasted_iota(jnp.int32, (CHUNK, GROUP_W), 1) % CHUNK
    strict = (scol > trow) if reverse else (scol < trow)
    incl = (scol >= trow) if reverse else (scol <= trow)
    eye = jnp.where(scol == trow, 1.0, 0.0).astype(F32)
    bdr = lax.broadcasted_iota(jnp.int32, (GROUP_H * CHUNK, GROUP_W), 0) // CHUNK
    bdc = lax.broadcasted_iota(jnp.int32, (GROUP_H * CHUNK, GROUP_W), 1) // RW_N
    bdmask = bdr == bdc
    bd = functools.partial(_block_diag, mask=bdmask)

    def solve(chunks, tok, res):
        lo = min(chunks) * CHUNK
        pairs = [(ci, g) for ci in chunks for g in range(N_GROUPS)]

        def ld(name):
            return {(ci, g): tok[name][ci * CHUNK - lo:(ci + 1) * CHUNK - lo, g * GROUP_W:(g + 1) * GROUP_W]
                    for ci, g in pairs}

        def stage(fn):
            return {x: fn(x) for x in pairs}

        a_t, r_t, v_c, b_t, k_t = ld("at"), ld("rt"), ld("v"), ld("bt"), ld("kt")
        ar = stage(lambda x: jnp.concatenate([a_t[x], r_t[x]], axis=0))
        g1 = stage(lambda x: _dot_nt(ar[x], bd(b_t[x])))
        yield
        g2_ = stage(lambda x: _dot_nt(ar[x], bd(k_t[x])))
        yield
        ab = stage(lambda x: jnp.where(strict, g1[x][:CHUNK], 0.0))
        rb = stage(lambda x: jnp.where(incl, g1[x][CHUNK:], 0.0))
        ak = stage(lambda x: jnp.where(strict, g2_[x][:CHUNK], 0.0))
        rk = stage(lambda x: jnp.where(incl, g2_[x][CHUNK:], 0.0))

        t_inv = stage(lambda x: eye + ab[x])
        pw = stage(lambda x: _dot(ab[x].astype(BF16), bd(ab[x])))
        yield
        n_lvl = int(math.log2(CHUNK))
        for lvl in range(1, n_lvl):
            if lvl < n_lvl - 1:
                both = stage(lambda x: _dot(jnp.concatenate([pw[x], t_inv[x]], axis=0).astype(BF16),
                                            bd(pw[x])))
                pw = stage(lambda x: both[x][:CHUNK])
                t_inv = stage(lambda x, t=t_inv: t[x] + both[x][CHUNK:])
            else:
                t_inv = stage(lambda x, t=t_inv: t[x] + _dot(t[x].astype(BF16), bd(pw[x])))
            yield

        res["wv"] = stage(lambda x: _dot(ak[x].astype(BF16), bd(v_c[x])))
        b_p, k_p = ld("bp"), ld("kp")
        res["ar"] = ar
        res["t"] = stage(lambda x: t_inv[x].astype(BF16))
        res["rbk"] = stage(lambda x: jnp.concatenate([rb[x], rk[x]], axis=1).astype(BF16))
        res["v"] = v_c
        res["bkp"] = stage(lambda x: jnp.concatenate([b_p[x], k_p[x]], axis=0))
        yield

    def carry(chunks, tok, res, state):
        groups = range(N_GROUPS)
        for ci in chunks:
            a_s = [_dot_nt(res["ar"][ci, g], state[g].astype(BF16)) for g in groups]
            yield
            u = [_dot(res["t"][ci, g], bd(a_s[g][:CHUNK] + res["wv"][ci, g])) for g in groups]
            yield
            for g in groups:
                x = (ci, g)
                lanes = slice(g * GROUP_W, (g + 1) * GROUP_W)
                y_s[ci * CHUNK:(ci + 1) * CHUNK, lanes] = a_s[g][CHUNK:] + _dot(
                    res["rbk"][x], jnp.concatenate([bd(u[g]), bd(res["v"][x])], axis=0))
                upd = _dot_tn(jnp.concatenate([u[g].astype(BF16), res["v"][x]], axis=0), res["bkp"][x])
                state[g] = state[g] * tok["pc", ci][:, lanes] + jnp.where(bdmask, upd, 0.0)
            yield

    def run(gen, *fill):
        for _ in gen:
            for f in fill:
                next(f, None)
        for f in fill:
            for _ in f:
                pass

    n_ch = RW_TILE // CHUNK
    scan = [n_ch - 1 - s if reverse else s for s in range(n_ch)]
    per = n_ch // RW_PARTS
    parts = [scan[p * per:(p + 1) * per] for p in range(RW_PARTS)]
    tok = [{} for _ in parts]
    sol = [{} for _ in parts]
    state = [s_ref[g] for g in range(N_GROUPS)]
    run(prepare(parts[0], tok[0]))
    for p in range(RW_PARTS):
        fill = []
        if p > 0:
            fill.append(carry(parts[p - 1], tok[p - 1], sol[p - 1], state))
        if p + 1 < RW_PARTS:
            fill.append(prepare(parts[p + 1], tok[p + 1]))
        run(solve(parts[p], tok[p], sol[p]), *fill)
    run(carry(parts[-1], tok[-1], sol[-1], state))
    for g in range(N_GROUPS):
        s_ref[g] = state[g]

    y = y_s[...]
    if not final:
        out_ref[...] = y
    else:
        in_rows = sorted(range(RW_PARTS), key=lambda p: min(parts[p]))
        bonus = jnp.concatenate([tok[p]["bonus"] for p in in_rows], axis=0)
        gate = jnp.concatenate([tok[p]["gate"] for p in in_rows], axis=0)
        ytot = y + y0[...]
        inv_n = 1.0 / RW_N
        mu = _head_sum(ytot) * inv_n
        d = ytot - mu
        var = _head_sum(d * d) * inv_n
        yn = d * lax.rsqrt(var + GN_EPS) * ln_w[...] + ln_b[...]
        out_ref[...] = ((yn + bonus) * gate).astype(BF16)


def _rwkv_pass(reverse, rkv, zl, consts, extra, b, t):
    n = rkv.shape[0]
    tb = RW_TILE
    nj = t // tb
    final = reverse

    def cur_map(i, j):
        return (i * nj + (nj - 1 - j if reverse else j), 0)

    in_specs = ([pl.BlockSpec((tb, 3 * RW_W), cur_map), pl.BlockSpec((tb, RW_LORA_W), cur_map)]
                + [_const_spec(c.shape) for c in consts])
    args = [rkv, zl] + list(consts)
    if final:
        y0 = extra[0]
        in_specs += [pl.BlockSpec((tb, RW_W), cur_map)] + [_const_spec(c.shape) for c in extra[1:]]
        args += list(extra)
    out_dtype = BF16 if final else F32
    return pl.pallas_call(
        functools.partial(_rwkv_kernel, reverse, final),
        grid=(b, nj),
        in_specs=in_specs,
        out_specs=pl.BlockSpec((tb, RW_W), cur_map),
        out_shape=jax.ShapeDtypeStruct((n, RW_W), out_dtype),
        scratch_shapes=[pltpu.VMEM((N_GROUPS, GROUP_W, GROUP_W), F32),
                        pltpu.VMEM((tb, RW_W), F32)],
        compiler_params=_params(2),
        name="rwkv_bwd" if reverse else "rwkv_fwd",
    )(*args)


def _merge_kernel(x_ref, oa_ref, ob_ref, gpre_ref, wg_ref, wa_ref, wb_ref, wo_ref, gpost_ref, h_ref):
    x = x_ref[...]
    u = _rms(x, gpre_ref[...]).astype(BF16)
    gates = _sigmoid(_dot(u, wg_ref[...]))
    merged = (gates[:, :D_MODEL] * _dot(oa_ref[...], wa_ref[...])
              + gates[:, D_MODEL:] * _dot(ob_ref[...], wb_ref[...]))
    m = _dot(merged.astype(BF16), wo_ref[...])
    h_ref[...] = x + _rms(m, gpost_ref[...])


def _merge(x, oa, ob, gpre, wg, wa, wb, wo, gpost):
    n = x.shape[0]
    tm = MERGE_TILE
    row = lambda w: pl.BlockSpec((tm, w), lambda i: (i, 0))
    consts = [gpre, wg, wa, wb, wo, gpost]
    return pl.pallas_call(
        _merge_kernel,
        grid=(n // tm,),
        in_specs=[row(D_MODEL), row(ATT_W), row(RW_W)] + [_const_spec(c.shape) for c in consts],
        out_specs=row(D_MODEL),
        out_shape=jax.ShapeDtypeStruct((n, D_MODEL), F32),
        compiler_params=_params(1),
        name="merge",
    )(x, oa, ob, *consts)


def _gelu_tanh(x):
    return 0.5 * x * (1.0 + jnp.tanh(math.sqrt(2.0 / math.pi) * (x + 0.044715 * (x * x * x))))


def _ffn_kernel(hc_ref, hp_ref, hn_ref, gpre_ref, wup_ref, cw_ref, cb_ref, wdn_ref, gpost_ref, o_ref,
                slab_s):
    j = pl.program_id(1)
    nj = pl.num_programs(1)
    tm = FFN_TILE
    sub = FFN_HALO
    nb = tm // FFN_SPAN
    lane_w = 128
    n_slab = D_MODEL // lane_w

    for c in range(n_slab):
        slab_s[c] = hc_ref[:, c * lane_w:(c + 1) * lane_w]
    hc = jnp.concatenate(
        [jnp.concatenate([slab_s[c, pl.ds(blk * FFN_SPAN + v, sub, stride=sub), :] for c in range(n_slab)],
                         axis=1)
         for blk in range(nb) for v in range(sub)], axis=0)

    sid = lax.broadcasted_iota(jnp.int32, (sub, 1), 0)
    is_before = jnp.logical_and(sid == 0, j > 0)
    is_after = jnp.logical_and(sid == sub - 1, j < nj - 1)
    edge = jnp.where(is_before, hp_ref[sub - 1:sub, :], 0.0) + jnp.where(is_after, hn_ref[0:1, :], 0.0)
    u_edge = jnp.where(jnp.logical_or(is_before, is_after), _rms(edge, gpre_ref[...]), 0.0)
    u = jnp.concatenate([_rms(hc, gpre_ref[...]), u_edge], axis=0).astype(BF16)

    def up(c0):
        return [_dot(u, wup_ref[:, off:off + FFN_COLS]) for off in (c0, D_FF + c0)]

    def conv(hh, off):
        cols = slice(off, off + FFN_COLS)
        grp = lambda blk, v: hh[(blk * sub + v) * sub:(blk * sub + v + 1) * sub]
        edge_h = hh[tm:]
        down = [pltpu.roll(grp(blk, sub - 1), 1, 0) for blk in range(nb)]
        upw = [pltpu.roll(grp(blk, 0), sub - 1, 0) for blk in range(nb)]
        prev, nxt = [], []
        for blk in range(nb):
            first = jnp.where(sid == 0, down[blk - 1] if blk > 0 else edge_h, down[blk])
            last = jnp.where(sid == sub - 1, upw[blk + 1] if blk + 1 < nb else edge_h, upw[blk])
            lo = blk * FFN_SPAN
            prev += [first, hh[lo:lo + FFN_SPAN - sub]]
            nxt += [hh[lo + sub:lo + FFN_SPAN], last]
        prev = jnp.concatenate(prev, axis=0)
        nxt = jnp.concatenate(nxt, axis=0)
        return (prev * cw_ref[0:1, cols] + hh[:tm] * cw_ref[1:2, cols]
                + nxt * cw_ref[2:3, cols] + cb_ref[:, cols])

    starts = list(range(0, D_FF, FFN_COLS))
    pending = up(starts[0])
    acts = []
    for i, c0 in enumerate(starts):
        hh = pending
        if i + 1 < len(starts):
            pending = up(starts[i + 1])
        acts.append((_gelu_tanh(conv(hh[0], c0)) * conv(hh[1], D_FF + c0)).astype(BF16))
    f = _dot(jnp.concatenate(acts, axis=1), wdn_ref[...])
    out = hc + _rms(f, gpost_ref[...])
    for r in range(tm // sub):
        for c in range(n_slab):
            slab_s[c, pl.ds((r // sub) * FFN_SPAN + r % sub, sub, stride=sub), :] = (
                out[r * sub:(r + 1) * sub, c * lane_w:(c + 1) * lane_w])
    for c in range(n_slab):
        o_ref[:, c * lane_w:(c + 1) * lane_w] = slab_s[c]


def _ffn(h, gpre, wup, cw, cb, wdn, gpost, b, t):
    n = h.shape[0]
    tm = FFN_TILE
    nj = t // tm
    per = tm // FFN_HALO
    nhalo = n // FFN_HALO

    def prev_map(i, j):
        return (jnp.maximum((i * nj + j) * per - 1, 0), 0)

    def next_map(i, j):
        return (jnp.minimum((i * nj + j + 1) * per, nhalo - 1), 0)

    consts = [gpre, wup, cw, cb, wdn, gpost]
    return pl.pallas_call(
        _ffn_kernel,
        grid=(b, nj),
        in_specs=[pl.BlockSpec((tm, D_MODEL), lambda i, j: (i * nj + j, 0)),
                  pl.BlockSpec((FFN_HALO, D_MODEL), prev_map),
                  pl.BlockSpec((FFN_HALO, D_MODEL), next_map)]
                 + [_const_spec(c.shape) for c in consts],
        out_specs=pl.BlockSpec((tm, D_MODEL), lambda i, j: (i * nj + j, 0)),
        out_shape=jax.ShapeDtypeStruct((n, D_MODEL), F32),
        scratch_shapes=[pltpu.VMEM((D_MODEL // 128, tm, 128), F32)],
        compiler_params=_params(2),
        name="ffn",
    )(h, h, h, *consts)


def _prepare(norm_mix_pre, norm_mix_post, norm_ffn_pre, norm_ffn_post, w_in, attn_sink,
             rw_mu_prev, rw_mu_next, rw_w0, rw_w2, rw_a0, rw_a2, rw_g2, rw_k_k, rw_k_a,
             rw_r_k, rw_ln_w, rw_ln_b, w_branch_attn, w_branch_rwkv, w_out,
             w_ffn_up, ffn_conv_w, ffn_conv_b, w_ffn_down):
    c_q = ATT_W
    c_kv = c_q + 2 * ATT_KV_W
    c_rkv = c_kv + 3 * RW_W
    c_zl = c_rkv + RW_LORA_W
    row = lambda p: p.reshape(1, -1).astype(F32)

    def lora_pad(w2, d, n_lora):
        z = jnp.zeros((2 * n_lora, RW_W), F32)
        return z.at[d * n_lora:(d + 1) * n_lora].set(w2[d]).astype(BF16)

    mu = jnp.stack([rw_mu_prev, rw_mu_next]).astype(F32)
    p = dict(
        g_mix_pre=row(norm_mix_pre), g_mix_post=row(norm_mix_post),
        g_ffn_pre=row(norm_ffn_pre), g_ffn_post=row(norm_ffn_post),
        wq=w_in[:, :c_q].astype(BF16), wkv=w_in[:, c_q:c_kv].astype(BF16),
        wrkv=w_in[:, c_kv:c_rkv].astype(BF16), wzl=w_in[:, c_rkv:c_zl].astype(BF16),
        wg=w_in[:, c_zl:].astype(BF16),
        sink=attn_sink.astype(F32), bias=_attn_bias(),
        mu_rkv=mu[:, :3 * RW_W], mu_zl=mu[:, 3 * RW_W:],
        w0=[row(rw_w0[d]) for d in range(2)],
        w2=[lora_pad(rw_w2, d, DECAY_LORA) for d in range(2)],
        a0=[row(rw_a0[d]) for d in range(2)],
        a2=[lora_pad(rw_a2, d, AAA_LORA) for d in range(2)],
        g2=rw_g2.astype(BF16), k_k=row(rw_k_k), k_a=row(rw_k_a), r_k=row(rw_r_k),
        ln_w=row(rw_ln_w), ln_b=row(rw_ln_b),
        wa=w_branch_attn.astype(BF16), wb=w_branch_rwkv.astype(BF16), wo=w_out.astype(BF16),
        wup=w_ffn_up.astype(BF16), cw=ffn_conv_w.astype(F32), cb=row(ffn_conv_b),
        wdn=w_ffn_down.astype(BF16),
    )
    return p


def _layer(x, p):
    b, t, _ = x.shape
    assert t % RW_TILE == 0 and t % ATT_TILE == 0 and t % FFN_TILE == 0
    assert (b * t) % PROJ_TILE == 0 and (b * t) % MERGE_TILE == 0
    x2 = x.reshape(b * t, D_MODEL)
    q, kv, rkv, zl = _in_proj(x2, p["g_mix_pre"], p["wq"], p["wkv"], p["wrkv"], p["wzl"],
                              p["mu_rkv"], p["mu_zl"], b, t)
    o_attn = _attention(q, kv, p["sink"], p["bias"], b, t)

    def consts(d):
        return [p["w0"][d], p["w2"][d], p["a0"][d], p["a2"][d],
                p["k_k"], p["k_a"]]

    y_fwd = _rwkv_pass(False, rkv, zl, consts(0), None, b, t)
    o_rwkv = _rwkv_pass(True, rkv, zl, consts(1),
                        [y_fwd, p["a0"][0], p["a2"][0], p["g2"], p["r_k"], p["ln_w"], p["ln_b"]], b, t)
    h = _merge(x2, o_attn, o_rwkv, p["g_mix_pre"], p["wg"], p["wa"], p["wb"], p["wo"], p["g_mix_post"])
    out = _ffn(h, p["g_ffn_pre"], p["wup"], p["cw"], p["cb"], p["wdn"], p["g_ffn_post"], b, t)
    return out.reshape(b, t, D_MODEL)


def kernel(x_prompt, x_sample, norm_mix_pre, norm_mix_post, norm_ffn_pre, norm_ffn_post, w_in, attn_sink, rw_mu_prev, rw_mu_next, rw_w0, rw_w2, rw_a0, rw_a2, rw_g2, rw_k_k, rw_k_a, rw_r_k, rw_ln_w, rw_ln_b, w_branch_attn, w_branch_rwkv, w_out, w_ffn_up, ffn_conv_w, ffn_conv_b, w_ffn_down):
    weights = (norm_mix_pre, norm_mix_post, norm_ffn_pre, norm_ffn_post, w_in, attn_sink,
               rw_mu_prev, rw_mu_next, rw_w0, rw_w2, rw_a0, rw_a2, rw_g2, rw_k_k, rw_k_a,
               rw_r_k, rw_ln_w, rw_ln_b, w_branch_attn, w_branch_rwkv, w_out,
               w_ffn_up, ffn_conv_w, ffn_conv_b, w_ffn_down)
    depth = w_in.shape[0]
    layers = [_prepare(*(w[l] for w in weights)) for l in range(depth)]

    def run(x):
        for p in layers:
            x = _layer(x, p)
        return x

    return (run(x_prompt), run(x_sample))
```

```python
import functools
import math

import numpy as np
import jax
import jax.numpy as jnp
from jax import lax
from jax.experimental import pallas as pl
from jax.experimental.pallas import tpu as pltpu

F32 = jnp.float32
BF16 = jnp.bfloat16

D_MODEL = 1024
ATT_H = 8
ATT_KV = 2
ATT_G = ATT_H // ATT_KV
ATT_HD = 64
ATT_W = ATT_H * ATT_HD
ATT_KV_W = ATT_KV * ATT_HD
WINDOW = 128
BLK = 128
ATT_SCALE = 1.0 / math.sqrt(ATT_HD)
RW_H = 8
RW_N = 64
RW_W = RW_H * RW_N
DECAY_LORA = 64
AAA_LORA = 64
GATE_LORA = 160
RW_LORA_W = 2 * DECAY_LORA + 2 * AAA_LORA + GATE_LORA
RW_MIX_W = 3 * RW_W + RW_LORA_W
GATE_W = 2 * D_MODEL
D_FF = 2816
NORM_EPS = 1e-6
GN_EPS = 64e-5

VMEM_LIMIT_BYTES = 56 * 2**20

CHUNK = 64
GROUP_H = 4
GROUP_W = GROUP_H * RW_N
N_GROUPS = RW_H // GROUP_H
RW_TILE = 1024
RW_PART_CHUNKS = (5, 5, 4, 2)
RW_PARTS = len(RW_PART_CHUNKS)

PROJ_TILE = 1024
PROJ_HALO = 8
ATT_TILE = 512
MERGE_TILE = 1024
FFN_TILE = 512
FFN_HALO = 8
FFN_SPAN = FFN_HALO * FFN_HALO
FFN_COLS = 256


def _dot(a, b):
    return jnp.dot(a, b, preferred_element_type=F32)


def _dot_nt(a, b):
    return lax.dot_general(a, b, (((1,), (1,)), ((), ())), preferred_element_type=F32)


def _dot_tn(a, b):
    return lax.dot_general(a, b, (((0,), (0,)), ((), ())), preferred_element_type=F32)


def _sigmoid(x):
    return 1.0 / (1.0 + jnp.exp(-x))


def _rms(x, g):
    return x * lax.rsqrt(jnp.mean(x * x, axis=-1, keepdims=True) + NORM_EPS) * g


def _const_spec(shape):
    nd = len(shape)
    return pl.BlockSpec(shape, lambda *_: (0,) * nd, pipeline_mode=pl.Buffered(1))


def _params(n_axes):
    return pltpu.CompilerParams(dimension_semantics=("arbitrary",) * n_axes,
                                vmem_limit_bytes=VMEM_LIMIT_BYTES)


def _in_proj_kernel(xc_ref, xp_ref, xn_ref, g_ref, wq_ref, wkv_ref, wrkv_ref, wzl_ref, mu_rkv_ref, mu_zl_ref,
                    q_ref, kv_ref, rkv_ref, zl_ref):
    j = pl.program_id(1)
    nj = pl.num_programs(1)
    tm = PROJ_TILE
    ext = tm + 2 * PROJ_HALO
    x = jnp.concatenate([xp_ref[...], xc_ref[...], xn_ref[...]], axis=0)
    rid = lax.broadcasted_iota(jnp.int32, (ext, 1), 0)
    keep = jnp.logical_and(jnp.logical_or(rid >= PROJ_HALO, j > 0),
                           jnp.logical_or(rid < PROJ_HALO + tm, j < nj - 1))
    u = jnp.where(keep, _rms(x, g_ref[...]), 0.0).astype(BF16)
    um = u[PROJ_HALO:PROJ_HALO + tm]
    q_ref[...] = (_dot(um, wq_ref[...]) * ATT_SCALE).astype(BF16)
    kv_ref[...] = _dot(um, wkv_ref[...]).astype(BF16)

    def shifted(w_ref, mu_ref, o_ref, cols):
        z = _dot(u, w_ref[:, cols])
        c = z[PROJ_HALO:PROJ_HALO + tm]
        up = pltpu.roll(z, 1, 0)[PROJ_HALO:PROJ_HALO + tm]
        dn = pltpu.roll(z, ext - 1, 0)[PROJ_HALO:PROJ_HALO + tm]
        o_ref[:, cols] = (c + mu_ref[0:1, cols] * (up - c) + mu_ref[1:2, cols] * (dn - c)).astype(BF16)

    for c0 in range(0, 3 * RW_W, RW_W):
        shifted(wrkv_ref, mu_rkv_ref, rkv_ref, slice(c0, c0 + RW_W))
    shifted(wzl_ref, mu_zl_ref, zl_ref, slice(0, RW_LORA_W))


def _in_proj(x, g, wq, wkv, wrkv, wzl, mu_rkv, mu_zl, b, t):
    n = x.shape[0]
    tm = PROJ_TILE
    nj = t // tm
    per = tm // PROJ_HALO
    nhalo = n // PROJ_HALO

    def prev_map(i, j):
        return (jnp.maximum((i * nj + j) * per - 1, 0), 0)

    def next_map(i, j):
        return (jnp.minimum((i * nj + j + 1) * per, nhalo - 1), 0)

    row = lambda w: pl.BlockSpec((tm, w), lambda i, j: (i * nj + j, 0))
    consts = [g, wq, wkv, wrkv, wzl, mu_rkv, mu_zl]
    return pl.pallas_call(
        _in_proj_kernel,
        grid=(b, nj),
        in_specs=[row(D_MODEL), pl.BlockSpec((PROJ_HALO, D_MODEL), prev_map),
                  pl.BlockSpec((PROJ_HALO, D_MODEL), next_map)] + [_const_spec(c.shape) for c in consts],
        out_specs=[row(ATT_W), row(2 * ATT_KV_W), row(3 * RW_W), row(RW_LORA_W)],
        out_shape=[jax.ShapeDtypeStruct((n, ATT_W), BF16),
                   jax.ShapeDtypeStruct((n, 2 * ATT_KV_W), BF16),
                   jax.ShapeDtypeStruct((n, 3 * RW_W), BF16),
                   jax.ShapeDtypeStruct((n, RW_LORA_W), BF16)],
        compiler_params=_params(2),
        name="in_proj",
    )(x, x, x, *consts)


def _attn_bias():
    slopes = np.exp2(-8.0 / ATT_H * np.arange(1, ATT_H + 1, dtype=np.float64))
    dist = np.abs(np.arange(BLK)[:, None] - (np.arange(3 * BLK)[None, :] - BLK))
    bias = -(slopes[:, None, None] * dist[None])
    bias = np.where(dist[None] <= WINDOW, bias, -np.inf)
    bias = bias.reshape(ATT_KV, ATT_G * BLK, 3 * BLK).transpose(0, 2, 1)
    return jnp.asarray(bias, F32)


def _attn_kernel(sink_ref, q_ref, kvp_ref, kvc_ref, kvn_ref, bias_ref, o_ref):
    j = pl.program_id(1)
    nj = pl.num_programs(1)
    has_next = jnp.where(j < nj - 1, 1.0, 0.0).astype(F32)
    kvwin = jnp.concatenate([kvp_ref[...].astype(F32), kvc_ref[...].astype(F32),
                             kvn_ref[...].astype(F32) * has_next], axis=0)
    kwin = kvwin[:, :ATT_KV_W].astype(BF16)
    v_t = kvwin[:, ATT_KV_W:].T.astype(BF16)
    krow = lax.broadcasted_iota(jnp.int32, (3 * BLK, 1), 0)
    n_qb = ATT_TILE // BLK
    gw = ATT_G * ATT_HD
    no_q = jnp.zeros((ATT_HD, ATT_G * BLK), F32)

    pairs = [(qb, kv) for qb in range(n_qb) for kv in range(ATT_KV)]
    every = range(len(pairs))

    def scores(x):
        qb, kv = pairs[x]
        q_t = q_ref[qb * BLK:(qb + 1) * BLK, kv * gw:(kv + 1) * gw].astype(F32).T
        q_t = jnp.concatenate([q_t[g * ATT_HD:(g + 1) * ATT_HD] for g in range(ATT_G)], axis=1)
        q_t = jnp.concatenate([q_t, no_q] if kv == 0 else [no_q, q_t], axis=0).astype(BF16)
        return _dot(kwin[qb * BLK:qb * BLK + 3 * BLK], q_t)

    def softmax(x, s):
        qb, kv = pairs[x]
        key_ok = krow >= jnp.maximum(2 - (j * n_qb + qb), 0) * BLK
        s = jnp.where(key_ok, s + bias_ref[kv], -jnp.inf)
        sink = jnp.concatenate(
            [jnp.full((1, BLK), sink_ref[kv * ATT_G + g], F32) for g in range(ATT_G)], axis=1)
        mx = jnp.maximum(jnp.max(s, axis=0, keepdims=True), sink)
        p = jnp.exp(s - mx)
        den = jnp.sum(p, axis=0, keepdims=True) + jnp.exp(sink - mx)
        return p.astype(BF16), 1.0 / den

    def values(x, p, inv_den):
        qb, kv = pairs[x]
        o_t = _dot(v_t[kv * ATT_HD:(kv + 1) * ATT_HD, qb * BLK:qb * BLK + 3 * BLK], p) * inv_den
        o_t = jnp.concatenate([o_t[:, g * BLK:(g + 1) * BLK] for g in range(ATT_G)], axis=0)
        o_ref[qb * BLK:(qb + 1) * BLK, kv * gw:(kv + 1) * gw] = o_t.T.astype(BF16)

    s_all = [scores(x) for x in every]
    p_all = [softmax(x, s_all[x]) for x in every]
    for x in every:
        values(x, *p_all[x])


def _attention(q, kv, sink, bias, b, t):
    n = q.shape[0]
    tq = ATT_TILE
    nj = t // tq
    per = tq // BLK
    nblk = n // BLK

    def prev_map(i, j):
        return (jnp.maximum((i * nj + j) * per - 1, 0), 0)

    def next_map(i, j):
        return (jnp.minimum((i * nj + j + 1) * per, nblk - 1), 0)

    return pl.pallas_call(
        _attn_kernel,
        grid=(b, nj),
        in_specs=[pl.BlockSpec(memory_space=pltpu.SMEM),
                  pl.BlockSpec((tq, ATT_W), lambda i, j: (i * nj + j, 0)),
                  pl.BlockSpec((BLK, 2 * ATT_KV_W), prev_map),
                  pl.BlockSpec((tq, 2 * ATT_KV_W), lambda i, j: (i * nj + j, 0)),
                  pl.BlockSpec((BLK, 2 * ATT_KV_W), next_map),
                  _const_spec(bias.shape)],
        out_specs=pl.BlockSpec((tq, ATT_W), lambda i, j: (i * nj + j, 0)),
        out_shape=jax.ShapeDtypeStruct((n, ATT_W), BF16),
        compiler_params=_params(2),
        name="attention",
    )(sink, q, kv, kv, kv, bias)


def _block_diag(x, mask):
    tiled = jnp.concatenate([x.astype(F32)] * GROUP_H, axis=0)
    return jnp.where(mask, tiled, 0.0).astype(BF16)


def _head_sum(x):
    lanes = 2 * RW_N
    low = lax.broadcasted_iota(jnp.int32, (x.shape[0], lanes), 1) < RW_N
    out = []
    for c0 in range(0, RW_W, lanes):
        xc = x[:, c0:c0 + lanes]
        lo = jnp.sum(jnp.where(low, xc, 0.0), axis=1, keepdims=True)
        hi = jnp.sum(jnp.where(low, 0.0, xc), axis=1, keepdims=True)
        out.append(jnp.where(low, lo, hi))
    return jnp.concatenate(out, axis=1)


def _rwkv_kernel(reverse, final, *refs):
    if final:
        (rkv_ref, zl_ref, w0, w2, a0, a2, k_k, k_a,
         y0, a0o, a2o, g2, r_k, ln_w, ln_b, out_ref, s_ref, y_s) = refs
    else:
        (rkv_ref, zl_ref, w0, w2, a0, a2, k_k, k_a, out_ref, s_ref, y_s) = refs

    j = pl.program_id(1)

    @pl.when(j == 0)
    def _():
        s_ref[...] = jnp.zeros_like(s_ref)

    rid = lax.broadcasted_iota(jnp.int32, (CHUNK, CHUNK), 0)
    cid = lax.broadcasted_iota(jnp.int32, (CHUNK, CHUNK), 1)
    lmat = jnp.where((cid >= rid) if reverse else (cid <= rid), 1.0, 0.0).astype(BF16)

    def prepare(chunks, res):
        lo = min(chunks) * CHUNK
        rows = slice(lo, (max(chunks) + 1) * CHUNK)
        r = rkv_ref[rows, :RW_W].astype(F32)
        k = rkv_ref[rows, RW_W:2 * RW_W].astype(F32)
        v_bf = rkv_ref[rows, 2 * RW_W:]
        wd = zl_ref[rows, :2 * DECAY_LORA].astype(F32)
        ad = zl_ref[rows, 2 * DECAY_LORA:2 * DECAY_LORA + 2 * AAA_LORA]
        w = w0[...] + _dot(jnp.tanh(wd).astype(BF16), w2[...])
        lw = -math.exp(-0.5) * _sigmoid(w)
        a = _sigmoid(a0[...] + _dot(ad, a2[...]))
        yield
        lw_hi = lw.astype(BF16)
        lw_lo = (lw - lw_hi.astype(F32)).astype(BF16)
        cum = jnp.concatenate(
            [_dot(lmat, lw_hi[c0:c0 + CHUNK]) + _dot(lmat, lw_lo[c0:c0 + CHUNK])
             for c0 in range(0, len(chunks) * CHUNK, CHUNK)], axis=0)
        yield
        kkk = k * k_k[...]
        kk = kkk / jnp.maximum(jnp.sqrt(_head_sum(kkk * kkk)), 1e-12)
        kdir = k * (1.0 + (a - 1.0) * k_a[...])
        bb = kk * a
        yield
        res["at"] = (-kk * jnp.exp(cum - lw)).astype(BF16)
        res["rt"] = (r * jnp.exp(cum)).astype(BF16)
        yield
        e_neg = jnp.exp(-cum)
        res["bt"] = (bb * e_neg).astype(BF16)
        res["kt"] = (kdir * e_neg).astype(BF16)
        res["v"] = v_bf
        yield
        bp, kp = [], []
        for ci in sorted(chunks):
            local = slice(ci * CHUNK - lo, (ci + 1) * CHUNK - lo)
            end = local.start if reverse else local.stop - 1
            cum_end = cum[end:end + 1, :]
            e_end = jnp.exp(cum_end - cum[local])
            bp.append((bb[local] * e_end).astype(BF16))
            kp.append((kdir[local] * e_end).astype(BF16))
            res["pc", ci] = jnp.exp(cum_end)
        res["bp"] = jnp.concatenate(bp, axis=0)
        res["kp"] = jnp.concatenate(kp, axis=0)
        yield
        if final:
            a_other = _sigmoid(a0o[...] + _dot(ad, a2o[...]))
            ksum = k * (2.0 + (a + a_other - 2.0) * k_a[...])
            res["bonus"] = _head_sum(r * ksum * r_k[...]) * v_bf.astype(F32)
            yield
            gd = zl_ref[rows, 2 * DECAY_LORA + 2 * AAA_LORA:].astype(F32)
            res["gate"] = _dot(_sigmoid(gd).astype(BF16), g2[...])
            yield

    trow = lax.broadcasted_iota(jnp.int32, (CHUNK, GROUP_W), 0)
    scol = lax.broadcasted_iota(jnp.int32, (CHUNK, GROUP_W), 1) % CHUNK
    strict = (scol > trow) if reverse else (scol < trow)
    incl = (scol >= trow) if reverse else (scol <= trow)
    eye = jnp.where(scol == trow, 1.0, 0.0).astype(F32)
    bdr = lax.broadcasted_iota(jnp.int32, (GROUP_H * CHUNK, GROUP_W), 0) // CHUNK
    bdc = lax.broadcasted_iota(jnp.int32, (GROUP_H * CHUNK, GROUP_W), 1) // RW_N
    bdmask = bdr == bdc
    bd = functools.partial(_block_diag, mask=bdmask)

    def solve(chunks, tok, res):
        lo = min(chunks) * CHUNK
        pairs = [(ci, g) for ci in chunks for g in range(N_GROUPS)]

        def ld(name):
            return {(ci, g): tok[name][ci * CHUNK - lo:(ci + 1) * CHUNK - lo, g * GROUP_W:(g + 1) * GROUP_W]
                    for ci, g in pairs}

        def stage(fn):
            return {x: fn(x) for x in pairs}

        def mxu_stage(fn):
            out = {}
            for n_done, x in enumerate(pairs, 1):
                out[x] = fn(x)
                if 2 * n_done == len(pairs):
                    yield
            yield
            return out

        a_t, r_t, v_c, b_t, k_t = ld("at"), ld("rt"), ld("v"), ld("bt"), ld("kt")
        ar = stage(lambda x: jnp.concatenate([a_t[x], r_t[x]], axis=0))
        g1 = yield from mxu_stage(lambda x: _dot_nt(ar[x], bd(b_t[x])))
        g2_ = yield from mxu_stage(lambda x: _dot_nt(ar[x], bd(k_t[x])))
        ab = stage(lambda x: jnp.where(strict, g1[x][:CHUNK], 0.0))
        rb = stage(lambda x: jnp.where(incl, g1[x][CHUNK:], 0.0))
        ak = stage(lambda x: jnp.where(strict, g2_[x][:CHUNK], 0.0))
        rk = stage(lambda x: jnp.where(incl, g2_[x][CHUNK:], 0.0))

        t_inv = stage(lambda x: eye + ab[x])
        pw = yield from mxu_stage(lambda x: _dot(ab[x].astype(BF16), bd(ab[x])))
        n_lvl = int(math.log2(CHUNK))
        for lvl in range(1, n_lvl):
            if lvl < n_lvl - 1:
                both = yield from mxu_stage(
                    lambda x: _dot(jnp.concatenate([pw[x], t_inv[x]], axis=0).astype(BF16), bd(pw[x])))
                pw = stage(lambda x: both[x][:CHUNK])
                t_inv = stage(lambda x, t=t_inv: t[x] + both[x][CHUNK:])
            else:
                t_inv = yield from mxu_stage(
                    lambda x, t=t_inv: t[x] + _dot(t[x].astype(BF16), bd(pw[x])))

        res["wv"] = yield from mxu_stage(lambda x: _dot(ak[x].astype(BF16), bd(v_c[x])))
        b_p, k_p = ld("bp"), ld("kp")
        res["ar"] = ar
        res["t"] = stage(lambda x: t_inv[x].astype(BF16))
        res["rbk"] = stage(lambda x: jnp.concatenate([rb[x], rk[x]], axis=1).astype(BF16))
        res["v"] = v_c
        res["bkp"] = stage(lambda x: jnp.concatenate([b_p[x], k_p[x]], axis=0))

    def carry(chunks, tok, res, state):
        groups = range(N_GROUPS)
        for ci in chunks:
            a_s = [_dot_nt(res["ar"][ci, g], state[g].astype(BF16)) for g in groups]
            yield
            u = [_dot(res["t"][ci, g], bd(a_s[g][:CHUNK] + res["wv"][ci, g])) for g in groups]
            yield
            for g in groups:
                x = (ci, g)
                lanes = slice(g * GROUP_W, (g + 1) * GROUP_W)
                y_s[ci * CHUNK:(ci + 1) * CHUNK, lanes] = a_s[g][CHUNK:] + _dot(
                    res["rbk"][x], jnp.concatenate([bd(u[g]), bd(res["v"][x])], axis=0))
                upd = _dot_tn(jnp.concatenate([u[g].astype(BF16), res["v"][x]], axis=0), res["bkp"][x])
                state[g] = state[g] * tok["pc", ci][:, lanes] + jnp.where(bdmask, upd, 0.0)
            yield

    def run(gen, *fill):
        for _ in gen:
            for f in fill:
                next(f, None)
        for f in fill:
            for _ in f:
                pass

    n_ch = RW_TILE // CHUNK
    scan = [n_ch - 1 - s if reverse else s for s in range(n_ch)]
    assert sum(RW_PART_CHUNKS) == n_ch
    starts = [sum(RW_PART_CHUNKS[:p]) for p in range(RW_PARTS)]
    parts = [scan[s:s + n] for s, n in zip(starts, RW_PART_CHUNKS)]
    tok = [{} for _ in parts]
    sol = [{} for _ in parts]
    state = [s_ref[g] for g in range(N_GROUPS)]
    run(prepare(parts[0], tok[0]))
    for p in range(RW_PARTS):
        fill = []
        if p > 0:
            fill.append(carry(parts[p - 1], tok[p - 1], sol[p - 1], state))
        if p + 1 < RW_PARTS:
            fill.append(prepare(parts[p + 1], tok[p + 1]))
        run(solve(parts[p], tok[p], sol[p]), *fill)
    run(carry(parts[-1], tok[-1], sol[-1], state))
    for g in range(N_GROUPS):
        s_ref[g] = state[g]

    y = y_s[...]
    if not final:
        out_ref[...] = y
    else:
        in_rows = sorted(range(RW_PARTS), key=lambda p: min(parts[p]))
        bonus = jnp.concatenate([tok[p]["bonus"] for p in in_rows], axis=0)
        gate = jnp.concatenate([tok[p]["gate"] for p in in_rows], axis=0)
        ytot = y + y0[...]
        inv_n = 1.0 / RW_N
        mu = _head_sum(ytot) * inv_n
        d = ytot - mu
        var = _head_sum(d * d) * inv_n
        yn = d * lax.rsqrt(var + GN_EPS) * ln_w[...] + ln_b[...]
        out_ref[...] = ((yn + bonus) * gate).astype(BF16)


def _rwkv_pass(reverse, rkv, zl, consts, extra, b, t):
    n = rkv.shape[0]
    tb = RW_TILE
    nj = t // tb
    final = reverse

    def cur_map(i, j):
        return (i * nj + (nj - 1 - j if reverse else j), 0)

    in_specs = ([pl.BlockSpec((tb, 3 * RW_W), cur_map), pl.BlockSpec((tb, RW_LORA_W), cur_map)]
                + [_const_spec(c.shape) for c in consts])
    args = [rkv, zl] + list(consts)
    if final:
        y0 = extra[0]
        in_specs += [pl.BlockSpec((tb, RW_W), cur_map)] + [_const_spec(c.shape) for c in extra[1:]]
        args += list(extra)
    out_dtype = BF16 if final else F32
    return pl.pallas_call(
        functools.partial(_rwkv_kernel, reverse, final),
        grid=(b, nj),
        in_specs=in_specs,
        out_specs=pl.BlockSpec((tb, RW_W), cur_map),
        out_shape=jax.ShapeDtypeStruct((n, RW_W), out_dtype),
        scratch_shapes=[pltpu.VMEM((N_GROUPS, GROUP_W, GROUP_W), F32),
                        pltpu.VMEM((tb, RW_W), F32)],
        compiler_params=_params(2),
        name="rwkv_bwd" if reverse else "rwkv_fwd",
    )(*args)


def _merge_kernel(x_ref, oa_ref, ob_ref, gpre_ref, wg_ref, wa_ref, wb_ref, wo_ref, gpost_ref, h_ref):
    x = x_ref[...]
    u = _rms(x, gpre_ref[...]).astype(BF16)
    gates = _sigmoid(_dot(u, wg_ref[...]))
    merged = (gates[:, :D_MODEL] * _dot(oa_ref[...], wa_ref[...])
              + gates[:, D_MODEL:] * _dot(ob_ref[...], wb_ref[...]))
    m = _dot(merged.astype(BF16), wo_ref[...])
    h_ref[...] = x + _rms(m, gpost_ref[...])


def _merge(x, oa, ob, gpre, wg, wa, wb, wo, gpost):
    n = x.shape[0]
    tm = MERGE_TILE
    row = lambda w: pl.BlockSpec((tm, w), lambda i: (i, 0))
    consts = [gpre, wg, wa, wb, wo, gpost]
    return pl.pallas_call(
        _merge_kernel,
        grid=(n // tm,),
        in_specs=[row(D_MODEL), row(ATT_W), row(RW_W)] + [_const_spec(c.shape) for c in consts],
        out_specs=row(D_MODEL),
        out_shape=jax.ShapeDtypeStruct((n, D_MODEL), F32),
        compiler_params=_params(1),
        name="merge",
    )(x, oa, ob, *consts)


def _gelu_tanh(x):
    return 0.5 * x * (1.0 + jnp.tanh(math.sqrt(2.0 / math.pi) * (x + 0.044715 * (x * x * x))))


def _ffn_kernel(hc_ref, hp_ref, hn_ref, gpre_ref, wup_ref, cw_ref, cb_ref, wdn_ref, gpost_ref, o_ref,
                slab_s):
    j = pl.program_id(1)
    nj = pl.num_programs(1)
    tm = FFN_TILE
    sub = FFN_HALO
    nb = tm // FFN_SPAN
    lane_w = 128
    n_slab = D_MODEL // lane_w

    for c in range(n_slab):
        slab_s[c] = hc_ref[:, c * lane_w:(c + 1) * lane_w]
    hc = jnp.concatenate(
        [jnp.concatenate([slab_s[c, pl.ds(blk * FFN_SPAN + v, sub, stride=sub), :] for c in range(n_slab)],
                         axis=1)
         for blk in range(nb) for v in range(sub)], axis=0)

    sid = lax.broadcasted_iota(jnp.int32, (sub, 1), 0)
    is_before = jnp.logical_and(sid == 0, j > 0)
    is_after = jnp.logical_and(sid == sub - 1, j < nj - 1)
    edge = jnp.where(is_before, hp_ref[sub - 1:sub, :], 0.0) + jnp.where(is_after, hn_ref[0:1, :], 0.0)
    u_edge = jnp.where(jnp.logical_or(is_before, is_after), _rms(edge, gpre_ref[...]), 0.0)
    u = jnp.concatenate([_rms(hc, gpre_ref[...]), u_edge], axis=0).astype(BF16)

    def up(c0):
        return [_dot(u, wup_ref[:, off:off + FFN_COLS]) for off in (c0, D_FF + c0)]

    def conv(hh, off):
        cols = slice(off, off + FFN_COLS)
        grp = lambda blk, v: hh[(blk * sub + v) * sub:(blk * sub + v + 1) * sub]
        edge_h = hh[tm:]
        down = [pltpu.roll(grp(blk, sub - 1), 1, 0) for blk in range(nb)]
        upw = [pltpu.roll(grp(blk, 0), sub - 1, 0) for blk in range(nb)]
        prev, nxt = [], []
        for blk in range(nb):
            first = jnp.where(sid == 0, down[blk - 1] if blk > 0 else edge_h, down[blk])
            last = jnp.where(sid == sub - 1, upw[blk + 1] if blk + 1 < nb else edge_h, upw[blk])
            lo = blk * FFN_SPAN
            prev += [first, hh[lo:lo + FFN_SPAN - sub]]
            nxt += [hh[lo + sub:lo + FFN_SPAN], last]
        prev = jnp.concatenate(prev, axis=0)
        nxt = jnp.concatenate(nxt, axis=0)
        return (prev * cw_ref[0:1, cols] + hh[:tm] * cw_ref[1:2, cols]
                + nxt * cw_ref[2:3, cols] + cb_ref[:, cols])

    starts = list(range(0, D_FF, FFN_COLS))
    pending = up(starts[0])
    acts = []
    for i, c0 in enumerate(starts):
        hh = pending
        if i + 1 < len(starts):
            pending = up(starts[i + 1])
        acts.append((_gelu_tanh(conv(hh[0], c0)) * conv(hh[1], D_FF + c0)).astype(BF16))
    f = _dot(jnp.concatenate(acts, axis=1), wdn_ref[...])
    out = hc + _rms(f, gpost_ref[...])
    for r in range(tm // sub):
        for c in range(n_slab):
            slab_s[c, pl.ds((r // sub) * FFN_SPAN + r % sub, sub, stride=sub), :] = (
                out[r * sub:(r + 1) * sub, c * lane_w:(c + 1) * lane_w])
    for c in range(n_slab):
        o_ref[:, c * lane_w:(c + 1) * lane_w] = slab_s[c]


def _ffn(h, gpre, wup, cw, cb, wdn, gpost, b, t):
    n = h.shape[0]
    tm = FFN_TILE
    nj = t // tm
    per = tm // FFN_HALO
    nhalo = n // FFN_HALO

    def prev_map(i, j):
        return (jnp.maximum((i * nj + j) * per - 1, 0), 0)

    def next_map(i, j):
        return (jnp.minimum((i * nj + j + 1) * per, nhalo - 1), 0)

    consts = [gpre, wup, cw, cb, wdn, gpost]
    return pl.pallas_call(
        _ffn_kernel,
        grid=(b, nj),
        in_specs=[pl.BlockSpec((tm, D_MODEL), lambda i, j: (i * nj + j, 0)),
                  pl.BlockSpec((FFN_HALO, D_MODEL), prev_map),
                  pl.BlockSpec((FFN_HALO, D_MODEL), next_map)]
                 + [_const_spec(c.shape) for c in consts],
        out_specs=pl.BlockSpec((tm, D_MODEL), lambda i, j: (i * nj + j, 0)),
        out_shape=jax.ShapeDtypeStruct((n, D_MODEL), F32),
        scratch_shapes=[pltpu.VMEM((D_MODEL // 128, tm, 128), F32)],
        compiler_params=_params(2),
        name="ffn",
    )(h, h, h, *consts)


def _prepare(norm_mix_pre, norm_mix_post, norm_ffn_pre, norm_ffn_post, w_in, attn_sink,
             rw_mu_prev, rw_mu_next, rw_w0, rw_w2, rw_a0, rw_a2, rw_g2, rw_k_k, rw_k_a,
             rw_r_k, rw_ln_w, rw_ln_b, w_branch_attn, w_branch_rwkv, w_out,
             w_ffn_up, ffn_conv_w, ffn_conv_b, w_ffn_down):
    c_q = ATT_W
    c_kv = c_q + 2 * ATT_KV_W
    c_rkv = c_kv + 3 * RW_W
    c_zl = c_rkv + RW_LORA_W
    row = lambda p: p.reshape(1, -1).astype(F32)

    def lora_pad(w2, d, n_lora):
        z = jnp.zeros((2 * n_lora, RW_W), F32)
        return z.at[d * n_lora:(d + 1) * n_lora].set(w2[d]).astype(BF16)

    mu = jnp.stack([rw_mu_prev, rw_mu_next]).astype(F32)
    p = dict(
        g_mix_pre=row(norm_mix_pre), g_mix_post=row(norm_mix_post),
        g_ffn_pre=row(norm_ffn_pre), g_ffn_post=row(norm_ffn_post),
        wq=w_in[:, :c_q].astype(BF16), wkv=w_in[:, c_q:c_kv].astype(BF16),
        wrkv=w_in[:, c_kv:c_rkv].astype(BF16), wzl=w_in[:, c_rkv:c_zl].astype(BF16),
        wg=w_in[:, c_zl:].astype(BF16),
        sink=attn_sink.astype(F32), bias=_attn_bias(),
        mu_rkv=mu[:, :3 * RW_W], mu_zl=mu[:, 3 * RW_W:],
        w0=[row(rw_w0[d]) for d in range(2)],
        w2=[lora_pad(rw_w2, d, DECAY_LORA) for d in range(2)],
        a0=[row(rw_a0[d]) for d in range(2)],
        a2=[lora_pad(rw_a2, d, AAA_LORA) for d in range(2)],
        g2=rw_g2.astype(BF16), k_k=row(rw_k_k), k_a=row(rw_k_a), r_k=row(rw_r_k),
        ln_w=row(rw_ln_w), ln_b=row(rw_ln_b),
        wa=w_branch_attn.astype(BF16), wb=w_branch_rwkv.astype(BF16), wo=w_out.astype(BF16),
        wup=w_ffn_up.astype(BF16), cw=ffn_conv_w.astype(F32), cb=row(ffn_conv_b),
        wdn=w_ffn_down.astype(BF16),
    )
    return p


def _layer(x, p):
    b, t, _ = x.shape
    assert t % RW_TILE == 0 and t % ATT_TILE == 0 and t % FFN_TILE == 0
    assert (b * t) % PROJ_TILE == 0 and (b * t) % MERGE_TILE == 0
    x2 = x.reshape(b * t, D_MODEL)
    q, kv, rkv, zl = _in_proj(x2, p["g_mix_pre"], p["wq"], p["wkv"], p["wrkv"], p["wzl"],
                              p["mu_rkv"], p["mu_zl"], b, t)
    o_attn = _attention(q, kv, p["sink"], p["bias"], b, t)

    def consts(d):
        return [p["w0"][d], p["w2"][d], p["a0"][d], p["a2"][d],
                p["k_k"], p["k_a"]]

    y_fwd = _rwkv_pass(False, rkv, zl, consts(0), None, b, t)
    o_rwkv = _rwkv_pass(True, rkv, zl, consts(1),
                        [y_fwd, p["a0"][0], p["a2"][0], p["g2"], p["r_k"], p["ln_w"], p["ln_b"]], b, t)
    h = _merge(x2, o_attn, o_rwkv, p["g_mix_pre"], p["wg"], p["wa"], p["wb"], p["wo"], p["g_mix_post"])
    out = _ffn(h, p["g_ffn_pre"], p["wup"], p["cw"], p["cb"], p["wdn"], p["g_ffn_post"], b, t)
    return out.reshape(b, t, D_MODEL)


def kernel(x_prompt, x_sample, norm_mix_pre, norm_mix_post, norm_ffn_pre, norm_ffn_post, w_in, attn_sink, rw_mu_prev, rw_mu_next, rw_w0, rw_w2, rw_a0, rw_a2, rw_g2, rw_k_k, rw_k_a, rw_r_k, rw_ln_w, rw_ln_b, w_branch_attn, w_branch_rwkv, w_out, w_ffn_up, ffn_conv_w, ffn_conv_b, w_ffn_down):
    weights = (norm_mix_pre, norm_mix_post, norm_ffn_pre, norm_ffn_post, w_in, attn_sink,
               rw_mu_prev, rw_mu_next, rw_w0, rw_w2, rw_a0, rw_a2, rw_g2, rw_k_k, rw_k_a,
               rw_r_k, rw_ln_w, rw_ln_b, w_branch_attn, w_branch_rwkv, w_out,
               w_ffn_up, ffn_conv_w, ffn_conv_b, w_ffn_down)
    depth = w_in.shape[0]
    layers = [_prepare(*(w[l] for w in weights)) for l in range(depth)]

    def run(x):
        for p in layers:
            x = _layer(x, p)
        return x

    return (run(x_prompt), run(x_sample))
```

```python
import functools
import math

import numpy as np
import jax
import jax.numpy as jnp
from jax import lax
from jax.experimental import pallas as pl
from jax.experimental.pallas import tpu as pltpu

F32 = jnp.float32
BF16 = jnp.bfloat16

D_MODEL = 1024
ATT_H = 8
ATT_KV = 2
ATT_G = ATT_H // ATT_KV
ATT_HD = 64
ATT_W = ATT_H * ATT_HD
ATT_KV_W = ATT_KV * ATT_HD
WINDOW = 128
BLK = 128
ATT_SCALE = 1.0 / math.sqrt(ATT_HD)
RW_H = 8
RW_N = 64
RW_W = RW_H * RW_N
DECAY_LORA = 64
AAA_LORA = 64
GATE_LORA = 160
RW_LORA_W = 2 * DECAY_LORA + 2 * AAA_LORA + GATE_LORA
RW_MIX_W = 3 * RW_W + RW_LORA_W
GATE_W = 2 * D_MODEL
D_FF = 2816
NORM_EPS = 1e-6
GN_EPS = 64e-5

VMEM_LIMIT_BYTES = 56 * 2**20

CHUNK = 64
GROUP_H = 4
GROUP_W = GROUP_H * RW_N
N_GROUPS = RW_H // GROUP_H
RW_TILE = 1024
RW_PARTS = 4

PROJ_TILE = 1024
PROJ_HALO = 8
ATT_TILE = 1024
MERGE_TILE = 1024
FFN_TILE = 512
FFN_HALO = 8
FFN_SPAN = FFN_HALO * FFN_HALO
FFN_COLS = 256


def _dot(a, b):
    return jnp.dot(a, b, preferred_element_type=F32)


def _dot_nt(a, b):
    return lax.dot_general(a, b, (((1,), (1,)), ((), ())), preferred_element_type=F32)


def _dot_tn(a, b):
    return lax.dot_general(a, b, (((0,), (0,)), ((), ())), preferred_element_type=F32)


def _sigmoid(x):
    return 1.0 / (1.0 + jnp.exp(-x))


def _rms(x, g):
    return x * lax.rsqrt(jnp.mean(x * x, axis=-1, keepdims=True) + NORM_EPS) * g


def _const_spec(shape):
    nd = len(shape)
    return pl.BlockSpec(shape, lambda *_: (0,) * nd, pipeline_mode=pl.Buffered(1))


def _params(n_axes):
    return pltpu.CompilerParams(dimension_semantics=("arbitrary",) * n_axes,
                                vmem_limit_bytes=VMEM_LIMIT_BYTES)


def _in_proj_kernel(xc_ref, xp_ref, xn_ref, g_ref, wq_ref, wkv_ref, wrkv_ref, wzl_ref, mu_rkv_ref, mu_zl_ref,
                    q_ref, kv_ref, rkv_ref, zl_ref):
    j = pl.program_id(1)
    nj = pl.num_programs(1)
    tm = PROJ_TILE
    ext = tm + 2 * PROJ_HALO
    x = jnp.concatenate([xp_ref[...], xc_ref[...], xn_ref[...]], axis=0)
    rid = lax.broadcasted_iota(jnp.int32, (ext, 1), 0)
    keep = jnp.logical_and(jnp.logical_or(rid >= PROJ_HALO, j > 0),
                           jnp.logical_or(rid < PROJ_HALO + tm, j < nj - 1))
    u = jnp.where(keep, _rms(x, g_ref[...]), 0.0).astype(BF16)
    um = u[PROJ_HALO:PROJ_HALO + tm]
    q_ref[...] = (_dot(um, wq_ref[...]) * ATT_SCALE).astype(BF16)
    kv_ref[...] = _dot(um, wkv_ref[...]).astype(BF16)

    def shifted(w_ref, mu_ref, o_ref, cols):
        z = _dot(u, w_ref[:, cols])
        c = z[PROJ_HALO:PROJ_HALO + tm]
        up = pltpu.roll(z, 1, 0)[PROJ_HALO:PROJ_HALO + tm]
        dn = pltpu.roll(z, ext - 1, 0)[PROJ_HALO:PROJ_HALO + tm]
        o_ref[:, cols] = (c + mu_ref[0:1, cols] * (up - c) + mu_ref[1:2, cols] * (dn - c)).astype(BF16)

    for c0 in range(0, 3 * RW_W, RW_W):
        shifted(wrkv_ref, mu_rkv_ref, rkv_ref, slice(c0, c0 + RW_W))
    shifted(wzl_ref, mu_zl_ref, zl_ref, slice(0, RW_LORA_W))


def _in_proj(x, g, wq, wkv, wrkv, wzl, mu_rkv, mu_zl, b, t):
    n = x.shape[0]
    tm = PROJ_TILE
    nj = t // tm
    per = tm // PROJ_HALO
    nhalo = n // PROJ_HALO

    def prev_map(i, j):
        return (jnp.maximum((i * nj + j) * per - 1, 0), 0)

    def next_map(i, j):
        return (jnp.minimum((i * nj + j + 1) * per, nhalo - 1), 0)

    row = lambda w: pl.BlockSpec((tm, w), lambda i, j: (i * nj + j, 0))
    consts = [g, wq, wkv, wrkv, wzl, mu_rkv, mu_zl]
    return pl.pallas_call(
        _in_proj_kernel,
        grid=(b, nj),
        in_specs=[row(D_MODEL), pl.BlockSpec((PROJ_HALO, D_MODEL), prev_map),
                  pl.BlockSpec((PROJ_HALO, D_MODEL), next_map)] + [_const_spec(c.shape) for c in consts],
        out_specs=[row(ATT_W), row(2 * ATT_KV_W), row(3 * RW_W), row(RW_LORA_W)],
        out_shape=[jax.ShapeDtypeStruct((n, ATT_W), BF16),
                   jax.ShapeDtypeStruct((n, 2 * ATT_KV_W), BF16),
                   jax.ShapeDtypeStruct((n, 3 * RW_W), BF16),
                   jax.ShapeDtypeStruct((n, RW_LORA_W), BF16)],
        compiler_params=_params(2),
        name="in_proj",
    )(x, x, x, *consts)


def _attn_bias():
    slopes = np.exp2(-8.0 / ATT_H * np.arange(1, ATT_H + 1, dtype=np.float64))
    dist = np.abs(np.arange(BLK)[:, None] - (np.arange(3 * BLK)[None, :] - BLK))
    bias = -(slopes[:, None, None] * dist[None])
    bias = np.where(dist[None] <= WINDOW, bias, -np.inf)
    bias = bias.reshape(ATT_KV, ATT_G * BLK, 3 * BLK).transpose(0, 2, 1)
    return jnp.asarray(bias, F32)


def _attn_kernel(sink_ref, q_ref, kvp_ref, kvc_ref, kvn_ref, bias_ref, o_ref):
    j = pl.program_id(1)
    nj = pl.num_programs(1)
    has_next = jnp.where(j < nj - 1, 1.0, 0.0).astype(F32)
    kvwin = jnp.concatenate([kvp_ref[...].astype(F32), kvc_ref[...].astype(F32),
                             kvn_ref[...].astype(F32) * has_next], axis=0)
    kwin = kvwin[:, :ATT_KV_W].astype(BF16)
    v_t = kvwin[:, ATT_KV_W:].T.astype(BF16)
    krow = lax.broadcasted_iota(jnp.int32, (3 * BLK, 1), 0)
    n_qb = ATT_TILE // BLK
    gw = ATT_G * ATT_HD
    no_q = jnp.zeros((ATT_HD, ATT_G * BLK), F32)

    pairs = [(qb, kv) for qb in range(n_qb) for kv in range(ATT_KV)]
    every = range(len(pairs))

    def scores(x):
        qb, kv = pairs[x]
        q_t = q_ref[qb * BLK:(qb + 1) * BLK, kv * gw:(kv + 1) * gw].astype(F32).T
        q_t = jnp.concatenate([q_t[g * ATT_HD:(g + 1) * ATT_HD] for g in range(ATT_G)], axis=1)
        q_t = jnp.concatenate([q_t, no_q] if kv == 0 else [no_q, q_t], axis=0).astype(BF16)
        return _dot(kwin[qb * BLK:qb * BLK + 3 * BLK], q_t)

    def softmax(x, s):
        qb, kv = pairs[x]
        key_ok = krow >= jnp.maximum(2 - (j * n_qb + qb), 0) * BLK
        s = jnp.where(key_ok, s + bias_ref[kv], -jnp.inf)
        sink = jnp.concatenate(
            [jnp.full((1, BLK), sink_ref[kv * ATT_G + g], F32) for g in range(ATT_G)], axis=1)
        mx = jnp.maximum(jnp.max(s, axis=0, keepdims=True), sink)
        p = jnp.exp(s - mx)
        den = jnp.sum(p, axis=0, keepdims=True) + jnp.exp(sink - mx)
        return p.astype(BF16), 1.0 / den

    def values(x, p, inv_den):
        qb, kv = pairs[x]
        o_t = _dot(v_t[kv * ATT_HD:(kv + 1) * ATT_HD, qb * BLK:qb * BLK + 3 * BLK], p) * inv_den
        o_t = jnp.concatenate([o_t[:, g * BLK:(g + 1) * BLK] for g in range(ATT_G)], axis=0)
        o_ref[qb * BLK:(qb + 1) * BLK, kv * gw:(kv + 1) * gw] = o_t.T.astype(BF16)

    s_all = [scores(x) for x in every]
    p_all = [softmax(x, s_all[x]) for x in every]
    for x in every:
        values(x, *p_all[x])


def _attention(q, kv, sink, bias, b, t):
    n = q.shape[0]
    tq = ATT_TILE
    nj = t // tq
    per = tq // BLK
    nblk = n // BLK

    def prev_map(i, j):
        return (jnp.maximum((i * nj + j) * per - 1, 0), 0)

    def next_map(i, j):
        return (jnp.minimum((i * nj + j + 1) * per, nblk - 1), 0)

    return pl.pallas_call(
        _attn_kernel,
        grid=(b, nj),
        in_specs=[pl.BlockSpec(memory_space=pltpu.SMEM),
                  pl.BlockSpec((tq, ATT_W), lambda i, j: (i * nj + j, 0)),
                  pl.BlockSpec((BLK, 2 * ATT_KV_W), prev_map),
                  pl.BlockSpec((tq, 2 * ATT_KV_W), lambda i, j: (i * nj + j, 0)),
                  pl.BlockSpec((BLK, 2 * ATT_KV_W), next_map),
                  _const_spec(bias.shape)],
        out_specs=pl.BlockSpec((tq, ATT_W), lambda i, j: (i * nj + j, 0)),
        out_shape=jax.ShapeDtypeStruct((n, ATT_W), BF16),
        compiler_params=_params(2),
        name="attention",
    )(sink, q, kv, kv, kv, bias)


def _block_diag(x, mask):
    tiled = jnp.concatenate([x.astype(F32)] * GROUP_H, axis=0)
    return jnp.where(mask, tiled, 0.0).astype(BF16)


def _head_sum(x):
    lanes = 2 * RW_N
    low = lax.broadcasted_iota(jnp.int32, (x.shape[0], lanes), 1) < RW_N
    out = []
    for c0 in range(0, RW_W, lanes):
        xc = x[:, c0:c0 + lanes]
        lo = jnp.sum(jnp.where(low, xc, 0.0), axis=1, keepdims=True)
        hi = jnp.sum(jnp.where(low, 0.0, xc), axis=1, keepdims=True)
        out.append(jnp.where(low, lo, hi))
    return jnp.concatenate(out, axis=1)


def _rwkv_kernel(reverse, final, *refs):
    if final:
        (rkv_ref, zl_ref, w0, w2, a0, a2, k_k, k_a,
         y0, a0o, a2o, g2, r_k, ln_w, ln_b, out_ref, s_ref, y_s) = refs
    else:
        (rkv_ref, zl_ref, w0, w2, a0, a2, k_k, k_a, out_ref, s_ref, y_s) = refs

    j = pl.program_id(1)

    @pl.when(j == 0)
    def _():
        s_ref[...] = jnp.zeros_like(s_ref)

    rid = lax.broadcasted_iota(jnp.int32, (CHUNK, CHUNK), 0)
    cid = lax.broadcasted_iota(jnp.int32, (CHUNK, CHUNK), 1)
    lmat = jnp.where((cid >= rid) if reverse else (cid <= rid), 1.0, 0.0).astype(BF16)

    def prepare(chunks, res):
        lo = min(chunks) * CHUNK
        rows = slice(lo, (max(chunks) + 1) * CHUNK)
        r = rkv_ref[rows, :RW_W].astype(F32)
        k = rkv_ref[rows, RW_W:2 * RW_W].astype(F32)
        v_bf = rkv_ref[rows, 2 * RW_W:]
        wd = zl_ref[rows, :2 * DECAY_LORA].astype(F32)
        ad = zl_ref[rows, 2 * DECAY_LORA:2 * DECAY_LORA + 2 * AAA_LORA]
        w = w0[...] + _dot(jnp.tanh(wd).astype(BF16), w2[...])
        lw = -math.exp(-0.5) * _sigmoid(w)
        a = _sigmoid(a0[...] + _dot(ad, a2[...]))
        yield
        lw_hi = lw.astype(BF16)
        lw_lo = (lw - lw_hi.astype(F32)).astype(BF16)
        cum = jnp.concatenate(
            [_dot(lmat, lw_hi[c0:c0 + CHUNK]) + _dot(lmat, lw_lo[c0:c0 + CHUNK])
             for c0 in range(0, len(chunks) * CHUNK, CHUNK)], axis=0)
        yield
        kkk = k * k_k[...]
        kk = kkk / jnp.maximum(jnp.sqrt(_head_sum(kkk * kkk)), 1e-12)
        kdir = k * (1.0 + (a - 1.0) * k_a[...])
        bb = kk * a
        yield
        res["at"] = (-kk * jnp.exp(cum - lw)).astype(BF16)
        res["rt"] = (r * jnp.exp(cum)).astype(BF16)
        yield
        e_neg = jnp.exp(-cum)
        res["bt"] = (bb * e_neg).astype(BF16)
        res["kt"] = (kdir * e_neg).astype(BF16)
        res["v"] = v_bf
        yield
        bp, kp = [], []
        for ci in sorted(chunks):
            local = slice(ci * CHUNK - lo, (ci + 1) * CHUNK - lo)
            end = local.start if reverse else local.stop - 1
            cum_end = cum[end:end + 1, :]
            e_end = jnp.exp(cum_end - cum[local])
            bp.append((bb[local] * e_end).astype(BF16))
            kp.append((kdir[local] * e_end).astype(BF16))
            res["pc", ci] = jnp.exp(cum_end)
        res["bp"] = jnp.concatenate(bp, axis=0)
        res["kp"] = jnp.concatenate(kp, axis=0)
        yield
        if final:
            a_other = _sigmoid(a0o[...] + _dot(ad, a2o[...]))
            ksum = k * (2.0 + (a + a_other - 2.0) * k_a[...])
            res["bonus"] = _head_sum(r * ksum * r_k[...]) * v_bf.astype(F32)
            yield
            gd = zl_ref[rows, 2 * DECAY_LORA + 2 * AAA_LORA:].astype(F32)
            res["gate"] = _dot(_sigmoid(gd).astype(BF16), g2[...])
            yield

    trow = lax.broadcasted_iota(jnp.int32, (CHUNK, GROUP_W), 0)
    scol = lax.broadcasted_iota(jnp.int32, (CHUNK, GROUP_W), 1) % CHUNK
    strict = (scol > trow) if reverse else (scol < trow)
    incl = (scol >= trow) if reverse else (scol <= trow)
    eye = jnp.where(scol == trow, 1.0, 0.0).astype(F32)
    bdr = lax.broadcasted_iota(jnp.int32, (GROUP_H * CHUNK, GROUP_W), 0) // CHUNK
    bdc = lax.broadcasted_iota(jnp.int32, (GROUP_H * CHUNK, GROUP_W), 1) // RW_N
    bdmask = bdr == bdc
    bd = functools.partial(_block_diag, mask=bdmask)

    def solve(chunks, tok, res):
        lo = min(chunks) * CHUNK
        pairs = [(ci, g) for ci in chunks for g in range(N_GROUPS)]

        def ld(name):
            return {(ci, g): tok[name][ci * CHUNK - lo:(ci + 1) * CHUNK - lo, g * GROUP_W:(g + 1) * GROUP_W]
                    for ci, g in pairs}

        def stage(fn):
            return {x: fn(x) for x in pairs}

        a_t, r_t, v_c, b_t, k_t = ld("at"), ld("rt"), ld("v"), ld("bt"), ld("kt")
        ar = stage(lambda x: jnp.concatenate([a_t[x], r_t[x]], axis=0))
        g1 = stage(lambda x: _dot_nt(ar[x], bd(b_t[x])))
        yield
        g2_ = stage(lambda x: _dot_nt(ar[x], bd(k_t[x])))
        yield
        ab = stage(lambda x: jnp.where(strict, g1[x][:CHUNK], 0.0))
        rb = stage(lambda x: jnp.where(incl, g1[x][CHUNK:], 0.0))
        ak = stage(lambda x: jnp.where(strict, g2_[x][:CHUNK], 0.0))
        rk = stage(lambda x: jnp.where(incl, g2_[x][CHUNK:], 0.0))

        t_inv = stage(lambda x: eye + ab[x])
        pw = stage(lambda x: _dot(ab[x].astype(BF16), bd(ab[x])))
        yield
        n_lvl = int(math.log2(CHUNK))
        for lvl in range(1, n_lvl):
            if lvl < n_lvl - 1:
                both = stage(lambda x: _dot(jnp.concatenate([pw[x], t_inv[x]], axis=0).astype(BF16),
                                            bd(pw[x])))
                pw = stage(lambda x: both[x][:CHUNK])
                t_inv = stage(lambda x, t=t_inv: t[x] + both[x][CHUNK:])
            else:
                t_inv = stage(lambda x, t=t_inv: t[x] + _dot(t[x].astype(BF16), bd(pw[x])))
            yield

        res["wv"] = stage(lambda x: _dot(ak[x].astype(BF16), bd(v_c[x])))
        b_p, k_p = ld("bp"), ld("kp")
        res["ar"] = ar
        res["t"] = stage(lambda x: t_inv[x].astype(BF16))
        res["rbk"] = stage(lambda x: jnp.concatenate([rb[x], rk[x]], axis=1).astype(BF16))
        res["v"] = v_c
        res["bkp"] = stage(lambda x: jnp.concatenate([b_p[x], k_p[x]], axis=0))
        yield

    def carry(chunks, tok, res, state):
        groups = range(N_GROUPS)
        for ci in chunks:
            a_s = [_dot_nt(res["ar"][ci, g], state[g].astype(BF16)) for g in groups]
            yield
            u = [_dot(res["t"][ci, g], bd(a_s[g][:CHUNK] + res["wv"][ci, g])) for g in groups]
            yield
            for g in groups:
                x = (ci, g)
                lanes = slice(g * GROUP_W, (g + 1) * GROUP_W)
                y_s[ci * CHUNK:(ci + 1) * CHUNK, lanes] = a_s[g][CHUNK:] + _dot(
                    res["rbk"][x], jnp.concatenate([bd(u[g]), bd(res["v"][x])], axis=0))
                upd = _dot_tn(jnp.concatenate([u[g].astype(BF16), res["v"][x]], axis=0), res["bkp"][x])
                state[g] = state[g] * tok["pc", ci][:, lanes] + jnp.where(bdmask, upd, 0.0)
            yield

    def run(gen, *fill):
        for _ in gen:
            for f in fill:
                next(f, None)
        for f in fill:
            for _ in f:
                pass

    n_ch = RW_TILE // CHUNK
    scan = [n_ch - 1 - s if reverse else s for s in range(n_ch)]
    per = n_ch // RW_PARTS
    parts = [scan[p * per:(p + 1) * per] for p in range(RW_PARTS)]
    tok = [{} for _ in parts]
    sol = [{} for _ in parts]
    state = [s_ref[g] for g in range(N_GROUPS)]
    run(prepare(parts[0], tok[0]))
    for p in range(RW_PARTS):
        fill = []
        if p > 0:
            fill.append(carry(parts[p - 1], tok[p - 1], sol[p - 1], state))
        if p + 1 < RW_PARTS:
            fill.append(prepare(parts[p + 1], tok[p + 1]))
        run(solve(parts[p], tok[p], sol[p]), *fill)
    run(carry(parts[-1], tok[-1], sol[-1], state))
    for g in range(N_GROUPS):
        s_ref[g] = state[g]

    y = y_s[...]
    if not final:
        out_ref[...] = y
    else:
        in_rows = sorted(range(RW_PARTS), key=lambda p: min(parts[p]))
        bonus = jnp.concatenate([tok[p]["bonus"] for p in in_rows], axis=0)
        gate = jnp.concatenate([tok[p]["gate"] for p in in_rows], axis=0)
        ytot = y + y0[...]
        inv_n = 1.0 / RW_N
        mu = _head_sum(ytot) * inv_n
        d = ytot - mu
        var = _head_sum(d * d) * inv_n
        yn = d * lax.rsqrt(var + GN_EPS) * ln_w[...] + ln_b[...]
        out_ref[...] = ((yn + bonus) * gate).astype(BF16)


def _rwkv_pass(reverse, rkv, zl, consts, extra, b, t):
    n = rkv.shape[0]
    tb = RW_TILE
    nj = t // tb
    final = reverse

    def cur_map(i, j):
        return (i * nj + (nj - 1 - j if reverse else j), 0)

    in_specs = ([pl.BlockSpec((tb, 3 * RW_W), cur_map), pl.BlockSpec((tb, RW_LORA_W), cur_map)]
                + [_const_spec(c.shape) for c in consts])
    args = [rkv, zl] + list(consts)
    if final:
        y0 = extra[0]
        in_specs += [pl.BlockSpec((tb, RW_W), cur_map)] + [_const_spec(c.shape) for c in extra[1:]]
        args += list(extra)
    out_dtype = BF16 if final else F32
    return pl.pallas_call(
        functools.partial(_rwkv_kernel, reverse, final),
        grid=(b, nj),
        in_specs=in_specs,
        out_specs=pl.BlockSpec((tb, RW_W), cur_map),
        out_shape=jax.ShapeDtypeStruct((n, RW_W), out_dtype),
        scratch_shapes=[pltpu.VMEM((N_GROUPS, GROUP_W, GROUP_W), F32),
                        pltpu.VMEM((tb, RW_W), F32)],
        compiler_params=_params(2),
        name="rwkv_bwd" if reverse else "rwkv_fwd",
    )(*args)


def _merge_kernel(x_ref, oa_ref, ob_ref, gpre_ref, wg_ref, wa_ref, wb_ref, wo_ref, gpost_ref, h_ref):
    x = x_ref[...]
    u = _rms(x, gpre_ref[...]).astype(BF16)
    gates = _sigmoid(_dot(u, wg_ref[...]))
    merged = (gates[:, :D_MODEL] * _dot(oa_ref[...], wa_ref[...])
              + gates[:, D_MODEL:] * _dot(ob_ref[...], wb_ref[...]))
    m = _dot(merged.astype(BF16), wo_ref[...])
    h_ref[...] = x + _rms(m, gpost_ref[...])


def _merge(x, oa, ob, gpre, wg, wa, wb, wo, gpost):
    n = x.shape[0]
    tm = MERGE_TILE
    row = lambda w: pl.BlockSpec((tm, w), lambda i: (i, 0))
    consts = [gpre, wg, wa, wb, wo, gpost]
    return pl.pallas_call(
        _merge_kernel,
        grid=(n // tm,),
        in_specs=[row(D_MODEL), row(ATT_W), row(RW_W)] + [_const_spec(c.shape) for c in consts],
        out_specs=row(D_MODEL),
        out_shape=jax.ShapeDtypeStruct((n, D_MODEL), F32),
        compiler_params=_params(1),
        name="merge",
    )(x, oa, ob, *consts)


def _gelu_tanh(x):
    return 0.5 * x * (1.0 + jnp.tanh(math.sqrt(2.0 / math.pi) * (x + 0.044715 * (x * x * x))))


def _ffn_kernel(hc_ref, hp_ref, hn_ref, gpre_ref, wup_ref, cw_ref, cb_ref, wdn_ref, gpost_ref, o_ref,
                slab_s):
    j = pl.program_id(1)
    nj = pl.num_programs(1)
    tm = FFN_TILE
    sub = FFN_HALO
    nb = tm // FFN_SPAN
    lane_w = 128
    n_slab = D_MODEL // lane_w

    for c in range(n_slab):
        slab_s[c] = hc_ref[:, c * lane_w:(c + 1) * lane_w]
    hc = jnp.concatenate(
        [jnp.concatenate([slab_s[c, pl.ds(blk * FFN_SPAN + v, sub, stride=sub), :] for c in range(n_slab)],
                         axis=1)
         for blk in range(nb) for v in range(sub)], axis=0)

    sid = lax.broadcasted_iota(jnp.int32, (sub, 1), 0)
    is_before = jnp.logical_and(sid == 0, j > 0)
    is_after = jnp.logical_and(sid == sub - 1, j < nj - 1)
    edge = jnp.where(is_before, hp_ref[sub - 1:sub, :], 0.0) + jnp.where(is_after, hn_ref[0:1, :], 0.0)
    u_edge = jnp.where(jnp.logical_or(is_before, is_after), _rms(edge, gpre_ref[...]), 0.0)
    u = jnp.concatenate([_rms(hc, gpre_ref[...]), u_edge], axis=0).astype(BF16)

    def up(c0):
        return [_dot(u, wup_ref[:, off:off + FFN_COLS]) for off in (c0, D_FF + c0)]

    def conv(hh, off):
        cols = slice(off, off + FFN_COLS)
        grp = lambda blk, v: hh[(blk * sub + v) * sub:(blk * sub + v + 1) * sub]
        edge_h = hh[tm:]
        down = [pltpu.roll(grp(blk, sub - 1), 1, 0) for blk in range(nb)]
        upw = [pltpu.roll(grp(blk, 0), sub - 1, 0) for blk in range(nb)]
        prev, nxt = [], []
        for blk in range(nb):
            first = jnp.where(sid == 0, down[blk - 1] if blk > 0 else edge_h, down[blk])
            last = jnp.where(sid == sub - 1, upw[blk + 1] if blk + 1 < nb else edge_h, upw[blk])
            lo = blk * FFN_SPAN
            prev += [first, hh[lo:lo + FFN_SPAN - sub]]
            nxt += [hh[lo + sub:lo + FFN_SPAN], last]
        prev = jnp.concatenate(prev, axis=0)
        nxt = jnp.concatenate(nxt, axis=0)
        return (prev * cw_ref[0:1, cols] + hh[:tm] * cw_ref[1:2, cols]
                + nxt * cw_ref[2:3, cols] + cb_ref[:, cols])

    starts = list(range(0, D_FF, FFN_COLS))
    pending = up(starts[0])
    acts = []
    for i, c0 in enumerate(starts):
        hh = pending
        if i + 1 < len(starts):
            pending = up(starts[i + 1])
        acts.append((_gelu_tanh(conv(hh[0], c0)) * conv(hh[1], D_FF + c0)).astype(BF16))
    f = _dot(jnp.concatenate(acts, axis=1), wdn_ref[...])
    out = hc + _rms(f, gpost_ref[...])
    for r in range(tm // sub):
        for c in range(n_slab):
            slab_s[c, pl.ds((r // sub) * FFN_SPAN + r % sub, sub, stride=sub), :] = (
                out[r * sub:(r + 1) * sub, c * lane_w:(c + 1) * lane_w])
    for c in range(n_slab):
        o_ref[:, c * lane_w:(c + 1) * lane_w] = slab_s[c]


def _ffn(h, gpre, wup, cw, cb, wdn, gpost, b, t):
    n = h.shape[0]
    tm = FFN_TILE
    nj = t // tm
    per = tm // FFN_HALO
    nhalo = n // FFN_HALO

    def prev_map(i, j):
        return (jnp.maximum((i * nj + j) * per - 1, 0), 0)

    def next_map(i, j):
        return (jnp.minimum((i * nj + j + 1) * per, nhalo - 1), 0)

    consts = [gpre, wup, cw, cb, wdn, gpost]
    return pl.pallas_call(
        _ffn_kernel,
        grid=(b, nj),
        in_specs=[pl.BlockSpec((tm, D_MODEL), lambda i, j: (i * nj + j, 0)),
                  pl.BlockSpec((FFN_HALO, D_MODEL), prev_map),
                  pl.BlockSpec((FFN_HALO, D_MODEL), next_map)]
                 + [_const_spec(c.shape) for c in consts],
        out_specs=pl.BlockSpec((tm, D_MODEL), lambda i, j: (i * nj + j, 0)),
        out_shape=jax.ShapeDtypeStruct((n, D_MODEL), F32),
        scratch_shapes=[pltpu.VMEM((D_MODEL // 128, tm, 128), F32)],
        compiler_params=_params(2),
        name="ffn",
    )(h, h, h, *consts)


def _prepare(norm_mix_pre, norm_mix_post, norm_ffn_pre, norm_ffn_post, w_in, attn_sink,
             rw_mu_prev, rw_mu_next, rw_w0, rw_w2, rw_a0, rw_a2, rw_g2, rw_k_k, rw_k_a,
             rw_r_k, rw_ln_w, rw_ln_b, w_branch_attn, w_branch_rwkv, w_out,
             w_ffn_up, ffn_conv_w, ffn_conv_b, w_ffn_down):
    c_q = ATT_W
    c_kv = c_q + 2 * ATT_KV_W
    c_rkv = c_kv + 3 * RW_W
    c_zl = c_rkv + RW_LORA_W
    row = lambda p: p.reshape(1, -1).astype(F32)

    def lora_pad(w2, d, n_lora):
        z = jnp.zeros((2 * n_lora, RW_W), F32)
        return z.at[d * n_lora:(d + 1) * n_lora].set(w2[d]).astype(BF16)

    mu = jnp.stack([rw_mu_prev, rw_mu_next]).astype(F32)
    p = dict(
        g_mix_pre=row(norm_mix_pre), g_mix_post=row(norm_mix_post),
        g_ffn_pre=row(norm_ffn_pre), g_ffn_post=row(norm_ffn_post),
        wq=w_in[:, :c_q].astype(BF16), wkv=w_in[:, c_q:c_kv].astype(BF16),
        wrkv=w_in[:, c_kv:c_rkv].astype(BF16), wzl=w_in[:, c_rkv:c_zl].astype(BF16),
        wg=w_in[:, c_zl:].astype(BF16),
        sink=attn_sink.astype(F32), bias=_attn_bias(),
        mu_rkv=mu[:, :3 * RW_W], mu_zl=mu[:, 3 * RW_W:],
        w0=[row(rw_w0[d]) for d in range(2)],
        w2=[lora_pad(rw_w2, d, DECAY_LORA) for d in range(2)],
        a0=[row(rw_a0[d]) for d in range(2)],
        a2=[lora_pad(rw_a2, d, AAA_LORA) for d in range(2)],
        g2=rw_g2.astype(BF16), k_k=row(rw_k_k), k_a=row(rw_k_a), r_k=row(rw_r_k),
        ln_w=row(rw_ln_w), ln_b=row(rw_ln_b),
        wa=w_branch_attn.astype(BF16), wb=w_branch_rwkv.astype(BF16), wo=w_out.astype(BF16),
        wup=w_ffn_up.astype(BF16), cw=ffn_conv_w.astype(F32), cb=row(ffn_conv_b),
        wdn=w_ffn_down.astype(BF16),
    )
    return p


def _layer(x, p):
    b, t, _ = x.shape
    assert t % RW_TILE == 0 and t % ATT_TILE == 0 and t % FFN_TILE == 0
    assert (b * t) % PROJ_TILE == 0 and (b * t) % MERGE_TILE == 0
    x2 = x.reshape(b * t, D_MODEL)
    q, kv, rkv, zl = _in_proj(x2, p["g_mix_pre"], p["wq"], p["wkv"], p["wrkv"], p["wzl"],
                              p["mu_rkv"], p["mu_zl"], b, t)
    o_attn = _attention(q, kv, p["sink"], p["bias"], b, t)

    def consts(d):
        return [p["w0"][d], p["w2"][d], p["a0"][d], p["a2"][d],
                p["k_k"], p["k_a"]]

    y_fwd = _rwkv_pass(False, rkv, zl, consts(0), None, b, t)
    o_rwkv = _rwkv_pass(True, rkv, zl, consts(1),
                        [y_fwd, p["a0"][0], p["a2"][0], p["g2"], p["r_k"], p["ln_w"], p["ln_b"]], b, t)
    h = _merge(x2, o_attn, o_rwkv, p["g_mix_pre"], p["wg"], p["wa"], p["wb"], p["wo"], p["g_mix_post"])
    out = _ffn(h, p["g_ffn_pre"], p["wup"], p["cw"], p["cb"], p["wdn"], p["g_ffn_post"], b, t)
    return out.reshape(b, t, D_MODEL)


def kernel(x_prompt, x_sample, norm_mix_pre, norm_mix_post, norm_ffn_pre, norm_ffn_post, w_in, attn_sink, rw_mu_prev, rw_mu_next, rw_w0, rw_w2, rw_a0, rw_a2, rw_g2, rw_k_k, rw_k_a, rw_r_k, rw_ln_w, rw_ln_b, w_branch_attn, w_branch_rwkv, w_out, w_ffn_up, ffn_conv_w, ffn_conv_b, w_ffn_down):
    weights = (norm_mix_pre, norm_mix_post, norm_ffn_pre, norm_ffn_post, w_in, attn_sink,
               rw_mu_prev, rw_mu_next, rw_w0, rw_w2, rw_a0, rw_a2, rw_g2, rw_k_k, rw_k_a,
               rw_r_k, rw_ln_w, rw_ln_b, w_branch_attn, w_branch_rwkv, w_out,
               w_ffn_up, ffn_conv_w, ffn_conv_b, w_ffn_down)
    depth = w_in.shape[0]
    layers = [_prepare(*(w[l] for w in weights)) for l in range(depth)]

    def run(x):
        for p in layers:
            x = _layer(x, p)
        return x

    return (run(x_prompt), run(x_sample))
```

```python
import functools
import math

import numpy as np
import jax
import jax.numpy as jnp
from jax import lax
from jax.experimental import pallas as pl
from jax.experimental.pallas import tpu as pltpu

F32 = jnp.float32
BF16 = jnp.bfloat16

D_MODEL = 1024
ATT_H = 8
ATT_KV = 2
ATT_G = ATT_H // ATT_KV
ATT_HD = 64
ATT_W = ATT_H * ATT_HD
ATT_KV_W = ATT_KV * ATT_HD
WINDOW = 128
BLK = 128
ATT_SCALE = 1.0 / math.sqrt(ATT_HD)
RW_H = 8
RW_N = 64
RW_W = RW_H * RW_N
DECAY_LORA = 64
AAA_LORA = 64
GATE_LORA = 160
RW_LORA_W = 2 * DECAY_LORA + 2 * AAA_LORA + GATE_LORA
RW_MIX_W = 3 * RW_W + RW_LORA_W
GATE_W = 2 * D_MODEL
D_FF = 2816
NORM_EPS = 1e-6
GN_EPS = 64e-5

VMEM_LIMIT_BYTES = 56 * 2**20

CHUNK = 64
GROUP_H = 4
GROUP_W = GROUP_H * RW_N
N_GROUPS = RW_H // GROUP_H
RW_TILE = 1024
RW_PARTS = 4

PROJ_TILE = 1024
PROJ_HALO = 8
ATT_TILE = 1024
MERGE_TILE = 1024
FFN_TILE = 512
FFN_HALO = 8
FFN_SPAN = FFN_HALO * FFN_HALO
FFN_COLS = 256


def _dot(a, b):
    return jnp.dot(a, b, preferred_element_type=F32)


def _dot_nt(a, b):
    return lax.dot_general(a, b, (((1,), (1,)), ((), ())), preferred_element_type=F32)


def _dot_tn(a, b):
    return lax.dot_general(a, b, (((0,), (0,)), ((), ())), preferred_element_type=F32)


def _sigmoid(x):
    return 1.0 / (1.0 + jnp.exp(-x))


def _rms(x, g):
    return x * lax.rsqrt(jnp.mean(x * x, axis=-1, keepdims=True) + NORM_EPS) * g


def _const_spec(shape):
    nd = len(shape)
    return pl.BlockSpec(shape, lambda *_: (0,) * nd, pipeline_mode=pl.Buffered(1))


def _params(n_axes):
    return pltpu.CompilerParams(dimension_semantics=("arbitrary",) * n_axes,
                                vmem_limit_bytes=VMEM_LIMIT_BYTES)


def _in_proj_kernel(xc_ref, xp_ref, xn_ref, g_ref, wq_ref, wkv_ref, wrkv_ref, wzl_ref, mu_rkv_ref, mu_zl_ref,
                    q_ref, kv_ref, rkv_ref, zl_ref):
    j = pl.program_id(1)
    nj = pl.num_programs(1)
    tm = PROJ_TILE
    ext = tm + 2 * PROJ_HALO
    x = jnp.concatenate([xp_ref[...], xc_ref[...], xn_ref[...]], axis=0)
    rid = lax.broadcasted_iota(jnp.int32, (ext, 1), 0)
    keep = jnp.logical_and(jnp.logical_or(rid >= PROJ_HALO, j > 0),
                           jnp.logical_or(rid < PROJ_HALO + tm, j < nj - 1))
    u = jnp.where(keep, _rms(x, g_ref[...]), 0.0).astype(BF16)
    um = u[PROJ_HALO:PROJ_HALO + tm]
    q_ref[...] = (_dot(um, wq_ref[...]) * ATT_SCALE).astype(BF16)
    kv_ref[...] = _dot(um, wkv_ref[...]).astype(BF16)

    def shifted(w_ref, mu_ref, o_ref, cols):
        z = _dot(u, w_ref[:, cols])
        c = z[PROJ_HALO:PROJ_HALO + tm]
        up = pltpu.roll(z, 1, 0)[PROJ_HALO:PROJ_HALO + tm]
        dn = pltpu.roll(z, ext - 1, 0)[PROJ_HALO:PROJ_HALO + tm]
        o_ref[:, cols] = (c + mu_ref[0:1, cols] * (up - c) + mu_ref[1:2, cols] * (dn - c)).astype(BF16)

    for c0 in range(0, 3 * RW_W, RW_W):
        shifted(wrkv_ref, mu_rkv_ref, rkv_ref, slice(c0, c0 + RW_W))
    shifted(wzl_ref, mu_zl_ref, zl_ref, slice(0, RW_LORA_W))


def _in_proj(x, g, wq, wkv, wrkv, wzl, mu_rkv, mu_zl, b, t):
    n = x.shape[0]
    tm = PROJ_TILE
    nj = t // tm
    per = tm // PROJ_HALO
    nhalo = n // PROJ_HALO

    def prev_map(i, j):
        return (jnp.maximum((i * nj + j) * per - 1, 0), 0)

    def next_map(i, j):
        return (jnp.minimum((i * nj + j + 1) * per, nhalo - 1), 0)

    row = lambda w: pl.BlockSpec((tm, w), lambda i, j: (i * nj + j, 0))
    consts = [g, wq, wkv, wrkv, wzl, mu_rkv, mu_zl]
    return pl.pallas_call(
        _in_proj_kernel,
        grid=(b, nj),
        in_specs=[row(D_MODEL), pl.BlockSpec((PROJ_HALO, D_MODEL), prev_map),
                  pl.BlockSpec((PROJ_HALO, D_MODEL), next_map)] + [_const_spec(c.shape) for c in consts],
        out_specs=[row(ATT_W), row(2 * ATT_KV_W), row(3 * RW_W), row(RW_LORA_W)],
        out_shape=[jax.ShapeDtypeStruct((n, ATT_W), BF16),
                   jax.ShapeDtypeStruct((n, 2 * ATT_KV_W), BF16),
                   jax.ShapeDtypeStruct((n, 3 * RW_W), BF16),
                   jax.ShapeDtypeStruct((n, RW_LORA_W), BF16)],
        compiler_params=_params(2),
        name="in_proj",
    )(x, x, x, *consts)


def _attn_bias():
    slopes = np.exp2(-8.0 / ATT_H * np.arange(1, ATT_H + 1, dtype=np.float64))
    dist = np.abs(np.arange(BLK)[:, None] - (np.arange(3 * BLK)[None, :] - BLK))
    bias = -(slopes[:, None, None] * dist[None])
    bias = np.where(dist[None] <= WINDOW, bias, -np.inf)
    bias = bias.reshape(ATT_KV, ATT_G * BLK, 3 * BLK).transpose(0, 2, 1)
    return jnp.asarray(bias, F32)


def _attn_kernel(sink_ref, q_ref, kvp_ref, kvc_ref, kvn_ref, bias_ref, o_ref):
    j = pl.program_id(1)
    nj = pl.num_programs(1)
    has_next = jnp.where(j < nj - 1, 1.0, 0.0).astype(F32)
    kvwin = jnp.concatenate([kvp_ref[...].astype(F32), kvc_ref[...].astype(F32),
                             kvn_ref[...].astype(F32) * has_next], axis=0)
    kwin = kvwin[:, :ATT_KV_W].astype(BF16)
    v_t = kvwin[:, ATT_KV_W:].T.astype(BF16)
    krow = lax.broadcasted_iota(jnp.int32, (3 * BLK, 1), 0)
    n_qb = ATT_TILE // BLK
    gw = ATT_G * ATT_HD
    no_q = jnp.zeros((ATT_HD, ATT_G * BLK), F32)

    pairs = [(qb, kv) for qb in range(n_qb) for kv in range(ATT_KV)]
    every = range(len(pairs))

    def scores(x):
        qb, kv = pairs[x]
        q_t = q_ref[qb * BLK:(qb + 1) * BLK, kv * gw:(kv + 1) * gw].astype(F32).T
        q_t = jnp.concatenate([q_t[g * ATT_HD:(g + 1) * ATT_HD] for g in range(ATT_G)], axis=1)
        q_t = jnp.concatenate([q_t, no_q] if kv == 0 else [no_q, q_t], axis=0).astype(BF16)
        return _dot(kwin[qb * BLK:qb * BLK + 3 * BLK], q_t)

    def softmax(x, s):
        qb, kv = pairs[x]
        key_ok = krow >= jnp.maximum(2 - (j * n_qb + qb), 0) * BLK
        s = jnp.where(key_ok, s + bias_ref[kv], -jnp.inf)
        sink = jnp.concatenate(
            [jnp.full((1, BLK), sink_ref[kv * ATT_G + g], F32) for g in range(ATT_G)], axis=1)
        mx = jnp.maximum(jnp.max(s, axis=0, keepdims=True), sink)
        p = jnp.exp(s - mx)
        den = jnp.sum(p, axis=0, keepdims=True) + jnp.exp(sink - mx)
        return p.astype(BF16), 1.0 / den

    def values(x, p, inv_den):
        qb, kv = pairs[x]
        o_t = _dot(v_t[kv * ATT_HD:(kv + 1) * ATT_HD, qb * BLK:qb * BLK + 3 * BLK], p) * inv_den
        o_t = jnp.concatenate([o_t[:, g * BLK:(g + 1) * BLK] for g in range(ATT_G)], axis=0)
        o_ref[qb * BLK:(qb + 1) * BLK, kv * gw:(kv + 1) * gw] = o_t.T.astype(BF16)

    s_all = [scores(x) for x in every]
    p_all = [softmax(x, s_all[x]) for x in every]
    for x in every:
        values(x, *p_all[x])


def _attention(q, kv, sink, bias, b, t):
    n = q.shape[0]
    tq = ATT_TILE
    nj = t // tq
    per = tq // BLK
    nblk = n // BLK

    def prev_map(i, j):
        return (jnp.maximum((i * nj + j) * per - 1, 0), 0)

    def next_map(i, j):
        return (jnp.minimum((i * nj + j + 1) * per, nblk - 1), 0)

    return pl.pallas_call(
        _attn_kernel,
        grid=(b, nj),
        in_specs=[pl.BlockSpec(memory_space=pltpu.SMEM),
                  pl.BlockSpec((tq, ATT_W), lambda i, j: (i * nj + j, 0)),
                  pl.BlockSpec((BLK, 2 * ATT_KV_W), prev_map),
                  pl.BlockSpec((tq, 2 * ATT_KV_W), lambda i, j: (i * nj + j, 0)),
                  pl.BlockSpec((BLK, 2 * ATT_KV_W), next_map),
                  _const_spec(bias.shape)],
        out_specs=pl.BlockSpec((tq, ATT_W), lambda i, j: (i * nj + j, 0)),
        out_shape=jax.ShapeDtypeStruct((n, ATT_W), BF16),
        compiler_params=_params(2),
        name="attention",
    )(sink, q, kv, kv, kv, bias)


def _block_diag(x, mask):
    tiled = jnp.concatenate([x.astype(F32)] * GROUP_H, axis=0)
    return jnp.where(mask, tiled, 0.0).astype(BF16)


def _head_sum(x):
    lanes = 2 * RW_N
    low = lax.broadcasted_iota(jnp.int32, (x.shape[0], lanes), 1) < RW_N
    out = []
    for c0 in range(0, RW_W, lanes):
        xc = x[:, c0:c0 + lanes]
        lo = jnp.sum(jnp.where(low, xc, 0.0), axis=1, keepdims=True)
        hi = jnp.sum(jnp.where(low, 0.0, xc), axis=1, keepdims=True)
        out.append(jnp.where(low, lo, hi))
    return jnp.concatenate(out, axis=1)


def _rwkv_kernel(reverse, final, *refs):
    if final:
        (rkv_ref, zl_ref, w0, w2, a0, a2, k_k, k_a,
         y0, a0o, a2o, g2, r_k, ln_w, ln_b, out_ref, s_ref, y_s) = refs
    else:
        (rkv_ref, zl_ref, w0, w2, a0, a2, k_k, k_a, out_ref, s_ref, y_s) = refs

    j = pl.program_id(1)

    @pl.when(j == 0)
    def _():
        s_ref[...] = jnp.zeros_like(s_ref)

    rid = lax.broadcasted_iota(jnp.int32, (CHUNK, CHUNK), 0)
    cid = lax.broadcasted_iota(jnp.int32, (CHUNK, CHUNK), 1)
    lmat = jnp.where((cid >= rid) if reverse else (cid <= rid), 1.0, 0.0).astype(BF16)

    def prepare(chunks, res):
        lo = min(chunks) * CHUNK
        rows = slice(lo, (max(chunks) + 1) * CHUNK)
        r = rkv_ref[rows, :RW_W].astype(F32)
        k = rkv_ref[rows, RW_W:2 * RW_W].astype(F32)
        v_bf = rkv_ref[rows, 2 * RW_W:]
        wd = zl_ref[rows, :2 * DECAY_LORA].astype(F32)
        ad = zl_ref[rows, 2 * DECAY_LORA:2 * DECAY_LORA + 2 * AAA_LORA]
        w = w0[...] + _dot(jnp.tanh(wd).astype(BF16), w2[...])
        lw = -math.exp(-0.5) * _sigmoid(w)
        a = _sigmoid(a0[...] + _dot(ad, a2[...]))
        yield
        lw_hi = lw.astype(BF16)
        lw_lo = (lw - lw_hi.astype(F32)).astype(BF16)
        cum = jnp.concatenate(
            [_dot(lmat, lw_hi[c0:c0 + CHUNK]) + _dot(lmat, lw_lo[c0:c0 + CHUNK])
             for c0 in range(0, len(chunks) * CHUNK, CHUNK)], axis=0)
        yield
        kkk = k * k_k[...]
        kk = kkk / jnp.maximum(jnp.sqrt(_head_sum(kkk * kkk)), 1e-12)
        kdir = k * (1.0 + (a - 1.0) * k_a[...])
        bb = kk * a
        yield
        res["at"] = (-kk * jnp.exp(cum - lw)).astype(BF16)
        res["rt"] = (r * jnp.exp(cum)).astype(BF16)
        yield
        e_neg = jnp.exp(-cum)
        res["bt"] = (bb * e_neg).astype(BF16)
        res["kt"] = (kdir * e_neg).astype(BF16)
        res["v"] = v_bf
        yield
        bp, kp = [], []
        for ci in sorted(chunks):
            local = slice(ci * CHUNK - lo, (ci + 1) * CHUNK - lo)
            end = local.start if reverse else local.stop - 1
            cum_end = cum[end:end + 1, :]
            e_end = jnp.exp(cum_end - cum[local])
            bp.append((bb[local] * e_end).astype(BF16))
            kp.append((kdir[local] * e_end).astype(BF16))
            res["pc", ci] = jnp.exp(cum_end)
        res["bp"] = jnp.concatenate(bp, axis=0)
        res["kp"] = jnp.concatenate(kp, axis=0)
        yield
        if final:
            a_other = _sigmoid(a0o[...] + _dot(ad, a2o[...]))
            ksum = k * (2.0 + (a + a_other - 2.0) * k_a[...])
            res["bonus"] = _head_sum(r * ksum * r_k[...]) * v_bf.astype(F32)
            yield
            gd = zl_ref[rows, 2 * DECAY_LORA + 2 * AAA_LORA:].astype(F32)
            res["gate"] = _dot(_sigmoid(gd).astype(BF16), g2[...])
            yield

    trow = lax.broadcasted_iota(jnp.int32, (CHUNK, GROUP_W), 0)
    scol = lax.broadcasted_iota(jnp.int32, (CHUNK, GROUP_W), 1) % CHUNK
    strict = (scol > trow) if reverse else (scol < trow)
    incl = (scol >= trow) if reverse else (scol <= trow)
    eye = jnp.where(scol == trow, 1.0, 0.0).astype(F32)
    bdr = lax.broadcasted_iota(jnp.int32, (GROUP_H * CHUNK, GROUP_W), 0) // CHUNK
    bdc = lax.broadcasted_iota(jnp.int32, (GROUP_H * CHUNK, GROUP_W), 1) // RW_N
    bdmask = bdr == bdc
    bd = functools.partial(_block_diag, mask=bdmask)

    def solve(chunks, tok, res):
        lo = min(chunks) * CHUNK
        pairs = [(ci, g) for ci in chunks for g in range(N_GROUPS)]

        def ld(name):
            return {(ci, g): tok[name][ci * CHUNK - lo:(ci + 1) * CHUNK - lo, g * GROUP_W:(g + 1) * GROUP_W]
                    for ci, g in pairs}

        def stage(fn):
            return {x: fn(x) for x in pairs}

        a_t, r_t, v_c, b_t, k_t = ld("at"), ld("rt"), ld("v"), ld("bt"), ld("kt")
        ar = stage(lambda x: jnp.concatenate([a_t[x], r_t[x]], axis=0))
        g1 = stage(lambda x: _dot_nt(ar[x], bd(b_t[x])))
        yield
        g2_ = stage(lambda x: _dot_nt(ar[x], bd(k_t[x])))
        yield
        ab = stage(lambda x: jnp.where(strict, g1[x][:CHUNK], 0.0))
        rb = stage(lambda x: jnp.where(incl, g1[x][CHUNK:], 0.0))
        ak = stage(lambda x: jnp.where(strict, g2_[x][:CHUNK], 0.0))
        rk = stage(lambda x: jnp.where(incl, g2_[x][CHUNK:], 0.0))

        t_inv = stage(lambda x: eye + ab[x])
        pw = stage(lambda x: _dot(ab[x].astype(BF16), bd(ab[x])))
        yield
        n_lvl = int(math.log2(CHUNK))
        for lvl in range(1, n_lvl):
            if lvl < n_lvl - 1:
                both = stage(lambda x: _dot(jnp.concatenate([pw[x], t_inv[x]], axis=0).astype(BF16),
                                            bd(pw[x])))
                pw = stage(lambda x: both[x][:CHUNK])
                t_inv = stage(lambda x, t=t_inv: t[x] + both[x][CHUNK:])
            else:
                t_inv = stage(lambda x, t=t_inv: t[x] + _dot(t[x].astype(BF16), bd(pw[x])))
            yield

        res["wv"] = stage(lambda x: _dot(ak[x].astype(BF16), bd(v_c[x])))
        b_p, k_p = ld("bp"), ld("kp")
        res["ar"] = ar
        res["t"] = stage(lambda x: t_inv[x].astype(BF16))
        res["rbk"] = stage(lambda x: jnp.concatenate([rb[x], rk[x]], axis=1).astype(BF16))
        res["v"] = v_c
        res["bkp"] = stage(lambda x: jnp.concatenate([b_p[x], k_p[x]], axis=0))
        yield

    def carry(chunks, tok, res, state):
        groups = range(N_GROUPS)
        for ci in chunks:
            a_s = [_dot_nt(res["ar"][ci, g], state[g].astype(BF16)) for g in groups]
            yield
            u = [_dot(res["t"][ci, g], bd(a_s[g][:CHUNK] + res["wv"][ci, g])) for g in groups]
            yield
            for g in groups:
                x = (ci, g)
                lanes = slice(g * GROUP_W, (g + 1) * GROUP_W)
                y_s[ci * CHUNK:(ci + 1) * CHUNK, lanes] = a_s[g][CHUNK:] + _dot(
                    res["rbk"][x], jnp.concatenate([bd(u[g]), bd(res["v"][x])], axis=0))
                upd = _dot_tn(jnp.concatenate([u[g].astype(BF16), res["v"][x]], axis=0), res["bkp"][x])
                state[g] = state[g] * tok["pc", ci][:, lanes] + jnp.where(bdmask, upd, 0.0)
            yield

    def run(gen, *fill):
        for _ in gen:
            for f in fill:
                next(f, None)
        for f in fill:
            for _ in f:
                pass

    n_ch = RW_TILE // CHUNK
    scan = [n_ch - 1 - s if reverse else s for s in range(n_ch)]
    per = n_ch // RW_PARTS
    parts = [scan[p * per:(p + 1) * per] for p in range(RW_PARTS)]
    tok = [{} for _ in parts]
    sol = [{} for _ in parts]
    def finish(p):
        rows = slice(min(parts[p]) * CHUNK, (max(parts[p]) + 1) * CHUNK)
        y = y_s[rows, :]
        if not final:
            out_ref[rows, :] = y
        else:
            ytot = y + y0[rows, :]
            inv_n = 1.0 / RW_N
            mu = _head_sum(ytot) * inv_n
            d = ytot - mu
            var = _head_sum(d * d) * inv_n
            yn = d * lax.rsqrt(var + GN_EPS) * ln_w[...] + ln_b[...]
            out_ref[rows, :] = ((yn + tok[p]["bonus"]) * tok[p]["gate"]).astype(BF16)
        yield

    state = [s_ref[g] for g in range(N_GROUPS)]
    run(prepare(parts[0], tok[0]))
    for p in range(RW_PARTS):
        fill = []
        if p > 0:
            fill.append(carry(parts[p - 1], tok[p - 1], sol[p - 1], state))
        if p + 1 < RW_PARTS:
            fill.append(prepare(parts[p + 1], tok[p + 1]))
        if p > 1:
            fill.append(finish(p - 2))
        run(solve(parts[p], tok[p], sol[p]), *fill)
    run(carry(parts[-1], tok[-1], sol[-1], state))
    for g in range(N_GROUPS):
        s_ref[g] = state[g]
    for p in range(max(RW_PARTS - 2, 0), RW_PARTS):
        run(finish(p))


def _rwkv_pass(reverse, rkv, zl, consts, extra, b, t):
    n = rkv.shape[0]
    tb = RW_TILE
    nj = t // tb
    final = reverse

    def cur_map(i, j):
        return (i * nj + (nj - 1 - j if reverse else j), 0)

    in_specs = ([pl.BlockSpec((tb, 3 * RW_W), cur_map), pl.BlockSpec((tb, RW_LORA_W), cur_map)]
                + [_const_spec(c.shape) for c in consts])
    args = [rkv, zl] + list(consts)
    if final:
        y0 = extra[0]
        in_specs += [pl.BlockSpec((tb, RW_W), cur_map)] + [_const_spec(c.shape) for c in extra[1:]]
        args += list(extra)
    out_dtype = BF16 if final else F32
    return pl.pallas_call(
        functools.partial(_rwkv_kernel, reverse, final),
        grid=(b, nj),
        in_specs=in_specs,
        out_specs=pl.BlockSpec((tb, RW_W), cur_map),
        out_shape=jax.ShapeDtypeStruct((n, RW_W), out_dtype),
        scratch_shapes=[pltpu.VMEM((N_GROUPS, GROUP_W, GROUP_W), F32),
                        pltpu.VMEM((tb, RW_W), F32)],
        compiler_params=_params(2),
        name="rwkv_bwd" if reverse else "rwkv_fwd",
    )(*args)


def _merge_kernel(x_ref, oa_ref, ob_ref, gpre_ref, wg_ref, wa_ref, wb_ref, wo_ref, gpost_ref, h_ref):
    x = x_ref[...]
    u = _rms(x, gpre_ref[...]).astype(BF16)
    gates = _sigmoid(_dot(u, wg_ref[...]))
    merged = (gates[:, :D_MODEL] * _dot(oa_ref[...], wa_ref[...])
              + gates[:, D_MODEL:] * _dot(ob_ref[...], wb_ref[...]))
    m = _dot(merged.astype(BF16), wo_ref[...])
    h_ref[...] = x + _rms(m, gpost_ref[...])


def _merge(x, oa, ob, gpre, wg, wa, wb, wo, gpost):
    n = x.shape[0]
    tm = MERGE_TILE
    row = lambda w: pl.BlockSpec((tm, w), lambda i: (i, 0))
    consts = [gpre, wg, wa, wb, wo, gpost]
    return pl.pallas_call(
        _merge_kernel,
        grid=(n // tm,),
        in_specs=[row(D_MODEL), row(ATT_W), row(RW_W)] + [_const_spec(c.shape) for c in consts],
        out_specs=row(D_MODEL),
        out_shape=jax.ShapeDtypeStruct((n, D_MODEL), F32),
        compiler_params=_params(1),
        name="merge",
    )(x, oa, ob, *consts)


def _gelu_tanh(x):
    return 0.5 * x * (1.0 + jnp.tanh(math.sqrt(2.0 / math.pi) * (x + 0.044715 * (x * x * x))))


def _ffn_kernel(hc_ref, hp_ref, hn_ref, gpre_ref, wup_ref, cw_ref, cb_ref, wdn_ref, gpost_ref, o_ref,
                slab_s):
    j = pl.program_id(1)
    nj = pl.num_programs(1)
    tm = FFN_TILE
    sub = FFN_HALO
    nb = tm // FFN_SPAN
    lane_w = 128
    n_slab = D_MODEL // lane_w

    for c in range(n_slab):
        slab_s[c] = hc_ref[:, c * lane_w:(c + 1) * lane_w]
    hc = jnp.concatenate(
        [jnp.concatenate([slab_s[c, pl.ds(blk * FFN_SPAN + v, sub, stride=sub), :] for c in range(n_slab)],
                         axis=1)
         for blk in range(nb) for v in range(sub)], axis=0)

    sid = lax.broadcasted_iota(jnp.int32, (sub, 1), 0)
    is_before = jnp.logical_and(sid == 0, j > 0)
    is_after = jnp.logical_and(sid == sub - 1, j < nj - 1)
    edge = jnp.where(is_before, hp_ref[sub - 1:sub, :], 0.0) + jnp.where(is_after, hn_ref[0:1, :], 0.0)
    u_edge = jnp.where(jnp.logical_or(is_before, is_after), _rms(edge, gpre_ref[...]), 0.0)
    u = jnp.concatenate([_rms(hc, gpre_ref[...]), u_edge], axis=0).astype(BF16)

    def up(c0):
        return [_dot(u, wup_ref[:, off:off + FFN_COLS]) for off in (c0, D_FF + c0)]

    def conv(hh, off):
        cols = slice(off, off + FFN_COLS)
        grp = lambda blk, v: hh[(blk * sub + v) * sub:(blk * sub + v + 1) * sub]
        edge_h = hh[tm:]
        down = [pltpu.roll(grp(blk, sub - 1), 1, 0) for blk in range(nb)]
        upw = [pltpu.roll(grp(blk, 0), sub - 1, 0) for blk in range(nb)]
        prev, nxt = [], []
        for blk in range(nb):
            first = jnp.where(sid == 0, down[blk - 1] if blk > 0 else edge_h, down[blk])
            last = jnp.where(sid == sub - 1, upw[blk + 1] if blk + 1 < nb else edge_h, upw[blk])
            lo = blk * FFN_SPAN
            prev += [first, hh[lo:lo + FFN_SPAN - sub]]
            nxt += [hh[lo + sub:lo + FFN_SPAN], last]
        prev = jnp.concatenate(prev, axis=0)
        nxt = jnp.concatenate(nxt, axis=0)
        return (prev * cw_ref[0:1, cols] + hh[:tm] * cw_ref[1:2, cols]
                + nxt * cw_ref[2:3, cols] + cb_ref[:, cols])

    starts = list(range(0, D_FF, FFN_COLS))
    pending = up(starts[0])
    acts = []
    for i, c0 in enumerate(starts):
        hh = pending
        if i + 1 < len(starts):
            pending = up(starts[i + 1])
        acts.append((_gelu_tanh(conv(hh[0], c0)) * conv(hh[1], D_FF + c0)).astype(BF16))
    f = _dot(jnp.concatenate(acts, axis=1), wdn_ref[...])
    out = hc + _rms(f, gpost_ref[...])
    for r in range(tm // sub):
        for c in range(n_slab):
            slab_s[c, pl.ds((r // sub) * FFN_SPAN + r % sub, sub, stride=sub), :] = (
                out[r * sub:(r + 1) * sub, c * lane_w:(c + 1) * lane_w])
    for c in range(n_slab):
        o_ref[:, c * lane_w:(c + 1) * lane_w] = slab_s[c]


def _ffn(h, gpre, wup, cw, cb, wdn, gpost, b, t):
    n = h.shape[0]
    tm = FFN_TILE
    nj = t // tm
    per = tm // FFN_HALO
    nhalo = n // FFN_HALO

    def prev_map(i, j):
        return (jnp.maximum((i * nj + j) * per - 1, 0), 0)

    def next_map(i, j):
        return (jnp.minimum((i * nj + j + 1) * per, nhalo - 1), 0)

    consts = [gpre, wup, cw, cb, wdn, gpost]
    return pl.pallas_call(
        _ffn_kernel,
        grid=(b, nj),
        in_specs=[pl.BlockSpec((tm, D_MODEL), lambda i, j: (i * nj + j, 0)),
                  pl.BlockSpec((FFN_HALO, D_MODEL), prev_map),
                  pl.BlockSpec((FFN_HALO, D_MODEL), next_map)]
                 + [_const_spec(c.shape) for c in consts],
        out_specs=pl.BlockSpec((tm, D_MODEL), lambda i, j: (i * nj + j, 0)),
        out_shape=jax.ShapeDtypeStruct((n, D_MODEL), F32),
        scratch_shapes=[pltpu.VMEM((D_MODEL // 128, tm, 128), F32)],
        compiler_params=_params(2),
        name="ffn",
    )(h, h, h, *consts)


def _prepare(norm_mix_pre, norm_mix_post, norm_ffn_pre, norm_ffn_post, w_in, attn_sink,
             rw_mu_prev, rw_mu_next, rw_w0, rw_w2, rw_a0, rw_a2, rw_g2, rw_k_k, rw_k_a,
             rw_r_k, rw_ln_w, rw_ln_b, w_branch_attn, w_branch_rwkv, w_out,
             w_ffn_up, ffn_conv_w, ffn_conv_b, w_ffn_down):
    c_q = ATT_W
    c_kv = c_q + 2 * ATT_KV_W
    c_rkv = c_kv + 3 * RW_W
    c_zl = c_rkv + RW_LORA_W
    row = lambda p: p.reshape(1, -1).astype(F32)

    def lora_pad(w2, d, n_lora):
        z = jnp.zeros((2 * n_lora, RW_W), F32)
        return z.at[d * n_lora:(d + 1) * n_lora].set(w2[d]).astype(BF16)

    mu = jnp.stack([rw_mu_prev, rw_mu_next]).astype(F32)
    p = dict(
        g_mix_pre=row(norm_mix_pre), g_mix_post=row(norm_mix_post),
        g_ffn_pre=row(norm_ffn_pre), g_ffn_post=row(norm_ffn_post),
        wq=w_in[:, :c_q].astype(BF16), wkv=w_in[:, c_q:c_kv].astype(BF16),
        wrkv=w_in[:, c_kv:c_rkv].astype(BF16), wzl=w_in[:, c_rkv:c_zl].astype(BF16),
        wg=w_in[:, c_zl:].astype(BF16),
        sink=attn_sink.astype(F32), bias=_attn_bias(),
        mu_rkv=mu[:, :3 * RW_W], mu_zl=mu[:, 3 * RW_W:],
        w0=[row(rw_w0[d]) for d in range(2)],
        w2=[lora_pad(rw_w2, d, DECAY_LORA) for d in range(2)],
        a0=[row(rw_a0[d]) for d in range(2)],
        a2=[lora_pad(rw_a2, d, AAA_LORA) for d in range(2)],
        g2=rw_g2.astype(BF16), k_k=row(rw_k_k), k_a=row(rw_k_a), r_k=row(rw_r_k),
        ln_w=row(rw_ln_w), ln_b=row(rw_ln_b),
        wa=w_branch_attn.astype(BF16), wb=w_branch_rwkv.astype(BF16), wo=w_out.astype(BF16),
        wup=w_ffn_up.astype(BF16), cw=ffn_conv_w.astype(F32), cb=row(ffn_conv_b),
        wdn=w_ffn_down.astype(BF16),
    )
    return p


def _layer(x, p):
    b, t, _ = x.shape
    assert t % RW_TILE == 0 and t % ATT_TILE == 0 and t % FFN_TILE == 0
    assert (b * t) % PROJ_TILE == 0 and (b * t) % MERGE_TILE == 0
    x2 = x.reshape(b * t, D_MODEL)
    q, kv, rkv, zl = _in_proj(x2, p["g_mix_pre"], p["wq"], p["wkv"], p["wrkv"], p["wzl"],
                              p["mu_rkv"], p["mu_zl"], b, t)
    o_attn = _attention(q, kv, p["sink"], p["bias"], b, t)

    def consts(d):
        return [p["w0"][d], p["w2"][d], p["a0"][d], p["a2"][d],
                p["k_k"], p["k_a"]]

    y_fwd = _rwkv_pass(False, rkv, zl, consts(0), None, b, t)
    o_rwkv = _rwkv_pass(True, rkv, zl, consts(1),
                        [y_fwd, p["a0"][0], p["a2"][0], p["g2"], p["r_k"], p["ln_w"], p["ln_b"]], b, t)
    h = _merge(x2, o_attn, o_rwkv, p["g_mix_pre"], p["wg"], p["wa"], p["wb"], p["wo"], p["g_mix_post"])
    out = _ffn(h, p["g_ffn_pre"], p["wup"], p["cw"], p["cb"], p["wdn"], p["g_ffn_post"], b, t)
    return out.reshape(b, t, D_MODEL)


def kernel(x_prompt, x_sample, norm_mix_pre, norm_mix_post, norm_ffn_pre, norm_ffn_post, w_in, attn_sink, rw_mu_prev, rw_mu_next, rw_w0, rw_w2, rw_a0, rw_a2, rw_g2, rw_k_k, rw_k_a, rw_r_k, rw_ln_w, rw_ln_b, w_branch_attn, w_branch_rwkv, w_out, w_ffn_up, ffn_conv_w, ffn_conv_b, w_ffn_down):
    weights = (norm_mix_pre, norm_mix_post, norm_ffn_pre, norm_ffn_post, w_in, attn_sink,
               rw_mu_prev, rw_mu_next, rw_w0, rw_w2, rw_a0, rw_a2, rw_g2, rw_k_k, rw_k_a,
               rw_r_k, rw_ln_w, rw_ln_b, w_branch_attn, w_branch_rwkv, w_out,
               w_ffn_up, ffn_conv_w, ffn_conv_b, w_ffn_down)
    depth = w_in.shape[0]
    layers = [_prepare(*(w[l] for w in weights)) for l in range(depth)]

    def run(x):
        for p in layers:
            x = _layer(x, p)
        return x

    return (run(x_prompt), run(x_sample))
```

```python
import functools
import math

import numpy as np
import jax
import jax.numpy as jnp
from jax import lax
from jax.experimental import pallas as pl
from jax.experimental.pallas import tpu as pltpu

F32 = jnp.float32
BF16 = jnp.bfloat16

D_MODEL = 1024
ATT_H = 8
ATT_KV = 2
ATT_G = ATT_H // ATT_KV
ATT_HD = 64
ATT_W = ATT_H * ATT_HD
ATT_KV_W = ATT_KV * ATT_HD
WINDOW = 128
BLK = 128
ATT_SCALE = 1.0 / math.sqrt(ATT_HD)
RW_H = 8
RW_N = 64
RW_W = RW_H * RW_N
DECAY_LORA = 64
AAA_LORA = 64
GATE_LORA = 160
RW_LORA_W = 2 * DECAY_LORA + 2 * AAA_LORA + GATE_LORA
RW_MIX_W = 3 * RW_W + RW_LORA_W
GATE_W = 2 * D_MODEL
D_FF = 2816
NORM_EPS = 1e-6
GN_EPS = 64e-5

VMEM_LIMIT_BYTES = 56 * 2**20

CHUNK = 64
GROUP_H = 4
GROUP_W = GROUP_H * RW_N
N_GROUPS = RW_H // GROUP_H
RW_TILE = 1024
RW_PARTS = 4

PROJ_TILE = 1024
PROJ_HALO = 8
ATT_TILE = 1024
MERGE_TILE = 1024
FFN_TILE = 512
FFN_HALO = 8
FFN_SPAN = FFN_HALO * FFN_HALO
FFN_COLS = 256


def _dot(a, b):
    return jnp.dot(a, b, preferred_element_type=F32)


def _dot_nt(a, b):
    return lax.dot_general(a, b, (((1,), (1,)), ((), ())), preferred_element_type=F32)


def _dot_tn(a, b):
    return lax.dot_general(a, b, (((0,), (0,)), ((), ())), preferred_element_type=F32)


def _sigmoid(x):
    return 1.0 / (1.0 + jnp.exp(-x))


def _rms(x, g):
    return x * lax.rsqrt(jnp.mean(x * x, axis=-1, keepdims=True) + NORM_EPS) * g


def _const_spec(shape):
    nd = len(shape)
    return pl.BlockSpec(shape, lambda *_: (0,) * nd, pipeline_mode=pl.Buffered(1))


def _params(n_axes):
    return pltpu.CompilerParams(dimension_semantics=("arbitrary",) * n_axes,
                                vmem_limit_bytes=VMEM_LIMIT_BYTES)


def _in_proj_kernel(xc_ref, xp_ref, xn_ref, g_ref, wq_ref, wkv_ref, wrkv_ref, wzl_ref, mu_rkv_ref, mu_zl_ref,
                    q_ref, kv_ref, z_ref):
    j = pl.program_id(1)
    nj = pl.num_programs(1)
    tm = PROJ_TILE
    ext = tm + 2 * PROJ_HALO
    x = jnp.concatenate([xp_ref[...], xc_ref[...], xn_ref[...]], axis=0)
    rid = lax.broadcasted_iota(jnp.int32, (ext, 1), 0)
    keep = jnp.logical_and(jnp.logical_or(rid >= PROJ_HALO, j > 0),
                           jnp.logical_or(rid < PROJ_HALO + tm, j < nj - 1))
    u = jnp.where(keep, _rms(x, g_ref[...]), 0.0).astype(BF16)
    um = u[PROJ_HALO:PROJ_HALO + tm]
    q_ref[...] = (_dot(um, wq_ref[...]) * ATT_SCALE).astype(BF16)
    kv_ref[...] = _dot(um, wkv_ref[...]).astype(BF16)

    def shifted(w_ref, mu_ref, cols, dst):
        z = _dot(u, w_ref[:, cols])
        c = z[PROJ_HALO:PROJ_HALO + tm]
        up = pltpu.roll(z, 1, 0)[PROJ_HALO:PROJ_HALO + tm]
        dn = pltpu.roll(z, ext - 1, 0)[PROJ_HALO:PROJ_HALO + tm]
        z_ref[:, dst:dst + cols.stop - cols.start] = (
            c + mu_ref[0:1, cols] * (up - c) + mu_ref[1:2, cols] * (dn - c)).astype(BF16)

    for c0 in range(0, 3 * RW_W, RW_W):
        shifted(wrkv_ref, mu_rkv_ref, slice(c0, c0 + RW_W), c0)
    shifted(wzl_ref, mu_zl_ref, slice(0, RW_LORA_W), 3 * RW_W)


def _in_proj(x, g, wq, wkv, wrkv, wzl, mu_rkv, mu_zl, b, t):
    n = x.shape[0]
    tm = PROJ_TILE
    nj = t // tm
    per = tm // PROJ_HALO
    nhalo = n // PROJ_HALO

    def prev_map(i, j):
        return (jnp.maximum((i * nj + j) * per - 1, 0), 0)

    def next_map(i, j):
        return (jnp.minimum((i * nj + j + 1) * per, nhalo - 1), 0)

    row = lambda w: pl.BlockSpec((tm, w), lambda i, j: (i * nj + j, 0))
    consts = [g, wq, wkv, wrkv, wzl, mu_rkv, mu_zl]
    return pl.pallas_call(
        _in_proj_kernel,
        grid=(b, nj),
        in_specs=[row(D_MODEL), pl.BlockSpec((PROJ_HALO, D_MODEL), prev_map),
                  pl.BlockSpec((PROJ_HALO, D_MODEL), next_map)] + [_const_spec(c.shape) for c in consts],
        out_specs=[row(ATT_W), row(2 * ATT_KV_W), row(RW_MIX_W)],
        out_shape=[jax.ShapeDtypeStruct((n, ATT_W), BF16),
                   jax.ShapeDtypeStruct((n, 2 * ATT_KV_W), BF16),
                   jax.ShapeDtypeStruct((n, RW_MIX_W), BF16)],
        compiler_params=_params(2),
        name="in_proj",
    )(x, x, x, *consts)


def _attn_bias():
    slopes = np.exp2(-8.0 / ATT_H * np.arange(1, ATT_H + 1, dtype=np.float64))
    dist = np.abs(np.arange(BLK)[:, None] - (np.arange(3 * BLK)[None, :] - BLK))
    bias = -(slopes[:, None, None] * dist[None])
    bias = np.where(dist[None] <= WINDOW, bias, -np.inf)
    bias = bias.reshape(ATT_KV, ATT_G * BLK, 3 * BLK).transpose(0, 2, 1)
    return jnp.asarray(bias, F32)


def _attn_kernel(sink_ref, q_ref, kvp_ref, kvc_ref, kvn_ref, bias_ref, o_ref):
    j = pl.program_id(1)
    nj = pl.num_programs(1)
    has_next = jnp.where(j < nj - 1, 1.0, 0.0).astype(F32)
    kvwin = jnp.concatenate([kvp_ref[...].astype(F32), kvc_ref[...].astype(F32),
                             kvn_ref[...].astype(F32) * has_next], axis=0)
    kwin = kvwin[:, :ATT_KV_W].astype(BF16)
    v_t = kvwin[:, ATT_KV_W:].T.astype(BF16)
    krow = lax.broadcasted_iota(jnp.int32, (3 * BLK, 1), 0)
    n_qb = ATT_TILE // BLK
    gw = ATT_G * ATT_HD
    no_q = jnp.zeros((ATT_HD, ATT_G * BLK), F32)

    pairs = [(qb, kv) for qb in range(n_qb) for kv in range(ATT_KV)]
    every = range(len(pairs))

    def scores(x):
        qb, kv = pairs[x]
        q_t = q_ref[qb * BLK:(qb + 1) * BLK, kv * gw:(kv + 1) * gw].astype(F32).T
        q_t = jnp.concatenate([q_t[g * ATT_HD:(g + 1) * ATT_HD] for g in range(ATT_G)], axis=1)
        q_t = jnp.concatenate([q_t, no_q] if kv == 0 else [no_q, q_t], axis=0).astype(BF16)
        return _dot(kwin[qb * BLK:qb * BLK + 3 * BLK], q_t)

    def softmax(x, s):
        qb, kv = pairs[x]
        key_ok = krow >= jnp.maximum(2 - (j * n_qb + qb), 0) * BLK
        s = jnp.where(key_ok, s + bias_ref[kv], -jnp.inf)
        sink = jnp.concatenate(
            [jnp.full((1, BLK), sink_ref[kv * ATT_G + g], F32) for g in range(ATT_G)], axis=1)
        mx = jnp.maximum(jnp.max(s, axis=0, keepdims=True), sink)
        p = jnp.exp(s - mx)
        den = jnp.sum(p, axis=0, keepdims=True) + jnp.exp(sink - mx)
        return p.astype(BF16), 1.0 / den

    def values(x, p, inv_den):
        qb, kv = pairs[x]
        o_t = _dot(v_t[kv * ATT_HD:(kv + 1) * ATT_HD, qb * BLK:qb * BLK + 3 * BLK], p) * inv_den
        o_t = jnp.concatenate([o_t[:, g * BLK:(g + 1) * BLK] for g in range(ATT_G)], axis=0)
        o_ref[qb * BLK:(qb + 1) * BLK, kv * gw:(kv + 1) * gw] = o_t.T.astype(BF16)

    s_all = [scores(x) for x in every]
    p_all = [softmax(x, s_all[x]) for x in every]
    for x in every:
        values(x, *p_all[x])


def _attention(q, kv, sink, bias, b, t):
    n = q.shape[0]
    tq = ATT_TILE
    nj = t // tq
    per = tq // BLK
    nblk = n // BLK

    def prev_map(i, j):
        return (jnp.maximum((i * nj + j) * per - 1, 0), 0)

    def next_map(i, j):
        return (jnp.minimum((i * nj + j + 1) * per, nblk - 1), 0)

    return pl.pallas_call(
        _attn_kernel,
        grid=(b, nj),
        in_specs=[pl.BlockSpec(memory_space=pltpu.SMEM),
                  pl.BlockSpec((tq, ATT_W), lambda i, j: (i * nj + j, 0)),
                  pl.BlockSpec((BLK, 2 * ATT_KV_W), prev_map),
                  pl.BlockSpec((tq, 2 * ATT_KV_W), lambda i, j: (i * nj + j, 0)),
                  pl.BlockSpec((BLK, 2 * ATT_KV_W), next_map),
                  _const_spec(bias.shape)],
        out_specs=pl.BlockSpec((tq, ATT_W), lambda i, j: (i * nj + j, 0)),
        out_shape=jax.ShapeDtypeStruct((n, ATT_W), BF16),
        compiler_params=_params(2),
        name="attention",
    )(sink, q, kv, kv, kv, bias)


def _block_diag(x, mask):
    tiled = jnp.concatenate([x.astype(F32)] * GROUP_H, axis=0)
    return jnp.where(mask, tiled, 0.0).astype(BF16)


def _head_sum(x):
    lanes = 2 * RW_N
    low = lax.broadcasted_iota(jnp.int32, (x.shape[0], lanes), 1) < RW_N
    out = []
    for c0 in range(0, RW_W, lanes):
        xc = x[:, c0:c0 + lanes]
        lo = jnp.sum(jnp.where(low, xc, 0.0), axis=1, keepdims=True)
        hi = jnp.sum(jnp.where(low, 0.0, xc), axis=1, keepdims=True)
        out.append(jnp.where(low, lo, hi))
    return jnp.concatenate(out, axis=1)


def _rwkv_kernel(reverse, final, *refs):
    if final:
        (rkv_ref, w0, w2, a0, a2, k_k, k_a,
         y0, a0o, a2o, g2, r_k, ln_w, ln_b, out_ref, s_ref, y_s) = refs
    else:
        (rkv_ref, w0, w2, a0, a2, k_k, k_a, out_ref, s_ref, y_s) = refs
    lora0 = 3 * RW_W

    j = pl.program_id(1)

    @pl.when(j == 0)
    def _():
        s_ref[...] = jnp.zeros_like(s_ref)

    rid = lax.broadcasted_iota(jnp.int32, (CHUNK, CHUNK), 0)
    cid = lax.broadcasted_iota(jnp.int32, (CHUNK, CHUNK), 1)
    lmat = jnp.where((cid >= rid) if reverse else (cid <= rid), 1.0, 0.0).astype(BF16)

    def prepare(chunks, res):
        lo = min(chunks) * CHUNK
        rows = slice(lo, (max(chunks) + 1) * CHUNK)
        r = rkv_ref[rows, :RW_W].astype(F32)
        k = rkv_ref[rows, RW_W:2 * RW_W].astype(F32)
        v_bf = rkv_ref[rows, 2 * RW_W:lora0]
        wd = rkv_ref[rows, lora0:lora0 + 2 * DECAY_LORA].astype(F32)
        ad = rkv_ref[rows, lora0 + 2 * DECAY_LORA:lora0 + 2 * DECAY_LORA + 2 * AAA_LORA]
        w = w0[...] + _dot(jnp.tanh(wd).astype(BF16), w2[...])
        lw = -math.exp(-0.5) * _sigmoid(w)
        a = _sigmoid(a0[...] + _dot(ad, a2[...]))
        yield
        lw_hi = lw.astype(BF16)
        lw_lo = (lw - lw_hi.astype(F32)).astype(BF16)
        cum = jnp.concatenate(
            [_dot(lmat, lw_hi[c0:c0 + CHUNK]) + _dot(lmat, lw_lo[c0:c0 + CHUNK])
             for c0 in range(0, len(chunks) * CHUNK, CHUNK)], axis=0)
        yield
        kkk = k * k_k[...]
        kk = kkk / jnp.maximum(jnp.sqrt(_head_sum(kkk * kkk)), 1e-12)
        kdir = k * (1.0 + (a - 1.0) * k_a[...])
        bb = kk * a
        yield
        res["at"] = (-kk * jnp.exp(cum - lw)).astype(BF16)
        res["rt"] = (r * jnp.exp(cum)).astype(BF16)
        yield
        e_neg = jnp.exp(-cum)
        res["bt"] = (bb * e_neg).astype(BF16)
        res["kt"] = (kdir * e_neg).astype(BF16)
        res["v"] = v_bf
        yield
        bp, kp = [], []
        for ci in sorted(chunks):
            local = slice(ci * CHUNK - lo, (ci + 1) * CHUNK - lo)
            end = local.start if reverse else local.stop - 1
            cum_end = cum[end:end + 1, :]
            e_end = jnp.exp(cum_end - cum[local])
            bp.append((bb[local] * e_end).astype(BF16))
            kp.append((kdir[local] * e_end).astype(BF16))
            res["pc", ci] = jnp.exp(cum_end)
        res["bp"] = jnp.concatenate(bp, axis=0)
        res["kp"] = jnp.concatenate(kp, axis=0)
        yield
        if final:
            a_other = _sigmoid(a0o[...] + _dot(ad, a2o[...]))
            ksum = k * (2.0 + (a + a_other - 2.0) * k_a[...])
            res["bonus"] = _head_sum(r * ksum * r_k[...]) * v_bf.astype(F32)
            yield
            gd = rkv_ref[rows, lora0 + 2 * DECAY_LORA + 2 * AAA_LORA:].astype(F32)
            res["gate"] = _dot(_sigmoid(gd).astype(BF16), g2[...])
            yield

    trow = lax.broadcasted_iota(jnp.int32, (CHUNK, GROUP_W), 0)
    scol = lax.broadcasted_iota(jnp.int32, (CHUNK, GROUP_W), 1) % CHUNK
    strict = (scol > trow) if reverse else (scol < trow)
    incl = (scol >= trow) if reverse else (scol <= trow)
    eye = jnp.where(scol == trow, 1.0, 0.0).astype(F32)
    bdr = lax.broadcasted_iota(jnp.int32, (GROUP_H * CHUNK, GROUP_W), 0) // CHUNK
    bdc = lax.broadcasted_iota(jnp.int32, (GROUP_H * CHUNK, GROUP_W), 1) // RW_N
    bdmask = bdr == bdc
    bd = functools.partial(_block_diag, mask=bdmask)

    def solve(chunks, tok, res):
        lo = min(chunks) * CHUNK
        pairs = [(ci, g) for ci in chunks for g in range(N_GROUPS)]

        def ld(name):
            return {(ci, g): tok[name][ci * CHUNK - lo:(ci + 1) * CHUNK - lo, g * GROUP_W:(g + 1) * GROUP_W]
                    for ci, g in pairs}

        def stage(fn):
            return {x: fn(x) for x in pairs}

        a_t, r_t, v_c, b_t, k_t = ld("at"), ld("rt"), ld("v"), ld("bt"), ld("kt")
        ar = stage(lambda x: jnp.concatenate([a_t[x], r_t[x]], axis=0))
        g1 = stage(lambda x: _dot_nt(ar[x], bd(b_t[x])))
        yield
        g2_ = stage(lambda x: _dot_nt(ar[x], bd(k_t[x])))
        yield
        ab = stage(lambda x: jnp.where(strict, g1[x][:CHUNK], 0.0))
        rb = stage(lambda x: jnp.where(incl, g1[x][CHUNK:], 0.0))
        ak = stage(lambda x: jnp.where(strict, g2_[x][:CHUNK], 0.0))
        rk = stage(lambda x: jnp.where(incl, g2_[x][CHUNK:], 0.0))

        t_inv = stage(lambda x: eye + ab[x])
        pw = stage(lambda x: _dot(ab[x].astype(BF16), bd(ab[x])))
        yield
        n_lvl = int(math.log2(CHUNK))
        for lvl in range(1, n_lvl):
            if lvl < n_lvl - 1:
                both = stage(lambda x: _dot(jnp.concatenate([pw[x], t_inv[x]], axis=0).astype(BF16),
                                            bd(pw[x])))
                pw = stage(lambda x: both[x][:CHUNK])
                t_inv = stage(lambda x, t=t_inv: t[x] + both[x][CHUNK:])
            else:
                t_inv = stage(lambda x, t=t_inv: t[x] + _dot(t[x].astype(BF16), bd(pw[x])))
            yield

        res["wv"] = stage(lambda x: _dot(ak[x].astype(BF16), bd(v_c[x])))
        b_p, k_p = ld("bp"), ld("kp")
        res["ar"] = ar
        res["t"] = stage(lambda x: t_inv[x].astype(BF16))
        res["rbk"] = stage(lambda x: jnp.concatenate([rb[x], rk[x]], axis=1).astype(BF16))
        res["v"] = v_c
        res["bkp"] = stage(lambda x: jnp.concatenate([b_p[x], k_p[x]], axis=0))
        yield

    def carry(chunks, tok, res, state):
        groups = range(N_GROUPS)
        for ci in chunks:
            a_s = [_dot_nt(res["ar"][ci, g], state[g].astype(BF16)) for g in groups]
            yield
            u = [_dot(res["t"][ci, g], bd(a_s[g][:CHUNK] + res["wv"][ci, g])) for g in groups]
            yield
            for g in groups:
                x = (ci, g)
                lanes = slice(g * GROUP_W, (g + 1) * GROUP_W)
                y_s[ci * CHUNK:(ci + 1) * CHUNK, lanes] = a_s[g][CHUNK:] + _dot(
                    res["rbk"][x], jnp.concatenate([bd(u[g]), bd(res["v"][x])], axis=0))
                upd = _dot_tn(jnp.concatenate([u[g].astype(BF16), res["v"][x]], axis=0), res["bkp"][x])
                state[g] = state[g] * tok["pc", ci][:, lanes] + jnp.where(bdmask, upd, 0.0)
            yield

    def run(gen, *fill):
        for _ in gen:
            for f in fill:
                next(f, None)
        for f in fill:
            for _ in f:
                pass

    n_ch = RW_TILE // CHUNK
    scan = [n_ch - 1 - s if reverse else s for s in range(n_ch)]
    per = n_ch // RW_PARTS
    parts = [scan[p * per:(p + 1) * per] for p in range(RW_PARTS)]
    tok = [{} for _ in parts]
    sol = [{} for _ in parts]
    def finish(p):
        rows = slice(min(parts[p]) * CHUNK, (max(parts[p]) + 1) * CHUNK)
        y = y_s[rows, :]
        if not final:
            out_ref[rows, :] = y
        else:
            ytot = y + y0[rows, :]
            inv_n = 1.0 / RW_N
            mu = _head_sum(ytot) * inv_n
            d = ytot - mu
            var = _head_sum(d * d) * inv_n
            yn = d * lax.rsqrt(var + GN_EPS) * ln_w[...] + ln_b[...]
            out_ref[rows, :] = ((yn + tok[p]["bonus"]) * tok[p]["gate"]).astype(BF16)
        yield

    state = [s_ref[g] for g in range(N_GROUPS)]
    run(prepare(parts[0], tok[0]))
    for p in range(RW_PARTS):
        fill = []
        if p > 0:
            fill.append(carry(parts[p - 1], tok[p - 1], sol[p - 1], state))
        if p + 1 < RW_PARTS:
            fill.append(prepare(parts[p + 1], tok[p + 1]))
        if p > 1:
            fill.append(finish(p - 2))
        run(solve(parts[p], tok[p], sol[p]), *fill)
    run(carry(parts[-1], tok[-1], sol[-1], state))
    for g in range(N_GROUPS):
        s_ref[g] = state[g]
    for p in range(max(RW_PARTS - 2, 0), RW_PARTS):
        run(finish(p))


def _rwkv_pass(reverse, rkv, consts, extra, b, t):
    n = rkv.shape[0]
    tb = RW_TILE
    nj = t // tb
    final = reverse

    def cur_map(i, j):
        return (i * nj + (nj - 1 - j if reverse else j), 0)

    in_specs = [pl.BlockSpec((tb, RW_MIX_W), cur_map)] + [_const_spec(c.shape) for c in consts]
    args = [rkv] + list(consts)
    if final:
        y0 = extra[0]
        in_specs += [pl.BlockSpec((tb, RW_W), cur_map)] + [_const_spec(c.shape) for c in extra[1:]]
        args += list(extra)
    out_dtype = BF16 if final else F32
    return pl.pallas_call(
        functools.partial(_rwkv_kernel, reverse, final),
        grid=(b, nj),
        in_specs=in_specs,
        out_specs=pl.BlockSpec((tb, RW_W), cur_map),
        out_shape=jax.ShapeDtypeStruct((n, RW_W), out_dtype),
        scratch_shapes=[pltpu.VMEM((N_GROUPS, GROUP_W, GROUP_W), F32),
                        pltpu.VMEM((tb, RW_W), F32)],
        compiler_params=_params(2),
        name="rwkv_bwd" if reverse else "rwkv_fwd",
    )(*args)


def _merge_kernel(x_ref, oa_ref, ob_ref, gpre_ref, wg_ref, wa_ref, wb_ref, wo_ref, gpost_ref, h_ref):
    x = x_ref[...]
    u = _rms(x, gpre_ref[...]).astype(BF16)
    gates = _sigmoid(_dot(u, wg_ref[...]))
    merged = (gates[:, :D_MODEL] * _dot(oa_ref[...], wa_ref[...])
              + gates[:, D_MODEL:] * _dot(ob_ref[...], wb_ref[...]))
    m = _dot(merged.astype(BF16), wo_ref[...])
    h_ref[...] = x + _rms(m, gpost_ref[...])


def _merge(x, oa, ob, gpre, wg, wa, wb, wo, gpost):
    n = x.shape[0]
    tm = MERGE_TILE
    row = lambda w: pl.BlockSpec((tm, w), lambda i: (i, 0))
    consts = [gpre, wg, wa, wb, wo, gpost]
    return pl.pallas_call(
        _merge_kernel,
        grid=(n // tm,),
        in_specs=[row(D_MODEL), row(ATT_W), row(RW_W)] + [_const_spec(c.shape) for c in consts],
        out_specs=row(D_MODEL),
        out_shape=jax.ShapeDtypeStruct((n, D_MODEL), F32),
        compiler_params=_params(1),
        name="merge",
    )(x, oa, ob, *consts)


def _gelu_tanh(x):
    return 0.5 * x * (1.0 + jnp.tanh(math.sqrt(2.0 / math.pi) * (x + 0.044715 * (x * x * x))))


def _ffn_kernel(hc_ref, hp_ref, hn_ref, gpre_ref, wup_ref, cw_ref, cb_ref, wdn_ref, gpost_ref, o_ref,
                slab_s):
    j = pl.program_id(1)
    nj = pl.num_programs(1)
    tm = FFN_TILE
    sub = FFN_HALO
    nb = tm // FFN_SPAN
    lane_w = 128
    n_slab = D_MODEL // lane_w

    for c in range(n_slab):
        slab_s[c] = hc_ref[:, c * lane_w:(c + 1) * lane_w]
    hc = jnp.concatenate(
        [jnp.concatenate([slab_s[c, pl.ds(blk * FFN_SPAN + v, sub, stride=sub), :] for c in range(n_slab)],
                         axis=1)
         for blk in range(nb) for v in range(sub)], axis=0)

    sid = lax.broadcasted_iota(jnp.int32, (sub, 1), 0)
    is_before = jnp.logical_and(sid == 0, j > 0)
    is_after = jnp.logical_and(sid == sub - 1, j < nj - 1)
    edge = jnp.where(is_before, hp_ref[sub - 1:sub, :], 0.0) + jnp.where(is_after, hn_ref[0:1, :], 0.0)
    u_edge = jnp.where(jnp.logical_or(is_before, is_after), _rms(edge, gpre_ref[...]), 0.0)
    u = jnp.concatenate([_rms(hc, gpre_ref[...]), u_edge], axis=0).astype(BF16)

    def up(c0):
        return [_dot(u, wup_ref[:, off:off + FFN_COLS]) for off in (c0, D_FF + c0)]

    def conv(hh, off):
        cols = slice(off, off + FFN_COLS)
        grp = lambda blk, v: hh[(blk * sub + v) * sub:(blk * sub + v + 1) * sub]
        edge_h = hh[tm:]
        down = [pltpu.roll(grp(blk, sub - 1), 1, 0) for blk in range(nb)]
        upw = [pltpu.roll(grp(blk, 0), sub - 1, 0) for blk in range(nb)]
        prev, nxt = [], []
        for blk in range(nb):
            first = jnp.where(sid == 0, down[blk - 1] if blk > 0 else edge_h, down[blk])
            last = jnp.where(sid == sub - 1, upw[blk + 1] if blk + 1 < nb else edge_h, upw[blk])
            lo = blk * FFN_SPAN
            prev += [first, hh[lo:lo + FFN_SPAN - sub]]
            nxt += [hh[lo + sub:lo + FFN_SPAN], last]
        prev = jnp.concatenate(prev, axis=0)
        nxt = jnp.concatenate(nxt, axis=0)
        return (prev * cw_ref[0:1, cols] + hh[:tm] * cw_ref[1:2, cols]
                + nxt * cw_ref[2:3, cols] + cb_ref[:, cols])

    starts = list(range(0, D_FF, FFN_COLS))
    pending = up(starts[0])
    acts = []
    for i, c0 in enumerate(starts):
        hh = pending
        if i + 1 < len(starts):
            pending = up(starts[i + 1])
        acts.append((_gelu_tanh(conv(hh[0], c0)) * conv(hh[1], D_FF + c0)).astype(BF16))
    f = _dot(jnp.concatenate(acts, axis=1), wdn_ref[...])
    out = hc + _rms(f, gpost_ref[...])
    for r in range(tm // sub):
        for c in range(n_slab):
            slab_s[c, pl.ds((r // sub) * FFN_SPAN + r % sub, sub, stride=sub), :] = (
                out[r * sub:(r + 1) * sub, c * lane_w:(c + 1) * lane_w])
    for c in range(n_slab):
        o_ref[:, c * lane_w:(c + 1) * lane_w] = slab_s[c]


def _ffn(h, gpre, wup, cw, cb, wdn, gpost, b, t):
    n = h.shape[0]
    tm = FFN_TILE
    nj = t // tm
    per = tm // FFN_HALO
    nhalo = n // FFN_HALO

    def prev_map(i, j):
        return (jnp.maximum((i * nj + j) * per - 1, 0), 0)

    def next_map(i, j):
        return (jnp.minimum((i * nj + j + 1) * per, nhalo - 1), 0)

    consts = [gpre, wup, cw, cb, wdn, gpost]
    return pl.pallas_call(
        _ffn_kernel,
        grid=(b, nj),
        in_specs=[pl.BlockSpec((tm, D_MODEL), lambda i, j: (i * nj + j, 0)),
                  pl.BlockSpec((FFN_HALO, D_MODEL), prev_map),
                  pl.BlockSpec((FFN_HALO, D_MODEL), next_map)]
                 + [_const_spec(c.shape) for c in consts],
        out_specs=pl.BlockSpec((tm, D_MODEL), lambda i, j: (i * nj + j, 0)),
        out_shape=jax.ShapeDtypeStruct((n, D_MODEL), F32),
        scratch_shapes=[pltpu.VMEM((D_MODEL // 128, tm, 128), F32)],
        compiler_params=_params(2),
        name="ffn",
    )(h, h, h, *consts)


def _prepare(norm_mix_pre, norm_mix_post, norm_ffn_pre, norm_ffn_post, w_in, attn_sink,
             rw_mu_prev, rw_mu_next, rw_w0, rw_w2, rw_a0, rw_a2, rw_g2, rw_k_k, rw_k_a,
             rw_r_k, rw_ln_w, rw_ln_b, w_branch_attn, w_branch_rwkv, w_out,
             w_ffn_up, ffn_conv_w, ffn_conv_b, w_ffn_down):
    c_q = ATT_W
    c_kv = c_q + 2 * ATT_KV_W
    c_rkv = c_kv + 3 * RW_W
    c_zl = c_rkv + RW_LORA_W
    row = lambda p: p.reshape(1, -1).astype(F32)

    def lora_pad(w2, d, n_lora):
        z = jnp.zeros((2 * n_lora, RW_W), F32)
        return z.at[d * n_lora:(d + 1) * n_lora].set(w2[d]).astype(BF16)

    mu = jnp.stack([rw_mu_prev, rw_mu_next]).astype(F32)
    p = dict(
        g_mix_pre=row(norm_mix_pre), g_mix_post=row(norm_mix_post),
        g_ffn_pre=row(norm_ffn_pre), g_ffn_post=row(norm_ffn_post),
        wq=w_in[:, :c_q].astype(BF16), wkv=w_in[:, c_q:c_kv].astype(BF16),
        wrkv=w_in[:, c_kv:c_rkv].astype(BF16), wzl=w_in[:, c_rkv:c_zl].astype(BF16),
        wg=w_in[:, c_zl:].astype(BF16),
        sink=attn_sink.astype(F32), bias=_attn_bias(),
        mu_rkv=mu[:, :3 * RW_W], mu_zl=mu[:, 3 * RW_W:],
        w0=[row(rw_w0[d]) for d in range(2)],
        w2=[lora_pad(rw_w2, d, DECAY_LORA) for d in range(2)],
        a0=[row(rw_a0[d]) for d in range(2)],
        a2=[lora_pad(rw_a2, d, AAA_LORA) for d in range(2)],
        g2=rw_g2.astype(BF16), k_k=row(rw_k_k), k_a=row(rw_k_a), r_k=row(rw_r_k),
        ln_w=row(rw_ln_w), ln_b=row(rw_ln_b),
        wa=w_branch_attn.astype(BF16), wb=w_branch_rwkv.astype(BF16), wo=w_out.astype(BF16),
        wup=w_ffn_up.astype(BF16), cw=ffn_conv_w.astype(F32), cb=row(ffn_conv_b),
        wdn=w_ffn_down.astype(BF16),
    )
    return p


def _layer(x, p):
    b, t, _ = x.shape
    assert t % RW_TILE == 0 and t % ATT_TILE == 0 and t % FFN_TILE == 0
    assert (b * t) % PROJ_TILE == 0 and (b * t) % MERGE_TILE == 0
    x2 = x.reshape(b * t, D_MODEL)
    q, kv, rkv = _in_proj(x2, p["g_mix_pre"], p["wq"], p["wkv"], p["wrkv"], p["wzl"],
                          p["mu_rkv"], p["mu_zl"], b, t)
    o_attn = _attention(q, kv, p["sink"], p["bias"], b, t)

    def consts(d):
        return [p["w0"][d], p["w2"][d], p["a0"][d], p["a2"][d],
                p["k_k"], p["k_a"]]

    y_fwd = _rwkv_pass(False, rkv, consts(0), None, b, t)
    o_rwkv = _rwkv_pass(True, rkv, consts(1),
                        [y_fwd, p["a0"][0], p["a2"][0], p["g2"], p["r_k"], p["ln_w"], p["ln_b"]], b, t)
    h = _merge(x2, o_attn, o_rwkv, p["g_mix_pre"], p["wg"], p["wa"], p["wb"], p["wo"], p["g_mix_post"])
    out = _ffn(h, p["g_ffn_pre"], p["wup"], p["cw"], p["cb"], p["wdn"], p["g_ffn_post"], b, t)
    return out.reshape(b, t, D_MODEL)


def kernel(x_prompt, x_sample, norm_mix_pre, norm_mix_post, norm_ffn_pre, norm_ffn_post, w_in, attn_sink, rw_mu_prev, rw_mu_next, rw_w0, rw_w2, rw_a0, rw_a2, rw_g2, rw_k_k, rw_k_a, rw_r_k, rw_ln_w, rw_ln_b, w_branch_attn, w_branch_rwkv, w_out, w_ffn_up, ffn_conv_w, ffn_conv_b, w_ffn_down):
    weights = (norm_mix_pre, norm_mix_post, norm_ffn_pre, norm_ffn_post, w_in, attn_sink,
               rw_mu_prev, rw_mu_next, rw_w0, rw_w2, rw_a0, rw_a2, rw_g2, rw_k_k, rw_k_a,
               rw_r_k, rw_ln_w, rw_ln_b, w_branch_attn, w_branch_rwkv, w_out,
               w_ffn_up, ffn_conv_w, ffn_conv_b, w_ffn_down)
    depth = w_in.shape[0]
    layers = [_prepare(*(w[l] for w in weights)) for l in range(depth)]

    def run(x):
        for p in layers:
            x = _layer(x, p)
        return x

    return (run(x_prompt), run(x_sample))
```
